```python
import jax, jax.numpy as jnp
from jax import lax
import numpy as np

D_MODEL = 1024
BATCH = 16
SEQ = 2048
DEPTH = 4

N_MIXERS = 2
HEAD_DIM = 64
N_HEADS = D_MODEL // HEAD_DIM
N_KV = 4
GQA = N_HEADS // N_KV
ROT_DIM = HEAD_DIM // 4
ROPE_THETA = 500000.0
NORM_EPS = 1e-6
Q_W = N_HEADS * HEAD_DIM
KV_W = N_KV * HEAD_DIM

CMP_LEN = 32
CMP_STRIDE = 16
CMP_HID = 4 * HEAD_DIM
SEL_BLOCK = 64
SEL_COUNT = 8
WINDOW = 256
NSA_Q_BLOCK = 32
NSA_IN = 2 * Q_W + 6 * KV_W + 3 * N_HEADS

MOBA_BLOCK = 256
MOBA_TOPK = 3
MOBA_Q_BLOCK = 8
MOBA_IN = 2 * Q_W + 2 * KV_W

N_A = (DEPTH + 1) // 2
N_B = DEPTH // 2

kernel_name = "nsa_moba_hybrid_adaln_trunk"


def rms_norm(x, g):
    xf = x.astype(jnp.float32)
    y = xf * lax.rsqrt(jnp.mean(xf * xf, axis=-1, keepdims=True) + NORM_EPS)
    return (y * g.astype(jnp.float32)).astype(x.dtype)


def rope_tables(pos):
    inv_freq = ROPE_THETA ** (-jnp.arange(0, ROT_DIM, 2, dtype=jnp.float32) / ROT_DIM)
    ang = pos.astype(jnp.float32)[..., None] * inv_freq
    return jnp.cos(ang), jnp.sin(ang)


def partial_rope(x, cos, sin):
    half = ROT_DIM // 2
    c = cos[:, :, None, :].astype(x.dtype)
    s = sin[:, :, None, :].astype(x.dtype)
    x1, x2, rest = x[..., :half], x[..., half:ROT_DIM], x[..., ROT_DIM:]
    return jnp.concatenate([x1 * c - x2 * s, x2 * c + x1 * s, rest], axis=-1)


def masked_softmax(s, mask):
    s = jnp.where(mask, s.astype(jnp.float32), -jnp.inf)
    m = jnp.max(s, axis=-1, keepdims=True)
    e = jnp.exp(s - jnp.where(jnp.isfinite(m), m, 0.0))
    return e / jnp.maximum(jnp.sum(e, axis=-1, keepdims=True), 1e-30)


def split_cols(a, widths):
    return jnp.split(a, np.cumsum(widths)[:-1].tolist(), axis=-1)


def cmp_to_sel_weights(n_cmp, n_sel):
    cs = np.arange(n_cmp)[:, None] * CMP_STRIDE
    ss = np.arange(n_sel)[None, :] * SEL_BLOCK
    shared = np.clip(np.minimum(cs + CMP_LEN, ss + SEL_BLOCK) - np.maximum(cs, ss), 0, None)
    return jnp.asarray(shared / CMP_LEN, dtype=jnp.float32)


def nsa_mixer(h, cos, sin, positions, w_in, w_out, q_g, k_g, cmp_pe, cmp_w1, cmp_b1, cmp_w2, gate_b):
    B, S, _ = h.shape
    q, kc, vc, ks, vs, kw, vw, gl, z = split_cols(h @ w_in, [Q_W] + [KV_W] * 6 + [3 * N_HEADS, Q_W])
    q = partial_rope(rms_norm(q.reshape(B, S, N_HEADS, HEAD_DIM), q_g), cos, sin)
    q = q.reshape(B, S, N_KV, GQA, HEAD_DIM)
    gates = jax.nn.sigmoid(gl + gate_b).reshape(B, S, N_KV, GQA, 3)
    scale = HEAD_DIM ** -0.5

    n_cmp = (S - CMP_LEN) // CMP_STRIDE + 1
    blk_idx = np.arange(n_cmp)[:, None] * CMP_STRIDE + np.arange(CMP_LEN)[None, :]
    cmp_end = blk_idx[:, -1]

    def compress(t, i):
        blocks = t.reshape(B, S, N_KV, HEAD_DIM)[:, blk_idx] + cmp_pe[i][None, None, :, None, :]
        flat = blocks.transpose(0, 1, 3, 2, 4).reshape(B, n_cmp, N_KV, CMP_LEN * HEAD_DIM)
        return jax.nn.gelu(flat @ cmp_w1[i] + cmp_b1[i]) @ cmp_w2[i]

    cos_c, sin_c = rope_tables(positions[:, cmp_end])
    k_cmp = partial_rope(rms_norm(compress(kc, 0), k_g[0]), cos_c, sin_c)
    v_cmp = compress(vc, 1)
    cmp_end_j = jnp.asarray(cmp_end, dtype=jnp.int32)

    n_sel = S // SEL_BLOCK
    k_sel = min(SEL_COUNT, n_sel)
    sel_w = cmp_to_sel_weights(n_cmp, n_sel)
    ks = partial_rope(rms_norm(ks.reshape(B, S, N_KV, HEAD_DIM), k_g[1]), cos, sin)
    ks_blk = ks.reshape(B, n_sel, SEL_BLOCK, N_KV, HEAD_DIM).transpose(0, 3, 1, 2, 4)
    vs_blk = vs.reshape(B, n_sel, SEL_BLOCK, N_KV, HEAD_DIM).transpose(0, 3, 1, 2, 4)

    pad = ((0, 0), (WINDOW, 0), (0, 0), (0, 0))
    kw_pad = jnp.pad(partial_rope(rms_norm(kw.reshape(B, S, N_KV, HEAD_DIM), k_g[2]), cos, sin), pad)
    vw_pad = jnp.pad(vw.reshape(B, S, N_KV, HEAD_DIM), pad)

    b_i = jnp.arange(B)[:, None, None, None]
    g_i = jnp.arange(N_KV)[None, None, :, None]
    sel_blocks = jnp.arange(n_sel)
    C = NSA_Q_BLOCK

    def q_block(ci):
        t0 = ci * C
        t = t0 + jnp.arange(C)
        qc = lax.dynamic_slice_in_dim(q, t0, C, axis=1)
        gc = lax.dynamic_slice_in_dim(gates, t0, C, axis=1)
        m_cmp = (cmp_end_j[None, :] <= t[:, None])[None, :, None, None, :]
        p_cmp = masked_softmax(jnp.einsum('bcgrd,bngd->bcgrn', qc, k_cmp) * scale, m_cmp)
        o_cmp = jnp.einsum('bcgrn,bngd->bcgrd', p_cmp.astype(v_cmp.dtype), v_cmp)
        imp = jnp.einsum('bcgrn,nj->bcgj', p_cmp, sel_w)
        cur = (t // SEL_BLOCK)[:, None]
        valid = sel_blocks[None, :] <= cur
        forced = (sel_blocks[None, :] == 0) | (sel_blocks[None, :] == cur) | (sel_blocks[None, :] == cur - 1)
        score = jnp.where(forced[None, :, None, :], jnp.inf,
                          jnp.where(valid[None, :, None, :], imp, -jnp.inf))
        _, sel = lax.top_k(score, k_sel)
        kg_sel = ks_blk[b_i, g_i, sel].reshape(B, C, N_KV, k_sel * SEL_BLOCK, HEAD_DIM)
        vg_sel = vs_blk[b_i, g_i, sel].reshape(B, C, N_KV, k_sel * SEL_BLOCK, HEAD_DIM)
        kpos = (sel[..., None] * SEL_BLOCK + jnp.arange(SEL_BLOCK)).reshape(B, C, N_KV, 1, k_sel * SEL_BLOCK)
        m_slc = kpos <= t[None, :, None, None, None]
        p_slc = masked_softmax(jnp.einsum('bcgrd,bcgnd->bcgrn', qc, kg_sel) * scale, m_slc)
        o_slc = jnp.einsum('bcgrn,bcgnd->bcgrd', p_slc.astype(vg_sel.dtype), vg_sel)
        kwc = lax.dynamic_slice_in_dim(kw_pad, t0, C + WINDOW, axis=1)
        vwc = lax.dynamic_slice_in_dim(vw_pad, t0, C + WINDOW, axis=1)
        kp = t0 - WINDOW + jnp.arange(C + WINDOW)
        dist = t[:, None] - kp[None, :]
        m_win = ((dist >= 0) & (dist < WINDOW) & (kp[None, :] >= 0))[None, :, None, None, :]
        p_win = masked_softmax(jnp.einsum('bcgrd,bkgd->bcgrk', qc, kwc) * scale, m_win)
        o_win = jnp.einsum('bcgrk,bkgd->bcgrd', p_win.astype(vwc.dtype), vwc)
        return gc[..., 0:1] * o_cmp + gc[..., 1:2] * o_slc + gc[..., 2:3] * o_win

    o = lax.map(q_block, jnp.arange(S // C))
    o = jnp.moveaxis(o, 0, 1).reshape(B, S, Q_W)
    return (o * jax.nn.silu(z)) @ w_out


def moba_mixer(h, cos, sin, w_in, w_out, q_g, k_g):
    B, S, _ = h.shape
    q, k, v, z = split_cols(h @ w_in, [Q_W, KV_W, KV_W, Q_W])
    q = partial_rope(rms_norm(q.reshape(B, S, N_HEADS, HEAD_DIM), q_g), cos, sin)
    q = q.reshape(B, S, N_KV, GQA, HEAD_DIM)
    k = partial_rope(rms_norm(k.reshape(B, S, N_KV, HEAD_DIM), k_g), cos, sin)
    v = v.reshape(B, S, N_KV, HEAD_DIM)
    scale = HEAD_DIM ** -0.5

    n_blk = -(-S // MOBA_BLOCK)
    pad = ((0, 0), (0, n_blk * MOBA_BLOCK - S), (0, 0), (0, 0))
    k_pad = jnp.pad(k, pad)
    v_pad = jnp.pad(v, pad)
    k_blk = k_pad.reshape(B, n_blk, MOBA_BLOCK, N_KV, HEAD_DIM)
    k_mean = jnp.mean(k_blk.astype(jnp.float32), axis=2).astype(k.dtype)
    k_blk_t = k_blk.transpose(0, 3, 1, 2, 4)
    v_blk_t = v_pad.reshape(B, n_blk, MOBA_BLOCK, N_KV, HEAD_DIM).transpose(0, 3, 1, 2, 4)
    k_top = min(MOBA_TOPK, n_blk - 1)
    b_i = jnp.arange(B)[:, None, None, None, None]
    g_i = jnp.arange(N_KV)[None, None, :, None, None]
    blk = jnp.arange(n_blk)
    C = MOBA_Q_BLOCK

    def q_block(ci):
        t0 = ci * C
        t = t0 + jnp.arange(C)
        qc = lax.dynamic_slice_in_dim(q, t0, C, axis=1)
        own = t0 // MOBA_BLOCK
        bstart = own * MOBA_BLOCK
        k_own = lax.dynamic_slice_in_dim(k_pad, bstart, MOBA_BLOCK, axis=1)
        v_own = lax.dynamic_slice_in_dim(v_pad, bstart, MOBA_BLOCK, axis=1)
        kpos = bstart + jnp.arange(MOBA_BLOCK)
        m_own = jnp.broadcast_to((kpos[None, :] <= t[:, None])[None, :, None, None, :],
                                 (B, C, N_KV, GQA, MOBA_BLOCK))
        s_own = jnp.einsum('bcgrd,bkgd->bcgrk', qc, k_own) * scale
        if k_top > 0:
            s_gate = jnp.einsum('bcgrd,bjgd->bcgrj', qc, k_mean).astype(jnp.float32)
            s_gate = jnp.where(blk < own, s_gate, -jnp.inf)
            _, sel = lax.top_k(s_gate, k_top)
            kg = k_blk_t[b_i, g_i, sel]
            vg = v_blk_t[b_i, g_i, sel].reshape(B, C, N_KV, GQA, k_top * MOBA_BLOCK, HEAD_DIM)
            s_past = jnp.einsum('bcgrd,bcgrkld->bcgrkl', qc, kg).reshape(B, C, N_KV, GQA, k_top * MOBA_BLOCK) * scale
            m_past = jnp.broadcast_to((sel < own)[..., None],
                                      (B, C, N_KV, GQA, k_top, MOBA_BLOCK)).reshape(B, C, N_KV, GQA, k_top * MOBA_BLOCK)
            p = masked_softmax(jnp.concatenate([s_past, s_own], axis=-1),
                               jnp.concatenate([m_past, m_own], axis=-1)).astype(v.dtype)
            n_past = k_top * MOBA_BLOCK
            return (jnp.einsum('bcgrn,bcgrnd->bcgrd', p[..., :n_past], vg)
                    + jnp.einsum('bcgrk,bkgd->bcgrd', p[..., n_past:], v_own))
        p = masked_softmax(s_own, m_own).astype(v.dtype)
        return jnp.einsum('bcgrk,bkgd->bcgrd', p, v_own)

    o = lax.map(q_block, jnp.arange(S // C))
    o = jnp.moveaxis(o, 0, 1).reshape(B, S, Q_W)
    return (o * jax.nn.silu(z)) @ w_out


def setup_inputs(seed: int = 0) -> dict:
    key = jax.random.key(seed)
    ks = jax.random.split(key, 20)
    D = D_MODEL

    def nrm(k, shape, s):
        return jax.random.normal(k, shape, jnp.float32) * s

    x = nrm(ks[0], (BATCH, SEQ, D), 1.0)
    c = nrm(ks[1], (BATCH, D), 1.0)
    offset = jax.random.randint(ks[2], (BATCH, 1), 0, 4096, dtype=jnp.int32)
    positions = offset + jnp.arange(SEQ, dtype=jnp.int32)[None, :]
    return {
        "x": x,
        "c": c,
        "positions": positions,
        "norm_g": 1.0 + nrm(ks[3], (DEPTH, D), 0.02),
        "ada_w": nrm(ks[4], (DEPTH, D, 3 * D), 0.5 * D ** -0.5),
        "ada_b": nrm(ks[5], (DEPTH, 3 * D), 0.02),
        "nsa_w_in": nrm(ks[6], (N_A, D, NSA_IN), D ** -0.5),
        "nsa_w_out": nrm(ks[7], (N_A, Q_W, D), Q_W ** -0.5),
        "nsa_q_norm": 1.0 + nrm(ks[8], (N_A, HEAD_DIM), 0.02),
        "nsa_k_norm": 1.0 + nrm(ks[9], (N_A, 3, HEAD_DIM), 0.02),
        "nsa_cmp_pe": nrm(ks[10], (N_A, 2, CMP_LEN, HEAD_DIM), 0.1),
        "nsa_cmp_w1": nrm(ks[11], (N_A, 2, CMP_LEN * HEAD_DIM, CMP_HID), (CMP_LEN * HEAD_DIM) ** -0.5),
        "nsa_cmp_b1": nrm(ks[12], (N_A, 2, CMP_HID), 0.02),
        "nsa_cmp_w2": nrm(ks[13], (N_A, 2, CMP_HID, HEAD_DIM), CMP_HID ** -0.5),
        "nsa_gate_b": nrm(ks[14], (N_A, 3 * N_HEADS), 0.1),
        "moba_w_in": nrm(ks[15], (N_B, D, MOBA_IN), D ** -0.5),
        "moba_w_out": nrm(ks[16], (N_B, Q_W, D), Q_W ** -0.5),
        "moba_q_norm": 1.0 + nrm(ks[17], (N_B, HEAD_DIM), 0.02),
        "moba_k_norm": 1.0 + nrm(ks[18], (N_B, HEAD_DIM), 0.02),
    }


def reference(x, c, positions, norm_g, ada_w, ada_b, nsa_w_in, nsa_w_out, nsa_q_norm, nsa_k_norm,
              nsa_cmp_pe, nsa_cmp_w1, nsa_cmp_b1, nsa_cmp_w2, nsa_gate_b,
              moba_w_in, moba_w_out, moba_q_norm, moba_k_norm):
    cos, sin = rope_tables(positions)
    cond = jax.nn.silu(c)
    for i in range(DEPTH):
        shift, scale, gate = jnp.split(cond @ ada_w[i] + ada_b[i], 3, axis=-1)
        h = rms_norm(x, norm_g[i]) * (1 + scale[:, None, :]) + shift[:, None, :]
        j = i // N_MIXERS
        if i % N_MIXERS == 0:
            y = nsa_mixer(h, cos, sin, positions, nsa_w_in[j], nsa_w_out[j], nsa_q_norm[j], nsa_k_norm[j],
                          nsa_cmp_pe[j], nsa_cmp_w1[j], nsa_cmp_b1[j], nsa_cmp_w2[j], nsa_gate_b[j])
        else:
            y = moba_mixer(h, cos, sin, moba_w_in[j], moba_w_out[j], moba_q_norm[j], moba_k_norm[j])
        x = x + gate[:, None, :] * y
    return x
```

```python
import functools

import numpy as np
import jax
import jax.numpy as jnp
from jax import lax
from jax.experimental import pallas as pl
from jax.experimental.pallas import tpu as pltpu

D_MODEL = 1024
BATCH = 16
SEQ = 2048
DEPTH = 4
HEAD_DIM = 64
N_HEADS = 16
N_KV = 4
GQA = 4
ROT_HALF = 8
ROPE_THETA = 500000.0
NORM_EPS = 1e-6
CMP_LEN = 32
CMP_STRIDE = 16
CMP_HID = 256
SEL_BLOCK = 64
SEL_COUNT = 8
N_SEL = SEQ // SEL_BLOCK
N_CMP = (SEQ - CMP_LEN) // CMP_STRIDE + 1
N_CMP_PAD = 128
MOBA_TOPK = 3

TQ = 256
N_CHUNK = SEQ // TQ
LANES_Q = GQA * TQ
TS = 512
NEG = -1e30
GATE_ROWS = 16

F32 = jnp.float32
BF16 = jnp.bfloat16

VMEM_LIMIT = 52 * 1024 * 1024


def _cparams(sem):
    return pltpu.CompilerParams(dimension_semantics=sem, vmem_limit_bytes=VMEM_LIMIT)


def _dot(a, b):
    return jnp.dot(a, b, preferred_element_type=F32)


def _ada_kernel(c_ref, w_ref, b_ref, o_ref):
    cond = c_ref[...]
    cond = cond * jax.nn.sigmoid(cond)
    o_ref[0] = jnp.dot(cond, w_ref[0], precision=lax.Precision.HIGHEST,
                       preferred_element_type=F32) + b_ref[0]


def _ada_call(c, ada_w, ada_b):
    nt = 1024
    return pl.pallas_call(
        _ada_kernel,
        grid=(DEPTH, 3 * D_MODEL // nt),
        in_specs=[
            pl.BlockSpec((BATCH, D_MODEL), lambda i, n: (0, 0)),
            pl.BlockSpec((1, D_MODEL, nt), lambda i, n: (i, 0, n)),
            pl.BlockSpec((1, 1, nt), lambda i, n: (i, 0, n)),
        ],
        out_specs=pl.BlockSpec((1, BATCH, nt), lambda i, n: (i, 0, n)),
        out_shape=jax.ShapeDtypeStruct((DEPTH, BATCH, 3 * D_MODEL), F32),
        compiler_params=_cparams(("parallel", "parallel")),
        name="ada_mod",
    )(c, ada_w, ada_b.reshape(DEPTH, 1, 3 * D_MODEL))


def _norm_mod(x_ref, ng_ref, sc_ref, sh_ref):
    x = x_ref[0]
    ms = jnp.mean(x * x, axis=0, keepdims=True)
    y = x * lax.rsqrt(ms + NORM_EPS)
    h = (y * ng_ref[...]) * (1.0 + sc_ref[0]) + sh_ref[0]
    return h.astype(BF16)


def _head_norm_rope(y, g, cos, sin):
    ms = jnp.mean(y * y, axis=0, keepdims=True)
    yn = (y * lax.rsqrt(ms + NORM_EPS)) * g
    x1 = yn[0:ROT_HALF]
    x2 = yn[ROT_HALF:2 * ROT_HALF]
    return jnp.concatenate([x1 * cos - x2 * sin, x2 * cos + x1 * sin, yn[2 * ROT_HALF:]], axis=0)


def _store_k_chunks(o_ref, gi, yh):
    for it in range(TS // TQ):
        o_ref[0, gi, it] = yh[:, it * TQ:(it + 1) * TQ].T.astype(BF16)


def _store_v_chunks(o_ref, y):
    yb = y.astype(BF16)
    for gi in range(N_KV):
        for it in range(TS // TQ):
            o_ref[0, gi, it] = yb[gi * HEAD_DIM:(gi + 1) * HEAD_DIM, it * TQ:(it + 1) * TQ]


NSA_OFF_Q = 0
NSA_OFF_KS = 1024
NSA_OFF_KW = 1280
NSA_OFF_KC = 1536
NSA_OFF_VC = 1792
NSA_OFF_VS = 2048
NSA_OFF_VW = 2304
NSA_OFF_GT = 2560
NSA_OFF_Z = NSA_OFF_GT + N_KV * GATE_ROWS
NSA_ROWS = NSA_OFF_Z + D_MODEL


def _nsa_in_kernel(x_ref, ng_ref, sc_ref, sh_ref, w_ref, qg_ref, ksg_ref, kwg_ref, gb_ref,
                   cos_ref, sin_ref,
                   q_ref, ks_ref, kw_ref, kc_ref, vc_ref, vs_ref, vw_ref, gt_ref, sz_ref):
    hb = _norm_mod(x_ref, ng_ref, sc_ref, sh_ref)
    cos = cos_ref[0]
    sin = sin_ref[0]

    def proj(r0, n):
        return _dot(w_ref[r0:r0 + n, :], hb)

    for gi in range(N_KV):
        y = proj(NSA_OFF_Q + gi * 256, 256)
        for r in range(GQA):
            yh = _head_norm_rope(y[r * HEAD_DIM:(r + 1) * HEAD_DIM], qg_ref[...], cos, sin)
            row = gi * 256 + r * HEAD_DIM
            q_ref[0, row:row + HEAD_DIM, :] = (yh * (HEAD_DIM ** -0.5)).astype(BF16)
    for off, g_ref, o_ref in ((NSA_OFF_KS, ksg_ref, ks_ref), (NSA_OFF_KW, kwg_ref, kw_ref)):
        y = proj(off, 256)
        for gi in range(N_KV):
            yh = _head_norm_rope(y[gi * HEAD_DIM:(gi + 1) * HEAD_DIM], g_ref[...], cos, sin)
            _store_k_chunks(o_ref, gi, yh)
    kc_ref[0] = proj(NSA_OFF_KC, 256)
    vc_ref[0] = proj(NSA_OFF_VC, 256)
    _store_v_chunks(vs_ref, proj(NSA_OFF_VS, 256))
    _store_v_chunks(vw_ref, proj(NSA_OFF_VW, 256))
    gt_ref[0] = jax.nn.sigmoid(proj(NSA_OFF_GT, N_KV * GATE_ROWS) + gb_ref[...])
    for cblk in range(D_MODEL // 256):
        z = proj(NSA_OFF_Z + cblk * 256, 256)
        sz_ref[0, cblk * 256:(cblk + 1) * 256, :] = z * jax.nn.sigmoid(z)


def _col_spec(rows):
    return pl.BlockSpec((rows, 1), lambda b, s: (0, 0))


def _bcol_spec(rows):
    return pl.BlockSpec((1, rows, 1), lambda b, s: (b, 0, 0))


def _fm_spec(rows):
    return pl.BlockSpec((1, rows, TS), lambda b, s: (b, 0, s))


_K_CHUNK_SPEC = pl.BlockSpec((1, N_KV, TS // TQ, TQ, HEAD_DIM), lambda b, s: (b, 0, s, 0, 0))
_V_CHUNK_SPEC = pl.BlockSpec((1, N_KV, TS // TQ, HEAD_DIM, TQ), lambda b, s: (b, 0, s, 0, 0))
_K_CHUNK_SHAPE = jax.ShapeDtypeStruct((BATCH, N_KV, N_CHUNK, TQ, HEAD_DIM), BF16)
_V_CHUNK_SHAPE = jax.ShapeDtypeStruct((BATCH, N_KV, N_CHUNK, HEAD_DIM, TQ), BF16)


def _nsa_in_call(xT, ng, sc, sh, wT, qg, ksg, kwg, gb, cosT, sinT):
    fm = lambda rows, dt: jax.ShapeDtypeStruct((BATCH, rows, SEQ), dt)
    return pl.pallas_call(
        _nsa_in_kernel,
        grid=(BATCH, SEQ // TS),
        in_specs=[
            _fm_spec(D_MODEL), _col_spec(D_MODEL), _bcol_spec(D_MODEL), _bcol_spec(D_MODEL),
            pl.BlockSpec((NSA_ROWS, D_MODEL), lambda b, s: (0, 0)),
            _col_spec(HEAD_DIM), _col_spec(HEAD_DIM), _col_spec(HEAD_DIM),
            _col_spec(N_KV * GATE_ROWS),
            _fm_spec(ROT_HALF), _fm_spec(ROT_HALF),
        ],
        out_specs=[
            _fm_spec(D_MODEL), _K_CHUNK_SPEC, _K_CHUNK_SPEC, _fm_spec(256), _fm_spec(256),
            _V_CHUNK_SPEC, _V_CHUNK_SPEC, _fm_spec(N_KV * GATE_ROWS), _fm_spec(D_MODEL),
        ],
        out_shape=[
            fm(D_MODEL, BF16), _K_CHUNK_SHAPE, _K_CHUNK_SHAPE, fm(256, F32), fm(256, F32),
            _V_CHUNK_SHAPE, _V_CHUNK_SHAPE, fm(N_KV * GATE_ROWS, F32), fm(D_MODEL, F32),
        ],
        compiler_params=_cparams(("parallel", "parallel")),
        name="nsa_in_proj",
    )(xT, ng, sc, sh, wT, qg, ksg, kwg, gb, cosT, sinT)


MOBA_OFF_Q = 0
MOBA_OFF_K = 1024
MOBA_OFF_V = 1280
MOBA_OFF_Z = 1536
MOBA_ROWS = 2560
KM_LANES = 128


def _moba_in_kernel(x_ref, ng_ref, sc_ref, sh_ref, w_ref, qg_ref, kg_ref, cos_ref, sin_ref,
                    q_ref, k_ref, km_ref, v_ref, sz_ref):
    hb = _norm_mod(x_ref, ng_ref, sc_ref, sh_ref)
    cos = cos_ref[0]
    sin = sin_ref[0]

    def proj(r0, n):
        return _dot(w_ref[r0:r0 + n, :], hb)

    for gi in range(N_KV):
        y = proj(MOBA_OFF_Q + gi * 256, 256)
        for r in range(GQA):
            yh = _head_norm_rope(y[r * HEAD_DIM:(r + 1) * HEAD_DIM], qg_ref[...], cos, sin)
            row = gi * 256 + r * HEAD_DIM
            q_ref[0, row:row + HEAD_DIM, :] = (yh * (HEAD_DIM ** -0.5)).astype(BF16)
    y = proj(MOBA_OFF_K, 256)
    lane = lax.broadcasted_iota(jnp.int32, (HEAD_DIM, KM_LANES), 1)
    for gi in range(N_KV):
        yh = _head_norm_rope(y[gi * HEAD_DIM:(gi + 1) * HEAD_DIM], kg_ref[...], cos, sin)
        _store_k_chunks(k_ref, gi, yh)
        km = jnp.zeros((HEAD_DIM, KM_LANES), F32)
        for it in range(TS // TQ):
            mean = jnp.mean(yh[:, it * TQ:(it + 1) * TQ], axis=1, keepdims=True)
            km = jnp.where(lane == it, mean, km)
        km_ref[0, 0, gi * HEAD_DIM:(gi + 1) * HEAD_DIM, :] = km
    _store_v_chunks(v_ref, proj(MOBA_OFF_V, 256))
    for cblk in range(D_MODEL // 256):
        z = proj(MOBA_OFF_Z + cblk * 256, 256)
        sz_ref[0, cblk * 256:(cblk + 1) * 256, :] = z * jax.nn.sigmoid(z)


def _moba_in_call(xT, ng, sc, sh, wT, qg, kg, cosT, sinT):
    fm = lambda rows, dt: jax.ShapeDtypeStruct((BATCH, rows, SEQ), dt)
    return pl.pallas_call(
        _moba_in_kernel,
        grid=(BATCH, SEQ // TS),
        in_specs=[
            _fm_spec(D_MODEL), _col_spec(D_MODEL), _bcol_spec(D_MODEL), _bcol_spec(D_MODEL),
            pl.BlockSpec((MOBA_ROWS, D_MODEL), lambda b, s: (0, 0)),
            _col_spec(HEAD_DIM), _col_spec(HEAD_DIM),
            _fm_spec(ROT_HALF), _fm_spec(ROT_HALF),
        ],
        out_specs=[
            _fm_spec(D_MODEL), _K_CHUNK_SPEC,
            pl.BlockSpec((1, 1, N_KV * HEAD_DIM, KM_LANES), lambda b, s: (b, s, 0, 0)),
            _V_CHUNK_SPEC, _fm_spec(D_MODEL),
        ],
        out_shape=[
            fm(D_MODEL, BF16), _K_CHUNK_SHAPE,
            jax.ShapeDtypeStruct((BATCH, SEQ // TS, N_KV * HEAD_DIM, KM_LANES), F32),
            _V_CHUNK_SHAPE, fm(D_MODEL, F32),
        ],
        compiler_params=_cparams(("parallel", "parallel")),
        name="moba_in_proj",
    )(xT, ng, sc, sh, wT, qg, kg, cosT, sinT)


CMP_TC = 1024
CMP_COLS = BATCH * N_KV * N_CMP_PAD
CMP_FEAT = CMP_STRIDE * HEAD_DIM


def _compress_kernel(x_ref, pea_ref, peb_ref, w1a_ref, w1b_ref, b1_ref, w2_ref, kg_ref,
                     cos_ref, sin_ref, o_ref):
    x = x_ref[0]
    u = _dot(w1a_ref[0], (x + pea_ref[0]).astype(BF16))
    v = _dot(w1b_ref[0], (x + peb_ref[0]).astype(BF16))
    v = pltpu.roll(v, CMP_TC - 1, 1)
    h = u + v + b1_ref[0]
    h = 0.5 * h * (1.0 + jnp.tanh(np.sqrt(2.0 / np.pi) * (h + 0.044715 * (h * h * h))))
    out = _dot(w2_ref[0], h.astype(BF16))
    is_key = pl.program_id(0) == 0
    keyed = _head_norm_rope(out, kg_ref[...], cos_ref[...], sin_ref[...])
    o_ref[0] = jnp.where(is_key, keyed, out)


def _compress_call(xT, pea, peb, w1a, w1b, b1, w2, kg, cosc, sinc):
    per_i = lambda shape: pl.BlockSpec((1,) + shape, lambda i, t: (i, 0, 0))
    return pl.pallas_call(
        _compress_kernel,
        grid=(2, CMP_COLS // CMP_TC),
        in_specs=[
            pl.BlockSpec((1, CMP_FEAT, CMP_TC), lambda i, t: (i, 0, t)),
            per_i((CMP_FEAT, 1)), per_i((CMP_FEAT, 1)),
            per_i((CMP_HID, CMP_FEAT)), per_i((CMP_HID, CMP_FEAT)),
            per_i((CMP_HID, 1)), per_i((HEAD_DIM, CMP_HID)),
            pl.BlockSpec((HEAD_DIM, 1), lambda i, t: (0, 0)),
            pl.BlockSpec((ROT_HALF, CMP_TC), lambda i, t: (0, t)),
            pl.BlockSpec((ROT_HALF, CMP_TC), lambda i, t: (0, t)),
        ],
        out_specs=pl.BlockSpec((1, HEAD_DIM, CMP_TC), lambda i, t: (i, 0, t)),
        out_shape=jax.ShapeDtypeStruct((2, HEAD_DIM, CMP_COLS), F32),
        compiler_params=_cparams(("parallel", "parallel")),
        name="nsa_compress",
    )(xT, pea, peb, w1a, w1b, b1, w2, kg, cosc, sinc)


def _group_queries(q_ref):
    return jnp.concatenate(
        [q_ref[0, r * HEAD_DIM:(r + 1) * HEAD_DIM, :] for r in range(GQA)], axis=1)


def _init_state(s, vT, m_ref, l_ref, acc_ref):
    m = jnp.max(s, axis=0, keepdims=True)
    p = jnp.exp(s - m)
    m_ref[...] = m
    l_ref[...] = jnp.sum(p, axis=0, keepdims=True)
    acc_ref[...] = _dot(vT, p.astype(BF16))


def _update_state(s, vT, m_ref, l_ref, acc_ref):
    m_old = m_ref[...]
    m_new = jnp.maximum(m_old, jnp.max(s, axis=0, keepdims=True))
    alpha = jnp.exp(m_old - m_new)
    p = jnp.exp(s - m_new)
    l_ref[...] = alpha * l_ref[...] + jnp.sum(p, axis=0, keepdims=True)
    acc_ref[...] = alpha * acc_ref[...] + _dot(vT, p.astype(BF16))
    m_ref[...] = m_new


def _rank_select(score_ref, n_rows, j_idx, count):
    score = score_ref[...]
    cnt = jnp.zeros(score.shape, jnp.int32)
    for jp in range(n_rows):
        row = score_ref[jp:jp + 1, :]
        beats = (row > score) | ((row == score) & (jp < j_idx))
        cnt = cnt + beats.astype(jnp.int32)
    return cnt < count


def _store_out(o_ref, sz_ref, o):
    for r in range(GQA):
        rows = slice(r * HEAD_DIM, (r + 1) * HEAD_DIM)
        o_ref[0, rows, :] = (o[:, r * TQ:(r + 1) * TQ] * sz_ref[0, rows, :]).astype(BF16)


def _causal_lower():
    a_idx = lax.broadcasted_iota(jnp.int32, (TQ, LANES_Q), 0)
    c_loc = lax.broadcasted_iota(jnp.int32, (TQ, LANES_Q), 1) & (TQ - 1)
    return a_idx <= c_loc


def _nsa_attn_kernel(q_ref, kc_ref, vc_ref, ks_ref, vs_ref, kw_ref, vw_ref, gt_ref, sz_ref,
                     selw_ref, o_ref, score_ref, bias_ref, m_ref, l_ref, acc_ref):
    qi = pl.program_id(2)
    q4 = _group_queries(q_ref)
    state = (m_ref, l_ref, acc_ref)

    t = qi * TQ + (lax.broadcasted_iota(jnp.int32, (1, LANES_Q), 1) & (TQ - 1))
    n_idx = lax.broadcasted_iota(jnp.int32, (N_CMP_PAD, LANES_Q), 0)
    s = jnp.where(n_idx * CMP_STRIDE + (CMP_LEN - 1) <= t, _dot(kc_ref[0, 0], q4), NEG)
    m = jnp.max(s, axis=0, keepdims=True)
    p = jnp.exp(s - m) * (m > 0.5 * NEG).astype(F32)
    p = p / jnp.maximum(jnp.sum(p, axis=0, keepdims=True), 1e-30)
    o_cmp = _dot(vc_ref[0, 0], p.astype(BF16))

    psum = p[:, 0:TQ]
    for r in range(1, GQA):
        psum = psum + p[:, r * TQ:(r + 1) * TQ]
    p_hi = psum.astype(BF16)
    p_lo = (psum - p_hi.astype(F32)).astype(BF16)
    imp = _dot(selw_ref[...], p_hi) + _dot(selw_ref[...], p_lo)
    j_idx = lax.broadcasted_iota(jnp.int32, (N_SEL, TQ), 0)
    cur = qi * (TQ // SEL_BLOCK) + (lax.broadcasted_iota(jnp.int32, (N_SEL, TQ), 1) >> 6)
    valid = j_idx <= cur
    forced = (j_idx == 0) | (j_idx == cur) | (j_idx == cur - 1)
    score_ref[...] = jnp.where(forced, jnp.inf, jnp.where(valid, imp, -jnp.inf))
    sel = valid & _rank_select(score_ref, N_SEL, j_idx, SEL_COUNT)
    bias = jnp.where(sel, 0.0, NEG)
    bias_ref[...] = jnp.concatenate([bias] * GQA, axis=1)

    lower = _causal_lower()

    def sel_bias(j):
        per_chunk = TQ // SEL_BLOCK
        return jnp.concatenate(
            [jnp.broadcast_to(bias_ref[pl.ds(per_chunk * j + u, 1), :], (SEL_BLOCK, LANES_Q))
             for u in range(per_chunk)], axis=0)

    s = jnp.where(lower, _dot(ks_ref[0, 0, qi], q4) + sel_bias(qi), NEG)
    _init_state(s, vs_ref[0, 0, qi], *state)

    def sel_body(j, carry):
        _update_state(_dot(ks_ref[0, 0, j], q4) + sel_bias(j), vs_ref[0, 0, j], *state)
        return carry

    lax.fori_loop(0, qi, sel_body, 0)
    o_slc = acc_ref[...] / l_ref[...]

    s = jnp.where(lower, _dot(kw_ref[0, 0, qi], q4), NEG)
    _init_state(s, vw_ref[0, 0, qi], *state)

    @pl.when(qi > 0)
    def _():
        sp = jnp.where(lower, NEG, _dot(kw_ref[0, 0, qi - 1], q4))
        _update_state(sp, vw_ref[0, 0, qi - 1], *state)

    o_win = acc_ref[...] / l_ref[...]

    def gate(br):
        return jnp.concatenate(
            [gt_ref[0, br * GQA + r:br * GQA + r + 1, :] for r in range(GQA)], axis=1)

    o = gate(0) * o_cmp + gate(1) * o_slc + gate(2) * o_win
    _store_out(o_ref, sz_ref, o)


def _attn_specs():
    q_spec = pl.BlockSpec((1, GQA * HEAD_DIM, TQ), lambda b, g, i: (b, g, i))
    k_spec = pl.BlockSpec((1, 1, N_CHUNK, TQ, HEAD_DIM), lambda b, g, i: (b, g, 0, 0, 0))
    v_spec = pl.BlockSpec((1, 1, N_CHUNK, HEAD_DIM, TQ), lambda b, g, i: (b, g, 0, 0, 0))
    return q_spec, k_spec, v_spec


def _nsa_attn_call(q, kc, vc, ks, vs, kw, vw, gt, sz, selw):
    q_spec, k_spec, v_spec = _attn_specs()
    return pl.pallas_call(
        _nsa_attn_kernel,
        grid=(BATCH, N_KV, N_CHUNK),
        in_specs=[
            q_spec,
            pl.BlockSpec((1, 1, N_CMP_PAD, HEAD_DIM), lambda b, g, i: (b, g, 0, 0)),
            pl.BlockSpec((1, 1, HEAD_DIM, N_CMP_PAD), lambda b, g, i: (b, g, 0, 0)),
            k_spec, v_spec, k_spec, v_spec,
            pl.BlockSpec((1, GATE_ROWS, TQ), lambda b, g, i: (b, g, i)),
            q_spec,
            pl.BlockSpec((N_SEL, N_CMP_PAD), lambda b, g, i: (0, 0)),
        ],
        out_specs=q_spec,
        out_shape=jax.ShapeDtypeStruct((BATCH, D_MODEL, SEQ), BF16),
        scratch_shapes=[
            pltpu.VMEM((N_SEL, TQ), F32),
            pltpu.VMEM((N_SEL, LANES_Q), F32),
            pltpu.VMEM((1, LANES_Q), F32),
            pltpu.VMEM((1, LANES_Q), F32),
            pltpu.VMEM((HEAD_DIM, LANES_Q), F32),
        ],
        compiler_params=_cparams(("parallel", "parallel", "arbitrary")),
        name="nsa_attention",
    )(q, kc, vc, ks, vs, kw, vw, gt, sz, selw)


def _moba_attn_kernel(q_ref, km_ref, k_ref, v_ref, sz_ref, o_ref,
                      score_ref, bias_ref, m_ref, l_ref, acc_ref):
    qi = pl.program_id(2)
    q4 = _group_queries(q_ref)
    state = (m_ref, l_ref, acc_ref)

    j_idx = lax.broadcasted_iota(jnp.int32, (N_CHUNK, LANES_Q), 0)
    past = j_idx < qi
    score_ref[...] = jnp.where(past, _dot(km_ref[0, 0], q4), -jnp.inf)
    sel = past & _rank_select(score_ref, N_CHUNK, j_idx, MOBA_TOPK)
    bias_ref[...] = jnp.where(sel, 0.0, NEG)

    s = jnp.where(_causal_lower(), _dot(k_ref[0, 0, qi], q4), NEG)
    _init_state(s, v_ref[0, 0, qi], *state)

    def body(j, carry):
        _update_state(_dot(k_ref[0, 0, j], q4) + bias_ref[pl.ds(j, 1), :], v_ref[0, 0, j], *state)
        return carry

    lax.fori_loop(0, qi, body, 0)
    _store_out(o_ref, sz_ref, acc_ref[...] / l_ref[...])


def _moba_attn_call(q, km, k, v, sz):
    q_spec, k_spec, v_spec = _attn_specs()
    return pl.pallas_call(
        _moba_attn_kernel,
        grid=(BATCH, N_KV, N_CHUNK),
        in_specs=[
            q_spec,
            pl.BlockSpec((1, 1, N_CHUNK, HEAD_DIM), lambda b, g, i: (b, g, 0, 0)),
            k_spec, v_spec, q_spec,
        ],
        out_specs=q_spec,
        out_shape=jax.ShapeDtypeStruct((BATCH, D_MODEL, SEQ), BF16),
        scratch_shapes=[
            pltpu.VMEM((N_CHUNK, LANES_Q), F32),
            pltpu.VMEM((N_CHUNK, LANES_Q), F32),
            pltpu.VMEM((1, LANES_Q), F32),
            pltpu.VMEM((1, LANES_Q), F32),
            pltpu.VMEM((HEAD_DIM, LANES_Q), F32),
        ],
        compiler_params=_cparams(("parallel", "parallel", "arbitrary")),
        name="moba_attention",
    )(q, km, k, v, sz)


def _out_kernel(oz_ref, w_ref, x_ref, gt_ref, o_ref):
    oz = oz_ref[0]
    for cblk in range(D_MODEL // 256):
        rows = slice(cblk * 256, (cblk + 1) * 256)
        y = _dot(w_ref[rows, :], oz)
        o_ref[0, rows, :] = x_ref[0, rows, :] + gt_ref[0, rows, :] * y


def _out_call(oz, wT, xT, gate):
    return pl.pallas_call(
        _out_kernel,
        grid=(BATCH, SEQ // TS),
        in_specs=[
            _fm_spec(D_MODEL),
            pl.BlockSpec((D_MODEL, D_MODEL), lambda b, s: (0, 0)),
            _fm_spec(D_MODEL), _bcol_spec(D_MODEL),
        ],
        out_specs=_fm_spec(D_MODEL),
        out_shape=jax.ShapeDtypeStruct((BATCH, D_MODEL, SEQ), F32),
        compiler_params=_cparams(("parallel", "parallel")),
        name="out_proj",
    )(oz, wT, xT, gate)


def _rope_tables(pos):
    inv_freq = ROPE_THETA ** (-jnp.arange(0, 2 * ROT_HALF, 2, dtype=F32) / (2 * ROT_HALF))
    ang = pos.astype(F32)[..., None] * inv_freq
    return jnp.cos(ang), jnp.sin(ang)


def _gate_perm():
    perm = np.full((N_KV * GATE_ROWS,), 3 * N_HEADS, dtype=np.int32)
    for g in range(N_KV):
        for br in range(3):
            for r in range(GQA):
                perm[g * GATE_ROWS + br * GQA + r] = (g * GQA + r) * 3 + br
    return perm


def _sel_weights_T():
    cs = np.arange(N_CMP)[:, None] * CMP_STRIDE
    ss = np.arange(N_SEL)[None, :] * SEL_BLOCK
    shared = np.clip(np.minimum(cs + CMP_LEN, ss + SEL_BLOCK) - np.maximum(cs, ss), 0, None)
    w = np.zeros((N_CMP_PAD, N_SEL), np.float32)
    w[:N_CMP] = shared / CMP_LEN
    return jnp.asarray(w.T, dtype=BF16)


def _col(v):
    return v.reshape(-1, 1)


def _nsa_layer(xT, ng, sc, sh, gate, cosT, sinT, cosc, sinc, w_in, w_out, q_g, k_g,
               cmp_pe, cmp_w1, cmp_b1, cmp_w2, gate_b):
    widths = [1024] + [256] * 6 + [3 * N_HEADS, 1024]
    q, kc, vc, ks, vs, kw, vw, gl, z = jnp.split(w_in, np.cumsum(widths)[:-1].tolist(), axis=1)
    perm = _gate_perm()
    gl_p = jnp.concatenate([gl, jnp.zeros((D_MODEL, 1), F32)], axis=1)[:, perm]
    gb_p = jnp.concatenate([gate_b, jnp.zeros((1,), F32)])[perm]
    wT = jnp.concatenate([q, ks, kw, kc, vc, vs, vw, gl_p, z], axis=1).T.astype(BF16)

    qT, ks_c, kw_c, kcT, vcT, vs_c, vw_c, gates, sz = _nsa_in_call(
        xT, ng, sc, sh, wT, _col(q_g), _col(k_g[1]), _col(k_g[2]), _col(gb_p), cosT, sinT)

    def half_blocks(t):
        t = t.reshape(BATCH, N_KV, HEAD_DIM, N_CMP_PAD, CMP_STRIDE)
        return t.transpose(4, 2, 0, 1, 3).reshape(CMP_FEAT, CMP_COLS)

    x_cmp = jnp.stack([half_blocks(kcT), half_blocks(vcT)])
    pe = cmp_pe.reshape(2, 2, CMP_FEAT, 1)
    w1 = cmp_w1.reshape(2, 2, CMP_FEAT, CMP_HID).transpose(0, 1, 3, 2).astype(BF16)
    cmp_out = _compress_call(
        x_cmp, pe[:, 0], pe[:, 1], w1[:, 0], w1[:, 1], cmp_b1.reshape(2, CMP_HID, 1),
        cmp_w2.transpose(0, 2, 1).astype(BF16), _col(k_g[0]), cosc, sinc)
    cmp_out = cmp_out.reshape(2, HEAD_DIM, BATCH, N_KV, N_CMP_PAD)
    k_cmp = cmp_out[0].transpose(1, 2, 3, 0).astype(BF16)
    v_cmpT = cmp_out[1].transpose(1, 2, 0, 3).astype(BF16)

    oz = _nsa_attn_call(qT, k_cmp, v_cmpT, ks_c, vs_c, kw_c, vw_c, gates, sz, _sel_weights_T())
    return _out_call(oz, w_out.T.astype(BF16), xT, gate)


def _moba_layer(xT, ng, sc, sh, gate, cosT, sinT, w_in, w_out, q_g, k_g):
    wT = w_in.T.astype(BF16)
    qT, k_c, km, v_c, sz = _moba_in_call(xT, ng, sc, sh, wT, _col(q_g), _col(k_g), cosT, sinT)
    nb = TS // TQ
    km = km[..., :nb].reshape(BATCH, SEQ // TS, N_KV, HEAD_DIM, nb)
    km = km.transpose(0, 2, 1, 4, 3).reshape(BATCH, N_KV, N_CHUNK, HEAD_DIM).astype(BF16)
    oz = _moba_attn_call(qT, km, k_c, v_c, sz)
    return _out_call(oz, w_out.T.astype(BF16), xT, gate)


@jax.jit
def _forward(x, c, positions, norm_g, ada_w, ada_b, nsa_w_in, nsa_w_out, nsa_q_norm, nsa_k_norm,
             nsa_cmp_pe, nsa_cmp_w1, nsa_cmp_b1, nsa_cmp_w2, nsa_gate_b,
             moba_w_in, moba_w_out, moba_q_norm, moba_k_norm):
    cos, sin = _rope_tables(positions)
    cosT = cos.transpose(0, 2, 1)
    sinT = sin.transpose(0, 2, 1)
    cmp_end = np.minimum(np.arange(N_CMP_PAD) * CMP_STRIDE + CMP_LEN - 1, SEQ - 1)
    cos_c, sin_c = _rope_tables(positions[:, cmp_end])

    def cmp_table(t):
        t = jnp.broadcast_to(t.transpose(2, 0, 1)[:, :, None, :],
                             (ROT_HALF, BATCH, N_KV, N_CMP_PAD))
        return t.reshape(ROT_HALF, CMP_COLS)

    cosc = cmp_table(cos_c)
    sinc = cmp_table(sin_c)

    mod = _ada_call(c, ada_w, ada_b)
    shift, scale, gate = jnp.split(mod[..., None], 3, axis=2)

    xT = x.transpose(0, 2, 1)
    for i in range(DEPTH):
        j = i // 2
        ng = _col(norm_g[i])
        if i % 2 == 0:
            xT = _nsa_layer(xT, ng, scale[i], shift[i], gate[i], cosT, sinT, cosc, sinc,
                            nsa_w_in[j], nsa_w_out[j], nsa_q_norm[j], nsa_k_norm[j],
                            nsa_cmp_pe[j], nsa_cmp_w1[j], nsa_cmp_b1[j], nsa_cmp_w2[j],
                            nsa_gate_b[j])
        else:
            xT = _moba_layer(xT, ng, scale[i], shift[i], gate[i], cosT, sinT,
                             moba_w_in[j], moba_w_out[j], moba_q_norm[j], moba_k_norm[j])
    return xT.transpose(0, 2, 1)


def kernel(x, c, positions, norm_g, ada_w, ada_b, nsa_w_in, nsa_w_out, nsa_q_norm, nsa_k_norm, nsa_cmp_pe, nsa_cmp_w1, nsa_cmp_b1, nsa_cmp_w2, nsa_gate_b, moba_w_in, moba_w_out, moba_q_norm, moba_k_norm):
    return _forward(x, c, positions, norm_g, ada_w, ada_b, nsa_w_in, nsa_w_out, nsa_q_norm,
                    nsa_k_norm, nsa_cmp_pe, nsa_cmp_w1, nsa_cmp_b1, nsa_cmp_w2, nsa_gate_b,
                    moba_w_in, moba_w_out, moba_q_norm, moba_k_norm)
```

```python
import functools

import numpy as np
import jax
import jax.numpy as jnp
from jax import lax
from jax.experimental import pallas as pl
from jax.experimental.pallas import tpu as pltpu

D_MODEL = 1024
BATCH = 16
SEQ = 2048
DEPTH = 4
HEAD_DIM = 64
N_HEADS = 16
N_KV = 4
GQA = 4
ROT_HALF = 8
ROPE_THETA = 500000.0
NORM_EPS = 1e-6
CMP_LEN = 32
CMP_STRIDE = 16
CMP_HID = 256
SEL_BLOCK = 64
SEL_COUNT = 8
N_SEL = SEQ // SEL_BLOCK
N_CMP = (SEQ - CMP_LEN) // CMP_STRIDE + 1
N_CMP_PAD = 128
MOBA_TOPK = 3

TQ = 256
N_CHUNK = SEQ // TQ
LANES_Q = GQA * TQ
TS = 512
NEG = -1e30
GATE_ROWS = 16

F32 = jnp.float32
BF16 = jnp.bfloat16

VMEM_LIMIT = 52 * 1024 * 1024


def _cparams(sem):
    return pltpu.CompilerParams(dimension_semantics=sem, vmem_limit_bytes=VMEM_LIMIT)


def _dot(a, b):
    return jnp.dot(a, b, preferred_element_type=F32)


def _ada_kernel(c_ref, w_ref, b_ref, o_ref):
    cond = c_ref[...]
    cond = cond * jax.nn.sigmoid(cond)
    o_ref[0] = jnp.dot(cond, w_ref[0], precision=lax.Precision.HIGHEST,
                       preferred_element_type=F32) + b_ref[0]


def _ada_call(c, ada_w, ada_b):
    nt = 1024
    return pl.pallas_call(
        _ada_kernel,
        grid=(DEPTH, 3 * D_MODEL // nt),
        in_specs=[
            pl.BlockSpec((BATCH, D_MODEL), lambda i, n: (0, 0)),
            pl.BlockSpec((1, D_MODEL, nt), lambda i, n: (i, 0, n)),
            pl.BlockSpec((1, 1, nt), lambda i, n: (i, 0, n)),
        ],
        out_specs=pl.BlockSpec((1, BATCH, nt), lambda i, n: (i, 0, n)),
        out_shape=jax.ShapeDtypeStruct((DEPTH, BATCH, 3 * D_MODEL), F32),
        compiler_params=_cparams(("parallel", "parallel")),
        name="ada_mod",
    )(c, ada_w, ada_b.reshape(DEPTH, 1, 3 * D_MODEL))


def _norm_mod(x_ref, ng_ref, sc_ref, sh_ref):
    x = x_ref[0]
    ms = jnp.mean(x * x, axis=0, keepdims=True)
    y = x * lax.rsqrt(ms + NORM_EPS)
    h = (y * ng_ref[...]) * (1.0 + sc_ref[0]) + sh_ref[0]
    return h.astype(BF16)


def _head_norm_rope(y, g, cos, sin):
    ms = jnp.mean(y * y, axis=0, keepdims=True)
    yn = (y * lax.rsqrt(ms + NORM_EPS)) * g
    x1 = yn[0:ROT_HALF]
    x2 = yn[ROT_HALF:2 * ROT_HALF]
    return jnp.concatenate([x1 * cos - x2 * sin, x2 * cos + x1 * sin, yn[2 * ROT_HALF:]], axis=0)


def _store_k_chunks(o_ref, gi, yh):
    for it in range(TS // TQ):
        o_ref[0, gi, it] = yh[:, it * TQ:(it + 1) * TQ].T.astype(BF16)


def _store_v_chunks(o_ref, y):
    yb = y.astype(BF16)
    for gi in range(N_KV):
        for it in range(TS // TQ):
            o_ref[0, gi, it] = yb[gi * HEAD_DIM:(gi + 1) * HEAD_DIM, it * TQ:(it + 1) * TQ]


NSA_OFF_Q = 0
NSA_OFF_KS = 1024
NSA_OFF_KW = 1280
NSA_OFF_KC = 1536
NSA_OFF_VC = 1792
NSA_OFF_VS = 2048
NSA_OFF_VW = 2304
NSA_OFF_GT = 2560
NSA_OFF_Z = NSA_OFF_GT + N_KV * GATE_ROWS
NSA_ROWS = NSA_OFF_Z + D_MODEL


def _nsa_in_kernel(x_ref, ng_ref, sc_ref, sh_ref, w_ref, qg_ref, ksg_ref, kwg_ref, gb_ref,
                   cos_ref, sin_ref,
                   q_ref, ks_ref, kw_ref, kc_ref, vc_ref, vs_ref, vw_ref, gt_ref, sz_ref):
    hb = _norm_mod(x_ref, ng_ref, sc_ref, sh_ref)
    cos = cos_ref[0]
    sin = sin_ref[0]

    def proj(r0, n):
        return _dot(w_ref[r0:r0 + n, :], hb)

    for gi in range(N_KV):
        y = proj(NSA_OFF_Q + gi * 256, 256)
        for r in range(GQA):
            yh = _head_norm_rope(y[r * HEAD_DIM:(r + 1) * HEAD_DIM], qg_ref[...], cos, sin)
            row = gi * 256 + r * HEAD_DIM
            q_ref[0, row:row + HEAD_DIM, :] = (yh * (HEAD_DIM ** -0.5)).astype(BF16)
    for off, g_ref, o_ref in ((NSA_OFF_KS, ksg_ref, ks_ref), (NSA_OFF_KW, kwg_ref, kw_ref)):
        y = proj(off, 256)
        for gi in range(N_KV):
            yh = _head_norm_rope(y[gi * HEAD_DIM:(gi + 1) * HEAD_DIM], g_ref[...], cos, sin)
            _store_k_chunks(o_ref, gi, yh)
    kc_ref[0] = proj(NSA_OFF_KC, 256)
    vc_ref[0] = proj(NSA_OFF_VC, 256)
    _store_v_chunks(vs_ref, proj(NSA_OFF_VS, 256))
    _store_v_chunks(vw_ref, proj(NSA_OFF_VW, 256))
    gt_ref[0] = jax.nn.sigmoid(proj(NSA_OFF_GT, N_KV * GATE_ROWS) + gb_ref[...])
    for cblk in range(D_MODEL // 256):
        z = proj(NSA_OFF_Z + cblk * 256, 256)
        sz_ref[0, cblk * 256:(cblk + 1) * 256, :] = z * jax.nn.sigmoid(z)


def _col_spec(rows):
    return pl.BlockSpec((rows, 1), lambda b, s: (0, 0))


def _bcol_spec(rows):
    return pl.BlockSpec((1, rows, 1), lambda b, s: (b, 0, 0))


def _fm_spec(rows):
    return pl.BlockSpec((1, rows, TS), lambda b, s: (b, 0, s))


_K_CHUNK_SPEC = pl.BlockSpec((1, N_KV, TS // TQ, TQ, HEAD_DIM), lambda b, s: (b, 0, s, 0, 0))
_V_CHUNK_SPEC = pl.BlockSpec((1, N_KV, TS // TQ, HEAD_DIM, TQ), lambda b, s: (b, 0, s, 0, 0))
_K_CHUNK_SHAPE = jax.ShapeDtypeStruct((BATCH, N_KV, N_CHUNK, TQ, HEAD_DIM), BF16)
_V_CHUNK_SHAPE = jax.ShapeDtypeStruct((BATCH, N_KV, N_CHUNK, HEAD_DIM, TQ), BF16)


def _nsa_in_call(xT, ng, sc, sh, wT, qg, ksg, kwg, gb, cosT, sinT):
    fm = lambda rows, dt: jax.ShapeDtypeStruct((BATCH, rows, SEQ), dt)
    return pl.pallas_call(
        _nsa_in_kernel,
        grid=(BATCH, SEQ // TS),
        in_specs=[
            _fm_spec(D_MODEL), _col_spec(D_MODEL), _bcol_spec(D_MODEL), _bcol_spec(D_MODEL),
            pl.BlockSpec((NSA_ROWS, D_MODEL), lambda b, s: (0, 0)),
            _col_spec(HEAD_DIM), _col_spec(HEAD_DIM), _col_spec(HEAD_DIM),
            _col_spec(N_KV * GATE_ROWS),
            _fm_spec(ROT_HALF), _fm_spec(ROT_HALF),
        ],
        out_specs=[
            _fm_spec(D_MODEL), _K_CHUNK_SPEC, _K_CHUNK_SPEC, _fm_spec(256), _fm_spec(256),
            _V_CHUNK_SPEC, _V_CHUNK_SPEC, _fm_spec(N_KV * GATE_ROWS), _fm_spec(D_MODEL),
        ],
        out_shape=[
            fm(D_MODEL, BF16), _K_CHUNK_SHAPE, _K_CHUNK_SHAPE, fm(256, F32), fm(256, F32),
            _V_CHUNK_SHAPE, _V_CHUNK_SHAPE, fm(N_KV * GATE_ROWS, F32), fm(D_MODEL, F32),
        ],
        compiler_params=_cparams(("parallel", "parallel")),
        name="nsa_in_proj",
    )(xT, ng, sc, sh, wT, qg, ksg, kwg, gb, cosT, sinT)


MOBA_OFF_Q = 0
MOBA_OFF_K = 1024
MOBA_OFF_V = 1280
MOBA_OFF_Z = 1536
MOBA_ROWS = 2560
KM_LANES = 128


def _moba_in_kernel(x_ref, ng_ref, sc_ref, sh_ref, w_ref, qg_ref, kg_ref, cos_ref, sin_ref,
                    q_ref, k_ref, km_ref, v_ref, sz_ref):
    hb = _norm_mod(x_ref, ng_ref, sc_ref, sh_ref)
    cos = cos_ref[0]
    sin = sin_ref[0]

    def proj(r0, n):
        return _dot(w_ref[r0:r0 + n, :], hb)

    for gi in range(N_KV):
        y = proj(MOBA_OFF_Q + gi * 256, 256)
        for r in range(GQA):
            yh = _head_norm_rope(y[r * HEAD_DIM:(r + 1) * HEAD_DIM], qg_ref[...], cos, sin)
            row = gi * 256 + r * HEAD_DIM
            q_ref[0, row:row + HEAD_DIM, :] = (yh * (HEAD_DIM ** -0.5)).astype(BF16)
    y = proj(MOBA_OFF_K, 256)
    lane = lax.broadcasted_iota(jnp.int32, (HEAD_DIM, KM_LANES), 1)
    for gi in range(N_KV):
        yh = _head_norm_rope(y[gi * HEAD_DIM:(gi + 1) * HEAD_DIM], kg_ref[...], cos, sin)
        _store_k_chunks(k_ref, gi, yh)
        km = jnp.zeros((HEAD_DIM, KM_LANES), F32)
        for it in range(TS // TQ):
            mean = jnp.mean(yh[:, it * TQ:(it + 1) * TQ], axis=1, keepdims=True)
            km = jnp.where(lane == it, mean, km)
        km_ref[0, 0, gi * HEAD_DIM:(gi + 1) * HEAD_DIM, :] = km
    _store_v_chunks(v_ref, proj(MOBA_OFF_V, 256))
    for cblk in range(D_MODEL // 256):
        z = proj(MOBA_OFF_Z + cblk * 256, 256)
        sz_ref[0, cblk * 256:(cblk + 1) * 256, :] = z * jax.nn.sigmoid(z)


def _moba_in_call(xT, ng, sc, sh, wT, qg, kg, cosT, sinT):
    fm = lambda rows, dt: jax.ShapeDtypeStruct((BATCH, rows, SEQ), dt)
    return pl.pallas_call(
        _moba_in_kernel,
        grid=(BATCH, SEQ // TS),
        in_specs=[
            _fm_spec(D_MODEL), _col_spec(D_MODEL), _bcol_spec(D_MODEL), _bcol_spec(D_MODEL),
            pl.BlockSpec((MOBA_ROWS, D_MODEL), lambda b, s: (0, 0)),
            _col_spec(HEAD_DIM), _col_spec(HEAD_DIM),
            _fm_spec(ROT_HALF), _fm_spec(ROT_HALF),
        ],
        out_specs=[
            _fm_spec(D_MODEL), _K_CHUNK_SPEC,
            pl.BlockSpec((1, 1, N_KV * HEAD_DIM, KM_LANES), lambda b, s: (b, s, 0, 0)),
            _V_CHUNK_SPEC, _fm_spec(D_MODEL),
        ],
        out_shape=[
            fm(D_MODEL, BF16), _K_CHUNK_SHAPE,
            jax.ShapeDtypeStruct((BATCH, SEQ // TS, N_KV * HEAD_DIM, KM_LANES), F32),
            _V_CHUNK_SHAPE, fm(D_MODEL, F32),
        ],
        compiler_params=_cparams(("parallel", "parallel")),
        name="moba_in_proj",
    )(xT, ng, sc, sh, wT, qg, kg, cosT, sinT)


CMP_TC = 1024
CMP_COLS = BATCH * N_KV * N_CMP_PAD
CMP_FEAT = CMP_STRIDE * HEAD_DIM


def _compress_kernel(x_ref, pea_ref, peb_ref, w1a_ref, w1b_ref, b1_ref, w2_ref, kg_ref,
                     cos_ref, sin_ref, o_ref):
    x = x_ref[0]
    u = _dot(w1a_ref[0], (x + pea_ref[0]).astype(BF16))
    v = _dot(w1b_ref[0], (x + peb_ref[0]).astype(BF16))
    v = pltpu.roll(v, CMP_TC - 1, 1)
    h = u + v + b1_ref[0]
    h = 0.5 * h * (1.0 + jnp.tanh(np.sqrt(2.0 / np.pi) * (h + 0.044715 * (h * h * h))))
    out = _dot(w2_ref[0], h.astype(BF16))
    is_key = pl.program_id(0) == 0
    keyed = _head_norm_rope(out, kg_ref[...], cos_ref[...], sin_ref[...])
    o_ref[0] = jnp.where(is_key, keyed, out)


def _compress_call(xT, pea, peb, w1a, w1b, b1, w2, kg, cosc, sinc):
    per_i = lambda shape: pl.BlockSpec((1,) + shape, lambda i, t: (i, 0, 0))
    return pl.pallas_call(
        _compress_kernel,
        grid=(2, CMP_COLS // CMP_TC),
        in_specs=[
            pl.BlockSpec((1, CMP_FEAT, CMP_TC), lambda i, t: (i, 0, t)),
            per_i((CMP_FEAT, 1)), per_i((CMP_FEAT, 1)),
            per_i((CMP_HID, CMP_FEAT)), per_i((CMP_HID, CMP_FEAT)),
            per_i((CMP_HID, 1)), per_i((HEAD_DIM, CMP_HID)),
            pl.BlockSpec((HEAD_DIM, 1), lambda i, t: (0, 0)),
            pl.BlockSpec((ROT_HALF, CMP_TC), lambda i, t: (0, t)),
            pl.BlockSpec((ROT_HALF, CMP_TC), lambda i, t: (0, t)),
        ],
        out_specs=pl.BlockSpec((1, HEAD_DIM, CMP_TC), lambda i, t: (i, 0, t)),
        out_shape=jax.ShapeDtypeStruct((2, HEAD_DIM, CMP_COLS), F32),
        compiler_params=_cparams(("parallel", "parallel")),
        name="nsa_compress",
    )(xT, pea, peb, w1a, w1b, b1, w2, kg, cosc, sinc)


def _group_queries(q_ref):
    return jnp.concatenate(
        [q_ref[0, r * HEAD_DIM:(r + 1) * HEAD_DIM, :] for r in range(GQA)], axis=1)


COL = 128


def _col_tiles():
    per_head = TQ // COL
    for ct in range(LANES_Q // COL):
        h = ct % per_head
        yield slice(h * COL, (h + 1) * COL), slice(ct * COL, (ct + 1) * COL)


def _flash_scratch():
    return [
        pltpu.VMEM((2, TQ, LANES_Q), F32),
        pltpu.VMEM((2, TQ, LANES_Q), BF16),
        pltpu.VMEM((2, 1, LANES_Q), F32),
        pltpu.VMEM((1, LANES_Q), F32),
        pltpu.VMEM((1, LANES_Q), F32),
        pltpu.VMEM((HEAD_DIM, LANES_Q), F32),
    ]


def _flash_branch(q_ref, k_ref, v_ref, bufs, own, n_past, past_chunk, own_fix, past_fix):
    s_buf, p_buf, a_buf, m_ref, l_ref, acc_ref = bufs

    def qk(chunk, slot):
        k = k_ref[0, 0, chunk]
        for r in range(GQA):
            s_buf[slot, :, r * TQ:(r + 1) * TQ] = _dot(k, q_ref[0, r * HEAD_DIM:(r + 1) * HEAD_DIM, :])

    def softmax(slot, fix, init):
        for qs, cs in _col_tiles():
            s = fix(s_buf[slot, :, cs], qs, cs)
            m_loc = jnp.max(s, axis=0, keepdims=True)
            if init:
                m_new = m_loc
                a_buf[slot, :, cs] = jnp.ones((1, COL), F32)
            else:
                m_old = m_ref[:, cs]
                m_new = jnp.maximum(m_old, m_loc)
                alpha = jnp.exp(m_old - m_new)
                a_buf[slot, :, cs] = alpha
            p = jnp.exp(s - m_new)
            p_sum = jnp.sum(p, axis=0, keepdims=True)
            l_ref[:, cs] = p_sum if init else alpha * l_ref[:, cs] + p_sum
            m_ref[:, cs] = m_new
            p_buf[slot, :, cs] = p.astype(BF16)

    def pv(chunk, slot):
        vT = v_ref[0, 0, chunk]
        for r in range(GQA):
            cs = slice(r * TQ, (r + 1) * TQ)
            acc_ref[:, cs] = a_buf[slot, :, cs] * acc_ref[:, cs] + _dot(vT, p_buf[slot, :, cs])

    def clamped_past(j):
        return past_chunk(jnp.clip(j, 0, jnp.maximum(n_past - 1, 0)))

    acc_ref[...] = jnp.zeros(acc_ref.shape, F32)
    qk(own, 0)
    qk(clamped_past(0), 1)
    softmax(0, own_fix, True)

    def trip(j, cur, prv):
        qk(clamped_past(j + 1), prv)
        softmax(cur, past_fix(j), False)
        pv(jnp.where(j == 0, own, clamped_past(j - 1)), prv)

    def pair(pp, carry):
        trip(2 * pp, 1, 0)
        trip(2 * pp + 1, 0, 1)
        return carry

    lax.fori_loop(0, n_past // 2, pair, 0)

    @pl.when(n_past % 2 == 1)
    def _():
        trip(n_past - 1, 1, 0)

    pv(jnp.where(n_past == 0, own, clamped_past(n_past - 1)), n_past % 2)
    return acc_ref[...] / l_ref[...]


def _causal_fix(keep_lower):
    a_idx = lax.broadcasted_iota(jnp.int32, (TQ, COL), 0)
    lane = lax.broadcasted_iota(jnp.int32, (TQ, COL), 1)

    def fix(s, qs, cs):
        lower = a_idx <= lane + qs.start
        return jnp.where(lower if keep_lower else ~lower, s, NEG)

    return fix


def _rank_select(score_ref, n_rows, j_idx, count):
    score = score_ref[...]
    cnt = jnp.zeros(score.shape, jnp.int32)
    for jp in range(n_rows):
        row = score_ref[jp:jp + 1, :]
        beats = (row > score) | ((row == score) & (jp < j_idx))
        cnt = cnt + beats.astype(jnp.int32)
    return cnt < count


def _store_out(o_ref, sz_ref, o):
    for r in range(GQA):
        rows = slice(r * HEAD_DIM, (r + 1) * HEAD_DIM)
        o_ref[0, rows, :] = (o[:, r * TQ:(r + 1) * TQ] * sz_ref[0, rows, :]).astype(BF16)


def _nsa_attn_kernel(q_ref, kc_ref, vc_ref, ks_ref, vs_ref, kw_ref, vw_ref, gt_ref, sz_ref,
                     selw_ref, o_ref, score_ref, bias_ref, *bufs):
    qi = pl.program_id(2)
    q4 = _group_queries(q_ref)

    t = qi * TQ + (lax.broadcasted_iota(jnp.int32, (1, LANES_Q), 1) & (TQ - 1))
    n_idx = lax.broadcasted_iota(jnp.int32, (N_CMP_PAD, LANES_Q), 0)
    s = jnp.where(n_idx * CMP_STRIDE + (CMP_LEN - 1) <= t, _dot(kc_ref[0, 0], q4), NEG)
    m = jnp.max(s, axis=0, keepdims=True)
    p = jnp.exp(s - m) * (m > 0.5 * NEG).astype(F32)
    p = p / jnp.maximum(jnp.sum(p, axis=0, keepdims=True), 1e-30)
    o_cmp = _dot(vc_ref[0, 0], p.astype(BF16))

    psum = p[:, 0:TQ]
    for r in range(1, GQA):
        psum = psum + p[:, r * TQ:(r + 1) * TQ]
    p_hi = psum.astype(BF16)
    p_lo = (psum - p_hi.astype(F32)).astype(BF16)
    imp = _dot(selw_ref[...], p_hi) + _dot(selw_ref[...], p_lo)
    j_idx = lax.broadcasted_iota(jnp.int32, (N_SEL, TQ), 0)
    cur = qi * (TQ // SEL_BLOCK) + (lax.broadcasted_iota(jnp.int32, (N_SEL, TQ), 1) >> 6)
    valid = j_idx <= cur
    forced = (j_idx == 0) | (j_idx == cur) | (j_idx == cur - 1)
    score_ref[...] = jnp.where(forced, jnp.inf, jnp.where(valid, imp, -jnp.inf))
    sel = valid & _rank_select(score_ref, N_SEL, j_idx, SEL_COUNT)
    bias = jnp.where(sel, 0.0, NEG)
    bias = jnp.concatenate([bias] * GQA, axis=1)
    for jp in range(N_SEL):
        bias_ref[jp] = bias[jp:jp + 1, :]

    keep_lower = _causal_fix(True)
    keep_upper = _causal_fix(False)

    def sel_fix(j, causal):
        per_chunk = TQ // SEL_BLOCK

        def fix(s, qs, cs):
            bias = jnp.concatenate(
                [jnp.broadcast_to(bias_ref[per_chunk * j + u, :, cs], (SEL_BLOCK, COL))
                 for u in range(per_chunk)], axis=0)
            s = s + bias
            return keep_lower(s, qs, cs) if causal else s

        return fix

    o_slc = _flash_branch(q_ref, ks_ref, vs_ref, bufs, qi, qi, lambda j: j,
                          sel_fix(qi, True), lambda j: sel_fix(j, False))

    o_win = _flash_branch(q_ref, kw_ref, vw_ref, bufs, qi, jnp.minimum(qi, 1),
                          lambda j: jnp.maximum(qi - 1, 0), keep_lower, lambda j: keep_upper)

    def gate(br):
        return jnp.concatenate(
            [gt_ref[0, br * GQA + r:br * GQA + r + 1, :] for r in range(GQA)], axis=1)

    o = gate(0) * o_cmp + gate(1) * o_slc + gate(2) * o_win
    _store_out(o_ref, sz_ref, o)


def _attn_specs():
    q_spec = pl.BlockSpec((1, GQA * HEAD_DIM, TQ), lambda b, g, i: (b, g, i))
    k_spec = pl.BlockSpec((1, 1, N_CHUNK, TQ, HEAD_DIM), lambda b, g, i: (b, g, 0, 0, 0))
    v_spec = pl.BlockSpec((1, 1, N_CHUNK, HEAD_DIM, TQ), lambda b, g, i: (b, g, 0, 0, 0))
    return q_spec, k_spec, v_spec


def _nsa_attn_call(q, kc, vc, ks, vs, kw, vw, gt, sz, selw):
    q_spec, k_spec, v_spec = _attn_specs()
    return pl.pallas_call(
        _nsa_attn_kernel,
        grid=(BATCH, N_KV, N_CHUNK),
        in_specs=[
            q_spec,
            pl.BlockSpec((1, 1, N_CMP_PAD, HEAD_DIM), lambda b, g, i: (b, g, 0, 0)),
            pl.BlockSpec((1, 1, HEAD_DIM, N_CMP_PAD), lambda b, g, i: (b, g, 0, 0)),
            k_spec, v_spec, k_spec, v_spec,
            pl.BlockSpec((1, GATE_ROWS, TQ), lambda b, g, i: (b, g, i)),
            q_spec,
            pl.BlockSpec((N_SEL, N_CMP_PAD), lambda b, g, i: (0, 0)),
        ],
        out_specs=q_spec,
        out_shape=jax.ShapeDtypeStruct((BATCH, D_MODEL, SEQ), BF16),
        scratch_shapes=[
            pltpu.VMEM((N_SEL, TQ), F32),
            pltpu.VMEM((N_SEL, 1, LANES_Q), F32),
        ] + _flash_scratch(),
        compiler_params=_cparams(("parallel", "parallel", "arbitrary")),
        name="nsa_attention",
    )(q, kc, vc, ks, vs, kw, vw, gt, sz, selw)


def _moba_attn_kernel(q_ref, km_ref, k_ref, v_ref, sz_ref, o_ref,
                      score_ref, bias_ref, *bufs):
    qi = pl.program_id(2)
    q4 = _group_queries(q_ref)

    j_idx = lax.broadcasted_iota(jnp.int32, (N_CHUNK, LANES_Q), 0)
    past = j_idx < qi
    score_ref[...] = jnp.where(past, _dot(km_ref[0, 0], q4), -jnp.inf)
    sel = past & _rank_select(score_ref, N_CHUNK, j_idx, MOBA_TOPK)
    bias = jnp.where(sel, 0.0, NEG)
    for jp in range(N_CHUNK):
        bias_ref[jp] = bias[jp:jp + 1, :]

    o = _flash_branch(q_ref, k_ref, v_ref, bufs, qi, qi, lambda j: j, _causal_fix(True),
                      lambda j: (lambda s, qs, cs: s + bias_ref[j, :, cs]))
    _store_out(o_ref, sz_ref, o)


def _moba_attn_call(q, km, k, v, sz):
    q_spec, k_spec, v_spec = _attn_specs()
    return pl.pallas_call(
        _moba_attn_kernel,
        grid=(BATCH, N_KV, N_CHUNK),
        in_specs=[
            q_spec,
            pl.BlockSpec((1, 1, N_CHUNK, HEAD_DIM), lambda b, g, i: (b, g, 0, 0)),
            k_spec, v_spec, q_spec,
        ],
        out_specs=q_spec,
        out_shape=jax.ShapeDtypeStruct((BATCH, D_MODEL, SEQ), BF16),
        scratch_shapes=[
            pltpu.VMEM((N_CHUNK, LANES_Q), F32),
            pltpu.VMEM((N_CHUNK, 1, LANES_Q), F32),
        ] + _flash_scratch(),
        compiler_params=_cparams(("parallel", "parallel", "arbitrary")),
        name="moba_attention",
    )(q, km, k, v, sz)


def _out_kernel(oz_ref, w_ref, x_ref, gt_ref, o_ref):
    oz = oz_ref[0]
    for cblk in range(D_MODEL // 256):
        rows = slice(cblk * 256, (cblk + 1) * 256)
        y = _dot(w_ref[rows, :], oz)
        o_ref[0, rows, :] = x_ref[0, rows, :] + gt_ref[0, rows, :] * y


def _out_call(oz, wT, xT, gate):
    return pl.pallas_call(
        _out_kernel,
        grid=(BATCH, SEQ // TS),
        in_specs=[
            _fm_spec(D_MODEL),
            pl.BlockSpec((D_MODEL, D_MODEL), lambda b, s: (0, 0)),
            _fm_spec(D_MODEL), _bcol_spec(D_MODEL),
        ],
        out_specs=_fm_spec(D_MODEL),
        out_shape=jax.ShapeDtypeStruct((BATCH, D_MODEL, SEQ), F32),
        compiler_params=_cparams(("parallel", "parallel")),
        name="out_proj",
    )(oz, wT, xT, gate)


def _rope_tables(pos):
    inv_freq = ROPE_THETA ** (-jnp.arange(0, 2 * ROT_HALF, 2, dtype=F32) / (2 * ROT_HALF))
    ang = pos.astype(F32)[..., None] * inv_freq
    return jnp.cos(ang), jnp.sin(ang)


def _gate_perm():
    perm = np.full((N_KV * GATE_ROWS,), 3 * N_HEADS, dtype=np.int32)
    for g in range(N_KV):
        for br in range(3):
            for r in range(GQA):
                perm[g * GATE_ROWS + br * GQA + r] = (g * GQA + r) * 3 + br
    return perm


def _sel_weights_T():
    cs = np.arange(N_CMP)[:, None] * CMP_STRIDE
    ss = np.arange(N_SEL)[None, :] * SEL_BLOCK
    shared = np.clip(np.minimum(cs + CMP_LEN, ss + SEL_BLOCK) - np.maximum(cs, ss), 0, None)
    w = np.zeros((N_CMP_PAD, N_SEL), np.float32)
    w[:N_CMP] = shared / CMP_LEN
    return jnp.asarray(w.T, dtype=BF16)


def _col(v):
    return v.reshape(-1, 1)


def _nsa_layer(xT, ng, sc, sh, gate, cosT, sinT, cosc, sinc, w_in, w_out, q_g, k_g,
               cmp_pe, cmp_w1, cmp_b1, cmp_w2, gate_b):
    widths = [1024] + [256] * 6 + [3 * N_HEADS, 1024]
    q, kc, vc, ks, vs, kw, vw, gl, z = jnp.split(w_in, np.cumsum(widths)[:-1].tolist(), axis=1)
    perm = _gate_perm()
    gl_p = jnp.concatenate([gl, jnp.zeros((D_MODEL, 1), F32)], axis=1)[:, perm]
    gb_p = jnp.concatenate([gate_b, jnp.zeros((1,), F32)])[perm]
    wT = jnp.concatenate([q, ks, kw, kc, vc, vs, vw, gl_p, z], axis=1).T.astype(BF16)

    qT, ks_c, kw_c, kcT, vcT, vs_c, vw_c, gates, sz = _nsa_in_call(
        xT, ng, sc, sh, wT, _col(q_g), _col(k_g[1]), _col(k_g[2]), _col(gb_p), cosT, sinT)

    def half_blocks(t):
        t = t.reshape(BATCH, N_KV, HEAD_DIM, N_CMP_PAD, CMP_STRIDE)
        return t.transpose(4, 2, 0, 1, 3).reshape(CMP_FEAT, CMP_COLS)

    x_cmp = jnp.stack([half_blocks(kcT), half_blocks(vcT)])
    pe = cmp_pe.reshape(2, 2, CMP_FEAT, 1)
    w1 = cmp_w1.reshape(2, 2, CMP_FEAT, CMP_HID).transpose(0, 1, 3, 2).astype(BF16)
    cmp_out = _compress_call(
        x_cmp, pe[:, 0], pe[:, 1], w1[:, 0], w1[:, 1], cmp_b1.reshape(2, CMP_HID, 1),
        cmp_w2.transpose(0, 2, 1).astype(BF16), _col(k_g[0]), cosc, sinc)
    cmp_out = cmp_out.reshape(2, HEAD_DIM, BATCH, N_KV, N_CMP_PAD)
    k_cmp = cmp_out[0].transpose(1, 2, 3, 0).astype(BF16)
    v_cmpT = cmp_out[1].transpose(1, 2, 0, 3).astype(BF16)

    oz = _nsa_attn_call(qT, k_cmp, v_cmpT, ks_c, vs_c, kw_c, vw_c, gates, sz, _sel_weights_T())
    return _out_call(oz, w_out.T.astype(BF16), xT, gate)


def _moba_layer(xT, ng, sc, sh, gate, cosT, sinT, w_in, w_out, q_g, k_g):
    wT = w_in.T.astype(BF16)
    qT, k_c, km, v_c, sz = _moba_in_call(xT, ng, sc, sh, wT, _col(q_g), _col(k_g), cosT, sinT)
    nb = TS // TQ
    km = km[..., :nb].reshape(BATCH, SEQ // TS, N_KV, HEAD_DIM, nb)
    km = km.transpose(0, 2, 1, 4, 3).reshape(BATCH, N_KV, N_CHUNK, HEAD_DIM).astype(BF16)
    oz = _moba_attn_call(qT, km, k_c, v_c, sz)
    return _out_call(oz, w_out.T.astype(BF16), xT, gate)


@jax.jit
def _forward(x, c, positions, norm_g, ada_w, ada_b, nsa_w_in, nsa_w_out, nsa_q_norm, nsa_k_norm,
             nsa_cmp_pe, nsa_cmp_w1, nsa_cmp_b1, nsa_cmp_w2, nsa_gate_b,
             moba_w_in, moba_w_out, moba_q_norm, moba_k_norm):
    cos, sin = _rope_tables(positions)
    cosT = cos.transpose(0, 2, 1)
    sinT = sin.transpose(0, 2, 1)
    cmp_end = np.minimum(np.arange(N_CMP_PAD) * CMP_STRIDE + CMP_LEN - 1, SEQ - 1)
    cos_c, sin_c = _rope_tables(positions[:, cmp_end])

    def cmp_table(t):
        t = jnp.broadcast_to(t.transpose(2, 0, 1)[:, :, None, :],
                             (ROT_HALF, BATCH, N_KV, N_CMP_PAD))
        return t.reshape(ROT_HALF, CMP_COLS)

    cosc = cmp_table(cos_c)
    sinc = cmp_table(sin_c)

    mod = _ada_call(c, ada_w, ada_b)
    shift, scale, gate = jnp.split(mod[..., None], 3, axis=2)

    xT = x.transpose(0, 2, 1)
    for i in range(DEPTH):
        j = i // 2
        ng = _col(norm_g[i])
        if i % 2 == 0:
            xT = _nsa_layer(xT, ng, scale[i], shift[i], gate[i], cosT, sinT, cosc, sinc,
                            nsa_w_in[j], nsa_w_out[j], nsa_q_norm[j], nsa_k_norm[j],
                            nsa_cmp_pe[j], nsa_cmp_w1[j], nsa_cmp_b1[j], nsa_cmp_w2[j],
                            nsa_gate_b[j])
        else:
            xT = _moba_layer(xT, ng, scale[i], shift[i], gate[i], cosT, sinT,
                             moba_w_in[j], moba_w_out[j], moba_q_norm[j], moba_k_norm[j])
    return xT.transpose(0, 2, 1)


def kernel(x, c, positions, norm_g, ada_w, ada_b, nsa_w_in, nsa_w_out, nsa_q_norm, nsa_k_norm, nsa_cmp_pe, nsa_cmp_w1, nsa_cmp_b1, nsa_cmp_w2, nsa_gate_b, moba_w_in, moba_w_out, moba_q_norm, moba_k_norm):
    return _forward(x, c, positions, norm_g, ada_w, ada_b, nsa_w_in, nsa_w_out, nsa_q_norm,
                    nsa_k_norm, nsa_cmp_pe, nsa_cmp_w1, nsa_cmp_b1, nsa_cmp_w2, nsa_gate_b,
                    moba_w_in, moba_w_out, moba_q_norm, moba_k_norm)
```

```python
import functools

import numpy as np
import jax
import jax.numpy as jnp
from jax import lax
from jax.experimental import pallas as pl
from jax.experimental.pallas import tpu as pltpu

D_MODEL = 1024
BATCH = 16
SEQ = 2048
DEPTH = 4
HEAD_DIM = 64
N_HEADS = 16
N_KV = 4
GQA = 4
ROT_HALF = 8
ROPE_THETA = 500000.0
NORM_EPS = 1e-6
CMP_LEN = 32
CMP_STRIDE = 16
CMP_HID = 256
SEL_BLOCK = 64
SEL_COUNT = 8
N_SEL = SEQ // SEL_BLOCK
N_CMP = (SEQ - CMP_LEN) // CMP_STRIDE + 1
N_CMP_PAD = 128
MOBA_TOPK = 3

TQ = 256
N_CHUNK = SEQ // TQ
LANES_Q = GQA * TQ
TS = 512
NEG = -1e30
BIAS_ROWS = 16
K_AUG = HEAD_DIM + BIAS_ROWS
Q_SCALE = HEAD_DIM ** -0.5 * float(np.log2(np.e))
GATE_ROWS = 16

F32 = jnp.float32
BF16 = jnp.bfloat16

VMEM_LIMIT = 52 * 1024 * 1024


def _cparams(sem):
    return pltpu.CompilerParams(dimension_semantics=sem, vmem_limit_bytes=VMEM_LIMIT)


def _dot(a, b):
    return jnp.dot(a, b, preferred_element_type=F32)


def _ada_kernel(c_ref, w_ref, b_ref, o_ref):
    cond = c_ref[...]
    cond = cond * jax.nn.sigmoid(cond)
    o_ref[0] = jnp.dot(cond, w_ref[0], precision=lax.Precision.HIGHEST,
                       preferred_element_type=F32) + b_ref[0]


def _ada_call(c, ada_w, ada_b):
    nt = 1024
    return pl.pallas_call(
        _ada_kernel,
        grid=(DEPTH, 3 * D_MODEL // nt),
        in_specs=[
            pl.BlockSpec((BATCH, D_MODEL), lambda i, n: (0, 0)),
            pl.BlockSpec((1, D_MODEL, nt), lambda i, n: (i, 0, n)),
            pl.BlockSpec((1, 1, nt), lambda i, n: (i, 0, n)),
        ],
        out_specs=pl.BlockSpec((1, BATCH, nt), lambda i, n: (i, 0, n)),
        out_shape=jax.ShapeDtypeStruct((DEPTH, BATCH, 3 * D_MODEL), F32),
        compiler_params=_cparams(("parallel", "parallel")),
        name="ada_mod",
    )(c, ada_w, ada_b.reshape(DEPTH, 1, 3 * D_MODEL))


def _norm_mod(x_ref, ng_ref, sc_ref, sh_ref):
    x = x_ref[0]
    ms = jnp.mean(x * x, axis=0, keepdims=True)
    y = x * lax.rsqrt(ms + NORM_EPS)
    h = (y * ng_ref[...]) * (1.0 + sc_ref[0]) + sh_ref[0]
    return h.astype(BF16)


def _head_norm_rope(y, g, cos, sin):
    ms = jnp.mean(y * y, axis=0, keepdims=True)
    yn = (y * lax.rsqrt(ms + NORM_EPS)) * g
    x1 = yn[0:ROT_HALF]
    x2 = yn[ROT_HALF:2 * ROT_HALF]
    return jnp.concatenate([x1 * cos - x2 * sin, x2 * cos + x1 * sin, yn[2 * ROT_HALF:]], axis=0)


def _store_k_chunks(o_ref, gi, yh, ind=None):
    for it in range(TS // TQ):
        chunk = yh[:, it * TQ:(it + 1) * TQ]
        if ind is None:
            o_ref[0, gi, it] = chunk.T.astype(BF16)
        else:
            aug = jnp.concatenate([chunk, ind], axis=0).T
            o_ref[0, gi, it] = aug[:, :K_AUG].astype(BF16)


def _block_indicator(blocks):
    row = lax.broadcasted_iota(jnp.int32, (HEAD_DIM, TQ), 0)
    lane = lax.broadcasted_iota(jnp.int32, (HEAD_DIM, TQ), 1)
    return (lane // (TQ // blocks) == row).astype(F32)


def _store_v_chunks(o_ref, y):
    yb = y.astype(BF16)
    for gi in range(N_KV):
        for it in range(TS // TQ):
            o_ref[0, gi, it] = yb[gi * HEAD_DIM:(gi + 1) * HEAD_DIM, it * TQ:(it + 1) * TQ]


NSA_OFF_Q = 0
NSA_OFF_KS = 1024
NSA_OFF_KW = 1280
NSA_OFF_KC = 1536
NSA_OFF_VC = 1792
NSA_OFF_VS = 2048
NSA_OFF_VW = 2304
NSA_OFF_GT = 2560
NSA_OFF_Z = NSA_OFF_GT + N_KV * GATE_ROWS
NSA_ROWS = NSA_OFF_Z + D_MODEL


def _nsa_in_kernel(x_ref, ng_ref, sc_ref, sh_ref, w_ref, qg_ref, ksg_ref, kwg_ref, gb_ref,
                   cos_ref, sin_ref,
                   q_ref, ks_ref, kw_ref, kc_ref, vc_ref, vs_ref, vw_ref, gt_ref, sz_ref):
    hb = _norm_mod(x_ref, ng_ref, sc_ref, sh_ref)
    cos = cos_ref[0]
    sin = sin_ref[0]

    def proj(r0, n):
        return _dot(w_ref[r0:r0 + n, :], hb)

    for gi in range(N_KV):
        y = proj(NSA_OFF_Q + gi * 256, 256)
        for r in range(GQA):
            yh = _head_norm_rope(y[r * HEAD_DIM:(r + 1) * HEAD_DIM], qg_ref[...], cos, sin)
            row = gi * 256 + r * HEAD_DIM
            q_ref[0, row:row + HEAD_DIM, :] = (yh * Q_SCALE).astype(BF16)
    sel_ind = _block_indicator(TQ // SEL_BLOCK)
    for off, g_ref, o_ref, ind in ((NSA_OFF_KS, ksg_ref, ks_ref, sel_ind),
                                   (NSA_OFF_KW, kwg_ref, kw_ref, None)):
        y = proj(off, 256)
        for gi in range(N_KV):
            yh = _head_norm_rope(y[gi * HEAD_DIM:(gi + 1) * HEAD_DIM], g_ref[...], cos, sin)
            _store_k_chunks(o_ref, gi, yh, ind)
    kc_ref[0] = proj(NSA_OFF_KC, 256)
    vc_ref[0] = proj(NSA_OFF_VC, 256)
    _store_v_chunks(vs_ref, proj(NSA_OFF_VS, 256))
    _store_v_chunks(vw_ref, proj(NSA_OFF_VW, 256))
    gt_ref[0] = jax.nn.sigmoid(proj(NSA_OFF_GT, N_KV * GATE_ROWS) + gb_ref[...])
    for cblk in range(D_MODEL // 256):
        z = proj(NSA_OFF_Z + cblk * 256, 256)
        sz_ref[0, cblk * 256:(cblk + 1) * 256, :] = z * jax.nn.sigmoid(z)


def _col_spec(rows):
    return pl.BlockSpec((rows, 1), lambda b, s: (0, 0))


def _bcol_spec(rows):
    return pl.BlockSpec((1, rows, 1), lambda b, s: (b, 0, 0))


def _fm_spec(rows):
    return pl.BlockSpec((1, rows, TS), lambda b, s: (b, 0, s))


def _k_chunk_spec(width):
    return pl.BlockSpec((1, N_KV, TS // TQ, TQ, width), lambda b, s: (b, 0, s, 0, 0))


def _k_chunk_shape(width):
    return jax.ShapeDtypeStruct((BATCH, N_KV, N_CHUNK, TQ, width), BF16)


_V_CHUNK_SPEC = pl.BlockSpec((1, N_KV, TS // TQ, HEAD_DIM, TQ), lambda b, s: (b, 0, s, 0, 0))
_V_CHUNK_SHAPE = jax.ShapeDtypeStruct((BATCH, N_KV, N_CHUNK, HEAD_DIM, TQ), BF16)


def _nsa_in_call(xT, ng, sc, sh, wT, qg, ksg, kwg, gb, cosT, sinT):
    fm = lambda rows, dt: jax.ShapeDtypeStruct((BATCH, rows, SEQ), dt)
    return pl.pallas_call(
        _nsa_in_kernel,
        grid=(BATCH, SEQ // TS),
        in_specs=[
            _fm_spec(D_MODEL), _col_spec(D_MODEL), _bcol_spec(D_MODEL), _bcol_spec(D_MODEL),
            pl.BlockSpec((NSA_ROWS, D_MODEL), lambda b, s: (0, 0)),
            _col_spec(HEAD_DIM), _col_spec(HEAD_DIM), _col_spec(HEAD_DIM),
            _col_spec(N_KV * GATE_ROWS),
            _fm_spec(ROT_HALF), _fm_spec(ROT_HALF),
        ],
        out_specs=[
            _fm_spec(D_MODEL), _k_chunk_spec(K_AUG), _k_chunk_spec(HEAD_DIM), _fm_spec(256), _fm_spec(256),
            _V_CHUNK_SPEC, _V_CHUNK_SPEC, _fm_spec(N_KV * GATE_ROWS), _fm_spec(D_MODEL),
        ],
        out_shape=[
            fm(D_MODEL, BF16), _k_chunk_shape(K_AUG), _k_chunk_shape(HEAD_DIM), fm(256, F32), fm(256, F32),
            _V_CHUNK_SHAPE, _V_CHUNK_SHAPE, fm(N_KV * GATE_ROWS, F32), fm(D_MODEL, F32),
        ],
        compiler_params=_cparams(("parallel", "parallel")),
        name="nsa_in_proj",
    )(xT, ng, sc, sh, wT, qg, ksg, kwg, gb, cosT, sinT)


MOBA_OFF_Q = 0
MOBA_OFF_K = 1024
MOBA_OFF_V = 1280
MOBA_OFF_Z = 1536
MOBA_ROWS = 2560
KM_LANES = 128


def _moba_in_kernel(x_ref, ng_ref, sc_ref, sh_ref, w_ref, qg_ref, kg_ref, cos_ref, sin_ref,
                    q_ref, k_ref, km_ref, v_ref, sz_ref):
    hb = _norm_mod(x_ref, ng_ref, sc_ref, sh_ref)
    cos = cos_ref[0]
    sin = sin_ref[0]

    def proj(r0, n):
        return _dot(w_ref[r0:r0 + n, :], hb)

    for gi in range(N_KV):
        y = proj(MOBA_OFF_Q + gi * 256, 256)
        for r in range(GQA):
            yh = _head_norm_rope(y[r * HEAD_DIM:(r + 1) * HEAD_DIM], qg_ref[...], cos, sin)
            row = gi * 256 + r * HEAD_DIM
            q_ref[0, row:row + HEAD_DIM, :] = (yh * Q_SCALE).astype(BF16)
    y = proj(MOBA_OFF_K, 256)
    lane = lax.broadcasted_iota(jnp.int32, (HEAD_DIM, KM_LANES), 1)
    for gi in range(N_KV):
        yh = _head_norm_rope(y[gi * HEAD_DIM:(gi + 1) * HEAD_DIM], kg_ref[...], cos, sin)
        _store_k_chunks(k_ref, gi, yh, _block_indicator(1))
        km = jnp.zeros((HEAD_DIM, KM_LANES), F32)
        for it in range(TS // TQ):
            mean = jnp.mean(yh[:, it * TQ:(it + 1) * TQ], axis=1, keepdims=True)
            km = jnp.where(lane == it, mean, km)
        km_ref[0, 0, gi * HEAD_DIM:(gi + 1) * HEAD_DIM, :] = km
    _store_v_chunks(v_ref, proj(MOBA_OFF_V, 256))
    for cblk in range(D_MODEL // 256):
        z = proj(MOBA_OFF_Z + cblk * 256, 256)
        sz_ref[0, cblk * 256:(cblk + 1) * 256, :] = z * jax.nn.sigmoid(z)


def _moba_in_call(xT, ng, sc, sh, wT, qg, kg, cosT, sinT):
    fm = lambda rows, dt: jax.ShapeDtypeStruct((BATCH, rows, SEQ), dt)
    return pl.pallas_call(
        _moba_in_kernel,
        grid=(BATCH, SEQ // TS),
        in_specs=[
            _fm_spec(D_MODEL), _col_spec(D_MODEL), _bcol_spec(D_MODEL), _bcol_spec(D_MODEL),
            pl.BlockSpec((MOBA_ROWS, D_MODEL), lambda b, s: (0, 0)),
            _col_spec(HEAD_DIM), _col_spec(HEAD_DIM),
            _fm_spec(ROT_HALF), _fm_spec(ROT_HALF),
        ],
        out_specs=[
            _fm_spec(D_MODEL), _k_chunk_spec(K_AUG),
            pl.BlockSpec((1, 1, N_KV * HEAD_DIM, KM_LANES), lambda b, s: (b, s, 0, 0)),
            _V_CHUNK_SPEC, _fm_spec(D_MODEL),
        ],
        out_shape=[
            fm(D_MODEL, BF16), _k_chunk_shape(K_AUG),
            jax.ShapeDtypeStruct((BATCH, SEQ // TS, N_KV * HEAD_DIM, KM_LANES), F32),
            _V_CHUNK_SHAPE, fm(D_MODEL, F32),
        ],
        compiler_params=_cparams(("parallel", "parallel")),
        name="moba_in_proj",
    )(xT, ng, sc, sh, wT, qg, kg, cosT, sinT)


CMP_TC = 1024
CMP_COLS = BATCH * N_KV * N_CMP_PAD
CMP_FEAT = CMP_STRIDE * HEAD_DIM


def _compress_kernel(x_ref, pea_ref, peb_ref, w1a_ref, w1b_ref, b1_ref, w2_ref, kg_ref,
                     cos_ref, sin_ref, o_ref):
    x = x_ref[0]
    u = _dot(w1a_ref[0], (x + pea_ref[0]).astype(BF16))
    v = _dot(w1b_ref[0], (x + peb_ref[0]).astype(BF16))
    v = pltpu.roll(v, CMP_TC - 1, 1)
    h = u + v + b1_ref[0]
    h = 0.5 * h * (1.0 + jnp.tanh(np.sqrt(2.0 / np.pi) * (h + 0.044715 * (h * h * h))))
    out = _dot(w2_ref[0], h.astype(BF16))
    is_key = pl.program_id(0) == 0
    keyed = _head_norm_rope(out, kg_ref[...], cos_ref[...], sin_ref[...])
    o_ref[0] = jnp.where(is_key, keyed, out)


def _compress_call(xT, pea, peb, w1a, w1b, b1, w2, kg, cosc, sinc):
    per_i = lambda shape: pl.BlockSpec((1,) + shape, lambda i, t: (i, 0, 0))
    return pl.pallas_call(
        _compress_kernel,
        grid=(2, CMP_COLS // CMP_TC),
        in_specs=[
            pl.BlockSpec((1, CMP_FEAT, CMP_TC), lambda i, t: (i, 0, t)),
            per_i((CMP_FEAT, 1)), per_i((CMP_FEAT, 1)),
            per_i((CMP_HID, CMP_FEAT)), per_i((CMP_HID, CMP_FEAT)),
            per_i((CMP_HID, 1)), per_i((HEAD_DIM, CMP_HID)),
            pl.BlockSpec((HEAD_DIM, 1), lambda i, t: (0, 0)),
            pl.BlockSpec((ROT_HALF, CMP_TC), lambda i, t: (0, t)),
            pl.BlockSpec((ROT_HALF, CMP_TC), lambda i, t: (0, t)),
        ],
        out_specs=pl.BlockSpec((1, HEAD_DIM, CMP_TC), lambda i, t: (i, 0, t)),
        out_shape=jax.ShapeDtypeStruct((2, HEAD_DIM, CMP_COLS), F32),
        compiler_params=_cparams(("parallel", "parallel")),
        name="nsa_compress",
    )(xT, pea, peb, w1a, w1b, b1, w2, kg, cosc, sinc)


def _group_queries(q_ref):
    return jnp.concatenate(
        [q_ref[0, r * HEAD_DIM:(r + 1) * HEAD_DIM, :] for r in range(GQA)], axis=1)


COL = 128


def _col_tiles():
    per_head = TQ // COL
    for ct in range(LANES_Q // COL):
        r, h = divmod(ct, per_head)
        yield r, slice(h * COL, (h + 1) * COL), slice(ct * COL, (ct + 1) * COL)


def _flash_scratch():
    return [
        pltpu.VMEM((2, TQ, LANES_Q), BF16),
        pltpu.VMEM((2, 1, LANES_Q), F32),
        pltpu.VMEM((1, LANES_Q), F32),
        pltpu.VMEM((1, LANES_Q), F32),
        pltpu.VMEM((HEAD_DIM, LANES_Q), F32),
    ]


def _flash_branch(q_ref, k_ref, v_ref, bufs, own, n_past, past_chunk, own_fix=None, past_fix=None,
                  own_rows=None, past_rows=None):
    p_buf, a_buf, m_ref, l_ref, acc_ref = bufs

    def softmax_tile(s, cs, slot, init):
        m_loc = jnp.max(s, axis=0, keepdims=True)
        if init:
            m_new = m_loc
            a_buf[slot, :, cs] = jnp.ones((1, COL), F32)
        else:
            m_old = m_ref[:, cs]
            m_new = jnp.maximum(m_old, m_loc)
            alpha = jnp.exp2(m_old - m_new)
            a_buf[slot, :, cs] = alpha
        p = jnp.exp2(s - m_new)
        p_sum = jnp.sum(p, axis=0, keepdims=True)
        l_ref[:, cs] = p_sum if init else alpha * l_ref[:, cs] + p_sum
        m_ref[:, cs] = m_new
        p_buf[slot, :, cs] = p.astype(BF16)

    def scores(k, r, qs, cs, rows, fix):
        q = q_ref[0, r * HEAD_DIM:(r + 1) * HEAD_DIM, qs]
        if rows is not None:
            q = jnp.concatenate([q, rows(cs)], axis=0)
        s = _dot(k, q)
        return s if fix is None else fix(s, qs)

    def pv_tile(vT, slot, cs):
        acc_ref[:, cs] = a_buf[slot, :, cs] * acc_ref[:, cs] + _dot(vT, p_buf[slot, :, cs])

    acc_ref[...] = jnp.zeros(acc_ref.shape, F32)
    k_own = k_ref[0, 0, own]
    for r, qs, cs in _col_tiles():
        softmax_tile(scores(k_own, r, qs, cs, own_rows, own_fix), cs, 0, True)

    def trip(j, cur, prv):
        k = k_ref[0, 0, past_chunk(j)]
        vT = v_ref[0, 0, jnp.where(j == 0, own, past_chunk(jnp.maximum(j - 1, 0)))]
        rows = None if past_rows is None else past_rows(j)
        for r, qs, cs in _col_tiles():
            s = scores(k, r, qs, cs, rows, past_fix)
            pv_tile(vT, prv, cs)
            softmax_tile(s, cs, cur, False)

    def pair(pp, carry):
        trip(2 * pp, 1, 0)
        trip(2 * pp + 1, 0, 1)
        return carry

    lax.fori_loop(0, n_past // 2, pair, 0)

    @pl.when(n_past % 2 == 1)
    def _():
        trip(n_past - 1, 1, 0)

    vT = v_ref[0, 0, jnp.where(n_past == 0, own, past_chunk(jnp.maximum(n_past - 1, 0)))]
    for _, _, cs in _col_tiles():
        pv_tile(vT, n_past % 2, cs)
    return acc_ref[...] / l_ref[...]


def _causal_fix(keep_lower):
    a_idx = lax.broadcasted_iota(jnp.int32, (TQ, COL), 0)
    lane = lax.broadcasted_iota(jnp.int32, (TQ, COL), 1)

    def fix(s, qs):
        lower = a_idx <= lane + qs.start
        return jnp.where(lower if keep_lower else ~lower, s, NEG)

    return fix


def _rank_select(score_ref, n_rows, j_idx, count):
    score = score_ref[...]
    cnt = jnp.zeros(score.shape, jnp.int32)
    for jp in range(n_rows):
        row = score_ref[jp:jp + 1, :]
        beats = (row > score) | ((row == score) & (jp < j_idx))
        cnt = cnt + beats.astype(jnp.int32)
    return cnt < count


def _store_out(o_ref, sz_ref, o):
    for r in range(GQA):
        rows = slice(r * HEAD_DIM, (r + 1) * HEAD_DIM)
        o_ref[0, rows, :] = (o[:, r * TQ:(r + 1) * TQ] * sz_ref[0, rows, :]).astype(BF16)


def _nsa_attn_kernel(q_ref, kc_ref, vc_ref, ks_ref, vs_ref, kw_ref, vw_ref, gt_ref, sz_ref,
                     selw_ref, o_ref, score_ref, bias_ref, *bufs):
    qi = pl.program_id(2)
    q4 = _group_queries(q_ref)

    t = qi * TQ + (lax.broadcasted_iota(jnp.int32, (1, LANES_Q), 1) & (TQ - 1))
    n_idx = lax.broadcasted_iota(jnp.int32, (N_CMP_PAD, LANES_Q), 0)
    s = jnp.where(n_idx * CMP_STRIDE + (CMP_LEN - 1) <= t, _dot(kc_ref[0, 0], q4), NEG)
    m = jnp.max(s, axis=0, keepdims=True)
    p = jnp.exp2(s - m) * (m > 0.5 * NEG).astype(F32)
    p = p / jnp.maximum(jnp.sum(p, axis=0, keepdims=True), 1e-30)
    o_cmp = _dot(vc_ref[0, 0], p.astype(BF16))

    psum = p[:, 0:TQ]
    for r in range(1, GQA):
        psum = psum + p[:, r * TQ:(r + 1) * TQ]
    p_hi = psum.astype(BF16)
    p_lo = (psum - p_hi.astype(F32)).astype(BF16)
    imp = _dot(selw_ref[...], p_hi) + _dot(selw_ref[...], p_lo)
    j_idx = lax.broadcasted_iota(jnp.int32, (N_SEL, TQ), 0)
    cur = qi * (TQ // SEL_BLOCK) + (lax.broadcasted_iota(jnp.int32, (N_SEL, TQ), 1) >> 6)
    valid = j_idx <= cur
    forced = (j_idx == 0) | (j_idx == cur) | (j_idx == cur - 1)
    score_ref[...] = jnp.where(forced, jnp.inf, jnp.where(valid, imp, -jnp.inf))
    sel = valid & _rank_select(score_ref, N_SEL, j_idx, SEL_COUNT)
    bias = jnp.where(sel, 0.0, NEG)
    bias = jnp.concatenate([bias] * GQA, axis=1)
    per_chunk = TQ // SEL_BLOCK
    bias_ref[...] = jnp.zeros(bias_ref.shape, F32)
    for jp in range(N_SEL):
        bias_ref[jp // per_chunk, jp % per_chunk:jp % per_chunk + 1, :] = bias[jp:jp + 1, :]

    keep_lower = _causal_fix(True)
    keep_upper = _causal_fix(False)

    def sel_rows(j):
        return lambda cs: bias_ref[j, :, cs].astype(BF16)

    o_slc = _flash_branch(q_ref, ks_ref, vs_ref, bufs, qi, qi, lambda j: j, own_fix=keep_lower,
                          own_rows=sel_rows(qi), past_rows=sel_rows)

    o_win = _flash_branch(q_ref, kw_ref, vw_ref, bufs, qi, jnp.minimum(qi, 1),
                          lambda j: jnp.maximum(qi - 1, 0), own_fix=keep_lower, past_fix=keep_upper)

    def gate(br):
        return jnp.concatenate(
            [gt_ref[0, br * GQA + r:br * GQA + r + 1, :] for r in range(GQA)], axis=1)

    o = gate(0) * o_cmp + gate(1) * o_slc + gate(2) * o_win
    _store_out(o_ref, sz_ref, o)


def _attn_specs():
    q_spec = pl.BlockSpec((1, GQA * HEAD_DIM, TQ), lambda b, g, i: (b, g, i))
    k_spec = lambda width: pl.BlockSpec((1, 1, N_CHUNK, TQ, width), lambda b, g, i: (b, g, 0, 0, 0))
    v_spec = pl.BlockSpec((1, 1, N_CHUNK, HEAD_DIM, TQ), lambda b, g, i: (b, g, 0, 0, 0))
    return q_spec, k_spec, v_spec


def _nsa_attn_call(q, kc, vc, ks, vs, kw, vw, gt, sz, selw):
    q_spec, k_spec, v_spec = _attn_specs()
    return pl.pallas_call(
        _nsa_attn_kernel,
        grid=(BATCH, N_KV, N_CHUNK),
        in_specs=[
            q_spec,
            pl.BlockSpec((1, 1, N_CMP_PAD, HEAD_DIM), lambda b, g, i: (b, g, 0, 0)),
            pl.BlockSpec((1, 1, HEAD_DIM, N_CMP_PAD), lambda b, g, i: (b, g, 0, 0)),
            k_spec(K_AUG), v_spec, k_spec(HEAD_DIM), v_spec,
            pl.BlockSpec((1, GATE_ROWS, TQ), lambda b, g, i: (b, g, i)),
            q_spec,
            pl.BlockSpec((N_SEL, N_CMP_PAD), lambda b, g, i: (0, 0)),
        ],
        out_specs=q_spec,
        out_shape=jax.ShapeDtypeStruct((BATCH, D_MODEL, SEQ), BF16),
        scratch_shapes=[
            pltpu.VMEM((N_SEL, TQ), F32),
            pltpu.VMEM((N_CHUNK, BIAS_ROWS, LANES_Q), F32),
        ] + _flash_scratch(),
        compiler_params=_cparams(("parallel", "parallel", "arbitrary")),
        name="nsa_attention",
    )(q, kc, vc, ks, vs, kw, vw, gt, sz, selw)


def _moba_attn_kernel(q_ref, km_ref, k_ref, v_ref, sz_ref, o_ref,
                      score_ref, bias_ref, *bufs):
    qi = pl.program_id(2)
    q4 = _group_queries(q_ref)

    j_idx = lax.broadcasted_iota(jnp.int32, (N_CHUNK, LANES_Q), 0)
    past = j_idx < qi
    score_ref[...] = jnp.where(past, _dot(km_ref[0, 0], q4), -jnp.inf)
    sel = past & _rank_select(score_ref, N_CHUNK, j_idx, MOBA_TOPK)
    bias = jnp.where(sel, 0.0, NEG)
    bias_ref[...] = jnp.zeros(bias_ref.shape, F32)
    for jp in range(N_CHUNK):
        bias_ref[jp, 0:1, :] = bias[jp:jp + 1, :]

    o = _flash_branch(q_ref, k_ref, v_ref, bufs, qi, qi, lambda j: j, own_fix=_causal_fix(True),
                      own_rows=lambda cs: jnp.zeros((BIAS_ROWS, COL), BF16),
                      past_rows=lambda j: (lambda cs: bias_ref[j, :, cs].astype(BF16)))
    _store_out(o_ref, sz_ref, o)


def _moba_attn_call(q, km, k, v, sz):
    q_spec, k_spec, v_spec = _attn_specs()
    return pl.pallas_call(
        _moba_attn_kernel,
        grid=(BATCH, N_KV, N_CHUNK),
        in_specs=[
            q_spec,
            pl.BlockSpec((1, 1, N_CHUNK, HEAD_DIM), lambda b, g, i: (b, g, 0, 0)),
            k_spec(K_AUG), v_spec, q_spec,
        ],
        out_specs=q_spec,
        out_shape=jax.ShapeDtypeStruct((BATCH, D_MODEL, SEQ), BF16),
        scratch_shapes=[
            pltpu.VMEM((N_CHUNK, LANES_Q), F32),
            pltpu.VMEM((N_CHUNK, BIAS_ROWS, LANES_Q), F32),
        ] + _flash_scratch(),
        compiler_params=_cparams(("parallel", "parallel", "arbitrary")),
        name="moba_attention",
    )(q, km, k, v, sz)


def _out_kernel(oz_ref, w_ref, x_ref, gt_ref, o_ref):
    oz = oz_ref[0]
    for cblk in range(D_MODEL // 256):
        rows = slice(cblk * 256, (cblk + 1) * 256)
        y = _dot(w_ref[rows, :], oz)
        o_ref[0, rows, :] = x_ref[0, rows, :] + gt_ref[0, rows, :] * y


def _out_call(oz, wT, xT, gate):
    return pl.pallas_call(
        _out_kernel,
        grid=(BATCH, SEQ // TS),
        in_specs=[
            _fm_spec(D_MODEL),
            pl.BlockSpec((D_MODEL, D_MODEL), lambda b, s: (0, 0)),
            _fm_spec(D_MODEL), _bcol_spec(D_MODEL),
        ],
        out_specs=_fm_spec(D_MODEL),
        out_shape=jax.ShapeDtypeStruct((BATCH, D_MODEL, SEQ), F32),
        compiler_params=_cparams(("parallel", "parallel")),
        name="out_proj",
    )(oz, wT, xT, gate)


def _rope_tables(pos):
    inv_freq = ROPE_THETA ** (-jnp.arange(0, 2 * ROT_HALF, 2, dtype=F32) / (2 * ROT_HALF))
    ang = pos.astype(F32)[..., None] * inv_freq
    return jnp.cos(ang), jnp.sin(ang)


def _gate_perm():
    perm = np.full((N_KV * GATE_ROWS,), 3 * N_HEADS, dtype=np.int32)
    for g in range(N_KV):
        for br in range(3):
            for r in range(GQA):
                perm[g * GATE_ROWS + br * GQA + r] = (g * GQA + r) * 3 + br
    return perm


def _sel_weights_T():
    cs = np.arange(N_CMP)[:, None] * CMP_STRIDE
    ss = np.arange(N_SEL)[None, :] * SEL_BLOCK
    shared = np.clip(np.minimum(cs + CMP_LEN, ss + SEL_BLOCK) - np.maximum(cs, ss), 0, None)
    w = np.zeros((N_CMP_PAD, N_SEL), np.float32)
    w[:N_CMP] = shared / CMP_LEN
    return jnp.asarray(w.T, dtype=BF16)


def _col(v):
    return v.reshape(-1, 1)


def _nsa_layer(xT, ng, sc, sh, gate, cosT, sinT, cosc, sinc, w_in, w_out, q_g, k_g,
               cmp_pe, cmp_w1, cmp_b1, cmp_w2, gate_b):
    widths = [1024] + [256] * 6 + [3 * N_HEADS, 1024]
    q, kc, vc, ks, vs, kw, vw, gl, z = jnp.split(w_in, np.cumsum(widths)[:-1].tolist(), axis=1)
    perm = _gate_perm()
    gl_p = jnp.concatenate([gl, jnp.zeros((D_MODEL, 1), F32)], axis=1)[:, perm]
    gb_p = jnp.concatenate([gate_b, jnp.zeros((1,), F32)])[perm]
    wT = jnp.concatenate([q, ks, kw, kc, vc, vs, vw, gl_p, z], axis=1).T.astype(BF16)

    qT, ks_c, kw_c, kcT, vcT, vs_c, vw_c, gates, sz = _nsa_in_call(
        xT, ng, sc, sh, wT, _col(q_g), _col(k_g[1]), _col(k_g[2]), _col(gb_p), cosT, sinT)

    def half_blocks(t):
        t = t.reshape(BATCH, N_KV, HEAD_DIM, N_CMP_PAD, CMP_STRIDE)
        return t.transpose(4, 2, 0, 1, 3).reshape(CMP_FEAT, CMP_COLS)

    x_cmp = jnp.stack([half_blocks(kcT), half_blocks(vcT)])
    pe = cmp_pe.reshape(2, 2, CMP_FEAT, 1)
    w1 = cmp_w1.reshape(2, 2, CMP_FEAT, CMP_HID).transpose(0, 1, 3, 2).astype(BF16)
    cmp_out = _compress_call(
        x_cmp, pe[:, 0], pe[:, 1], w1[:, 0], w1[:, 1], cmp_b1.reshape(2, CMP_HID, 1),
        cmp_w2.transpose(0, 2, 1).astype(BF16), _col(k_g[0]), cosc, sinc)
    cmp_out = cmp_out.reshape(2, HEAD_DIM, BATCH, N_KV, N_CMP_PAD)
    k_cmp = cmp_out[0].transpose(1, 2, 3, 0).astype(BF16)
    v_cmpT = cmp_out[1].transpose(1, 2, 0, 3).astype(BF16)

    oz = _nsa_attn_call(qT, k_cmp, v_cmpT, ks_c, vs_c, kw_c, vw_c, gates, sz, _sel_weights_T())
    return _out_call(oz, w_out.T.astype(BF16), xT, gate)


def _moba_layer(xT, ng, sc, sh, gate, cosT, sinT, w_in, w_out, q_g, k_g):
    wT = w_in.T.astype(BF16)
    qT, k_c, km, v_c, sz = _moba_in_call(xT, ng, sc, sh, wT, _col(q_g), _col(k_g), cosT, sinT)
    nb = TS // TQ
    km = km[..., :nb].reshape(BATCH, SEQ // TS, N_KV, HEAD_DIM, nb)
    km = km.transpose(0, 2, 1, 4, 3).reshape(BATCH, N_KV, N_CHUNK, HEAD_DIM).astype(BF16)
    oz = _moba_attn_call(qT, km, k_c, v_c, sz)
    return _out_call(oz, w_out.T.astype(BF16), xT, gate)


@jax.jit
def _forward(x, c, positions, norm_g, ada_w, ada_b, nsa_w_in, nsa_w_out, nsa_q_norm, nsa_k_norm,
             nsa_cmp_pe, nsa_cmp_w1, nsa_cmp_b1, nsa_cmp_w2, nsa_gate_b,
             moba_w_in, moba_w_out, moba_q_norm, moba_k_norm):
    cos, sin = _rope_tables(positions)
    cosT = cos.transpose(0, 2, 1)
    sinT = sin.transpose(0, 2, 1)
    cmp_end = np.minimum(np.arange(N_CMP_PAD) * CMP_STRIDE + CMP_LEN - 1, SEQ - 1)
    cos_c, sin_c = _rope_tables(positions[:, cmp_end])

    def cmp_table(t):
        t = jnp.broadcast_to(t.transpose(2, 0, 1)[:, :, None, :],
                             (ROT_HALF, BATCH, N_KV, N_CMP_PAD))
        return t.reshape(ROT_HALF, CMP_COLS)

    cosc = cmp_table(cos_c)
    sinc = cmp_table(sin_c)

    mod = _ada_call(c, ada_w, ada_b)
    shift, scale, gate = jnp.split(mod[..., None], 3, axis=2)

    xT = x.transpose(0, 2, 1)
    for i in range(DEPTH):
        j = i // 2
        ng = _col(norm_g[i])
        if i % 2 == 0:
            xT = _nsa_layer(xT, ng, scale[i], shift[i], gate[i], cosT, sinT, cosc, sinc,
                            nsa_w_in[j], nsa_w_out[j], nsa_q_norm[j], nsa_k_norm[j],
                            nsa_cmp_pe[j], nsa_cmp_w1[j], nsa_cmp_b1[j], nsa_cmp_w2[j],
                            nsa_gate_b[j])
        else:
            xT = _moba_layer(xT, ng, scale[i], shift[i], gate[i], cosT, sinT,
                             moba_w_in[j], moba_w_out[j], moba_q_norm[j], moba_k_norm[j])
    return xT.transpose(0, 2, 1)


def kernel(x, c, positions, norm_g, ada_w, ada_b, nsa_w_in, nsa_w_out, nsa_q_norm, nsa_k_norm, nsa_cmp_pe, nsa_cmp_w1, nsa_cmp_b1, nsa_cmp_w2, nsa_gate_b, moba_w_in, moba_w_out, moba_q_norm, moba_k_norm):
    return _forward(x, c, positions, norm_g, ada_w, ada_b, nsa_w_in, nsa_w_out, nsa_q_norm,
                    nsa_k_norm, nsa_cmp_pe, nsa_cmp_w1, nsa_cmp_b1, nsa_cmp_w2, nsa_gate_b,
                    moba_w_in, moba_w_out, moba_q_norm, moba_k_norm)
```

```python
import functools

import numpy as np
import jax
import jax.numpy as jnp
from jax import lax
from jax.experimental import pallas as pl
from jax.experimental.pallas import tpu as pltpu

D_MODEL = 1024
BATCH = 16
SEQ = 2048
DEPTH = 4
HEAD_DIM = 64
N_HEADS = 16
N_KV = 4
GQA = 4
ROT_HALF = 8
ROPE_THETA = 500000.0
NORM_EPS = 1e-6
CMP_LEN = 32
CMP_STRIDE = 16
CMP_HID = 256
SEL_BLOCK = 64
SEL_COUNT = 8
N_SEL = SEQ // SEL_BLOCK
N_CMP = (SEQ - CMP_LEN) // CMP_STRIDE + 1
N_CMP_PAD = 128
MOBA_TOPK = 3

TQ = 256
N_CHUNK = SEQ // TQ
LANES_Q = GQA * TQ
TS = 512
NEG = -1e30
BIAS_ROWS = 16
K_AUG = HEAD_DIM + BIAS_ROWS
Q_SCALE = HEAD_DIM ** -0.5 * float(np.log2(np.e))
GATE_ROWS = 16

F32 = jnp.float32
BF16 = jnp.bfloat16

VMEM_LIMIT = 52 * 1024 * 1024


def _cparams(sem):
    return pltpu.CompilerParams(dimension_semantics=sem, vmem_limit_bytes=VMEM_LIMIT)


def _dot(a, b):
    return jnp.dot(a, b, preferred_element_type=F32)


def _ada_kernel(c_ref, w_ref, b_ref, o_ref):
    cond = c_ref[...]
    cond = cond * jax.nn.sigmoid(cond)
    o_ref[0] = jnp.dot(cond, w_ref[0], precision=lax.Precision.HIGHEST,
                       preferred_element_type=F32) + b_ref[0]


def _ada_call(c, ada_w, ada_b):
    nt = 1024
    return pl.pallas_call(
        _ada_kernel,
        grid=(DEPTH, 3 * D_MODEL // nt),
        in_specs=[
            pl.BlockSpec((BATCH, D_MODEL), lambda i, n: (0, 0)),
            pl.BlockSpec((1, D_MODEL, nt), lambda i, n: (i, 0, n)),
            pl.BlockSpec((1, 1, nt), lambda i, n: (i, 0, n)),
        ],
        out_specs=pl.BlockSpec((1, BATCH, nt), lambda i, n: (i, 0, n)),
        out_shape=jax.ShapeDtypeStruct((DEPTH, BATCH, 3 * D_MODEL), F32),
        compiler_params=_cparams(("parallel", "parallel")),
        name="ada_mod",
    )(c, ada_w, ada_b.reshape(DEPTH, 1, 3 * D_MODEL))


def _norm_mod(x_ref, ng_ref, sc_ref, sh_ref):
    x = x_ref[0]
    ms = jnp.mean(x * x, axis=0, keepdims=True)
    y = x * lax.rsqrt(ms + NORM_EPS)
    h = (y * ng_ref[...]) * (1.0 + sc_ref[0]) + sh_ref[0]
    return h.astype(BF16)


def _head_norm_rope(y, g, cos, sin):
    ms = jnp.mean(y * y, axis=0, keepdims=True)
    yn = (y * lax.rsqrt(ms + NORM_EPS)) * g
    x1 = yn[0:ROT_HALF]
    x2 = yn[ROT_HALF:2 * ROT_HALF]
    return jnp.concatenate([x1 * cos - x2 * sin, x2 * cos + x1 * sin, yn[2 * ROT_HALF:]], axis=0)


def _store_k_chunks(o_ref, gi, yh, ind=None):
    for it in range(TS // TQ):
        chunk = yh[:, it * TQ:(it + 1) * TQ]
        if ind is None:
            o_ref[0, gi, it] = chunk.T.astype(BF16)
        else:
            aug = jnp.concatenate([chunk, ind], axis=0).T
            o_ref[0, gi, it] = aug[:, :K_AUG].astype(BF16)


def _block_indicator(blocks):
    row = lax.broadcasted_iota(jnp.int32, (HEAD_DIM, TQ), 0)
    lane = lax.broadcasted_iota(jnp.int32, (HEAD_DIM, TQ), 1)
    return (lane // (TQ // blocks) == row).astype(F32)


def _store_v_chunks(o_ref, y):
    yb = y.astype(BF16)
    for gi in range(N_KV):
        for it in range(TS // TQ):
            o_ref[0, gi, it] = yb[gi * HEAD_DIM:(gi + 1) * HEAD_DIM, it * TQ:(it + 1) * TQ]


NSA_OFF_Q = 0
NSA_OFF_KS = 1024
NSA_OFF_KW = 1280
NSA_OFF_KC = 1536
NSA_OFF_VC = 1792
NSA_OFF_VS = 2048
NSA_OFF_VW = 2304
NSA_OFF_GT = 2560
NSA_OFF_Z = NSA_OFF_GT + N_KV * GATE_ROWS
NSA_ROWS = NSA_OFF_Z + D_MODEL


def _nsa_in_kernel(x_ref, ng_ref, sc_ref, sh_ref, w_ref, qg_ref, ksg_ref, kwg_ref, gb_ref,
                   cos_ref, sin_ref,
                   q_ref, ks_ref, kw_ref, kc_ref, vc_ref, vs_ref, vw_ref, gt_ref, sz_ref):
    hb = _norm_mod(x_ref, ng_ref, sc_ref, sh_ref)
    cos = cos_ref[0]
    sin = sin_ref[0]

    def proj(r0, n):
        return _dot(w_ref[r0:r0 + n, :], hb)

    for gi in range(N_KV):
        y = proj(NSA_OFF_Q + gi * 256, 256)
        for r in range(GQA):
            yh = _head_norm_rope(y[r * HEAD_DIM:(r + 1) * HEAD_DIM], qg_ref[...], cos, sin)
            row = gi * 256 + r * HEAD_DIM
            q_ref[0, row:row + HEAD_DIM, :] = (yh * Q_SCALE).astype(BF16)
    sel_ind = _block_indicator(TQ // SEL_BLOCK)
    for off, g_ref, o_ref, ind in ((NSA_OFF_KS, ksg_ref, ks_ref, sel_ind),
                                   (NSA_OFF_KW, kwg_ref, kw_ref, None)):
        y = proj(off, 256)
        for gi in range(N_KV):
            yh = _head_norm_rope(y[gi * HEAD_DIM:(gi + 1) * HEAD_DIM], g_ref[...], cos, sin)
            _store_k_chunks(o_ref, gi, yh, ind)
    for off, o_ref in ((NSA_OFF_KC, kc_ref), (NSA_OFF_VC, vc_ref)):
        y = proj(off, 256)
        for gi in range(N_KV):
            for it in range(TS // TQ):
                o_ref[0, gi, it] = y[gi * HEAD_DIM:(gi + 1) * HEAD_DIM, it * TQ:(it + 1) * TQ].T
    _store_v_chunks(vs_ref, proj(NSA_OFF_VS, 256))
    _store_v_chunks(vw_ref, proj(NSA_OFF_VW, 256))
    gt_ref[0] = jax.nn.sigmoid(proj(NSA_OFF_GT, N_KV * GATE_ROWS) + gb_ref[...])
    for cblk in range(D_MODEL // 256):
        z = proj(NSA_OFF_Z + cblk * 256, 256)
        sz_ref[0, cblk * 256:(cblk + 1) * 256, :] = z * jax.nn.sigmoid(z)


def _col_spec(rows):
    return pl.BlockSpec((rows, 1), lambda b, s: (0, 0))


def _bcol_spec(rows):
    return pl.BlockSpec((1, rows, 1), lambda b, s: (b, 0, 0))


def _fm_spec(rows):
    return pl.BlockSpec((1, rows, TS), lambda b, s: (b, 0, s))


def _k_chunk_spec(width):
    return pl.BlockSpec((1, N_KV, TS // TQ, TQ, width), lambda b, s: (b, 0, s, 0, 0))


def _k_chunk_shape(width, dtype=BF16):
    return jax.ShapeDtypeStruct((BATCH, N_KV, N_CHUNK, TQ, width), dtype)


_V_CHUNK_SPEC = pl.BlockSpec((1, N_KV, TS // TQ, HEAD_DIM, TQ), lambda b, s: (b, 0, s, 0, 0))
_V_CHUNK_SHAPE = jax.ShapeDtypeStruct((BATCH, N_KV, N_CHUNK, HEAD_DIM, TQ), BF16)


def _nsa_in_call(xT, ng, sc, sh, wT, qg, ksg, kwg, gb, cosT, sinT):
    fm = lambda rows, dt: jax.ShapeDtypeStruct((BATCH, rows, SEQ), dt)
    return pl.pallas_call(
        _nsa_in_kernel,
        grid=(BATCH, SEQ // TS),
        in_specs=[
            _fm_spec(D_MODEL), _col_spec(D_MODEL), _bcol_spec(D_MODEL), _bcol_spec(D_MODEL),
            pl.BlockSpec((NSA_ROWS, D_MODEL), lambda b, s: (0, 0)),
            _col_spec(HEAD_DIM), _col_spec(HEAD_DIM), _col_spec(HEAD_DIM),
            _col_spec(N_KV * GATE_ROWS),
            _fm_spec(ROT_HALF), _fm_spec(ROT_HALF),
        ],
        out_specs=[
            _fm_spec(D_MODEL), _k_chunk_spec(K_AUG), _k_chunk_spec(HEAD_DIM),
            _k_chunk_spec(HEAD_DIM), _k_chunk_spec(HEAD_DIM),
            _V_CHUNK_SPEC, _V_CHUNK_SPEC, _fm_spec(N_KV * GATE_ROWS), _fm_spec(D_MODEL),
        ],
        out_shape=[
            fm(D_MODEL, BF16), _k_chunk_shape(K_AUG), _k_chunk_shape(HEAD_DIM),
            _k_chunk_shape(HEAD_DIM, F32), _k_chunk_shape(HEAD_DIM, F32),
            _V_CHUNK_SHAPE, _V_CHUNK_SHAPE, fm(N_KV * GATE_ROWS, F32), fm(D_MODEL, F32),
        ],
        compiler_params=_cparams(("parallel", "parallel")),
        name="nsa_in_proj",
    )(xT, ng, sc, sh, wT, qg, ksg, kwg, gb, cosT, sinT)


MOBA_OFF_Q = 0
MOBA_OFF_K = 1024
MOBA_OFF_V = 1280
MOBA_OFF_Z = 1536
MOBA_ROWS = 2560
KM_LANES = 128


def _moba_in_kernel(x_ref, ng_ref, sc_ref, sh_ref, w_ref, qg_ref, kg_ref, cos_ref, sin_ref,
                    q_ref, k_ref, km_ref, v_ref, sz_ref):
    hb = _norm_mod(x_ref, ng_ref, sc_ref, sh_ref)
    cos = cos_ref[0]
    sin = sin_ref[0]

    def proj(r0, n):
        return _dot(w_ref[r0:r0 + n, :], hb)

    for gi in range(N_KV):
        y = proj(MOBA_OFF_Q + gi * 256, 256)
        for r in range(GQA):
            yh = _head_norm_rope(y[r * HEAD_DIM:(r + 1) * HEAD_DIM], qg_ref[...], cos, sin)
            row = gi * 256 + r * HEAD_DIM
            q_ref[0, row:row + HEAD_DIM, :] = (yh * Q_SCALE).astype(BF16)
    y = proj(MOBA_OFF_K, 256)
    lane = lax.broadcasted_iota(jnp.int32, (HEAD_DIM, KM_LANES), 1)
    for gi in range(N_KV):
        yh = _head_norm_rope(y[gi * HEAD_DIM:(gi + 1) * HEAD_DIM], kg_ref[...], cos, sin)
        _store_k_chunks(k_ref, gi, yh, _block_indicator(1))
        km = jnp.zeros((HEAD_DIM, KM_LANES), F32)
        for it in range(TS // TQ):
            mean = jnp.mean(yh[:, it * TQ:(it + 1) * TQ], axis=1, keepdims=True)
            km = jnp.where(lane == it, mean, km)
        km_ref[0, 0, gi * HEAD_DIM:(gi + 1) * HEAD_DIM, :] = km
    _store_v_chunks(v_ref, proj(MOBA_OFF_V, 256))
    for cblk in range(D_MODEL // 256):
        z = proj(MOBA_OFF_Z + cblk * 256, 256)
        sz_ref[0, cblk * 256:(cblk + 1) * 256, :] = z * jax.nn.sigmoid(z)


def _moba_in_call(xT, ng, sc, sh, wT, qg, kg, cosT, sinT):
    fm = lambda rows, dt: jax.ShapeDtypeStruct((BATCH, rows, SEQ), dt)
    return pl.pallas_call(
        _moba_in_kernel,
        grid=(BATCH, SEQ // TS),
        in_specs=[
            _fm_spec(D_MODEL), _col_spec(D_MODEL), _bcol_spec(D_MODEL), _bcol_spec(D_MODEL),
            pl.BlockSpec((MOBA_ROWS, D_MODEL), lambda b, s: (0, 0)),
            _col_spec(HEAD_DIM), _col_spec(HEAD_DIM),
            _fm_spec(ROT_HALF), _fm_spec(ROT_HALF),
        ],
        out_specs=[
            _fm_spec(D_MODEL), _k_chunk_spec(K_AUG),
            pl.BlockSpec((1, 1, N_KV * HEAD_DIM, KM_LANES), lambda b, s: (b, s, 0, 0)),
            _V_CHUNK_SPEC, _fm_spec(D_MODEL),
        ],
        out_shape=[
            fm(D_MODEL, BF16), _k_chunk_shape(K_AUG),
            jax.ShapeDtypeStruct((BATCH, SEQ // TS, N_KV * HEAD_DIM, KM_LANES), F32),
            _V_CHUNK_SHAPE, fm(D_MODEL, F32),
        ],
        compiler_params=_cparams(("parallel", "parallel")),
        name="moba_in_proj",
    )(xT, ng, sc, sh, wT, qg, kg, cosT, sinT)


CMP_NB = 8
CMP_ROWS = CMP_NB * N_CMP_PAD
CMP_COLS = BATCH * N_KV * N_CMP_PAD
CMP_FEAT = CMP_STRIDE * HEAD_DIM


def _compress_mlp(x_ref, pea_ref, peb_ref, w1a_ref, w1b_ref, b1_ref, w2t_ref):
    x = x_ref[...].reshape(CMP_ROWS, CMP_FEAT)
    u = _dot((x + pea_ref[...]).astype(BF16), w1a_ref[...])
    v = _dot((x + peb_ref[...]).astype(BF16), w1b_ref[...])
    v = pltpu.roll(v, CMP_ROWS - 1, 0)
    h = u + v + b1_ref[...]
    h = 0.5 * h * (1.0 + jnp.tanh(np.sqrt(2.0 / np.pi) * (h + 0.044715 * (h * h * h))))
    return lax.dot_general(w2t_ref[...], h.astype(BF16), (((1,), (1,)), ((), ())),
                           preferred_element_type=F32)


def _compress_key_kernel(x_ref, pea_ref, peb_ref, w1a_ref, w1b_ref, b1_ref, w2t_ref, kg_ref,
                         cos_ref, sin_ref, o_ref):
    out = _compress_mlp(x_ref, pea_ref, peb_ref, w1a_ref, w1b_ref, b1_ref, w2t_ref)
    out = _head_norm_rope(out, kg_ref[...], cos_ref[...], sin_ref[...])
    for i in range(CMP_NB):
        o_ref[i] = out[:, i * N_CMP_PAD:(i + 1) * N_CMP_PAD].T.astype(BF16)


def _compress_value_kernel(x_ref, pea_ref, peb_ref, w1a_ref, w1b_ref, b1_ref, w2t_ref, o_ref):
    out = _compress_mlp(x_ref, pea_ref, peb_ref, w1a_ref, w1b_ref, b1_ref, w2t_ref)
    for i in range(CMP_NB):
        o_ref[i] = out[:, i * N_CMP_PAD:(i + 1) * N_CMP_PAD].astype(BF16)


def _compress_call(x, pe, w1, b1, w2, key_extras=None):
    full = lambda shape: pl.BlockSpec(shape, lambda t: (0,) * len(shape))
    in_specs = [
        pl.BlockSpec((CMP_NB, N_CMP_PAD, CMP_FEAT), lambda t: (t, 0, 0)),
        full((1, CMP_FEAT)), full((1, CMP_FEAT)),
        full((CMP_FEAT, CMP_HID)), full((CMP_FEAT, CMP_HID)),
        full((1, CMP_HID)), full((HEAD_DIM, CMP_HID)),
    ]
    pe = pe.reshape(2, 1, CMP_FEAT)
    w1 = w1.astype(BF16)
    args = [x, pe[0], pe[1], w1[:CMP_FEAT], w1[CMP_FEAT:], b1.reshape(1, CMP_HID), w2.T.astype(BF16)]
    if key_extras is not None:
        in_specs += [full((HEAD_DIM, 1)),
                     pl.BlockSpec((ROT_HALF, CMP_ROWS), lambda t: (0, t)),
                     pl.BlockSpec((ROT_HALF, CMP_ROWS), lambda t: (0, t))]
        args += list(key_extras)
        kernel, out_tail = _compress_key_kernel, (N_CMP_PAD, HEAD_DIM)
    else:
        kernel, out_tail = _compress_value_kernel, (HEAD_DIM, N_CMP_PAD)
    return pl.pallas_call(
        kernel,
        grid=(BATCH * N_KV // CMP_NB,),
        in_specs=in_specs,
        out_specs=pl.BlockSpec((CMP_NB,) + out_tail, lambda t: (t, 0, 0)),
        out_shape=jax.ShapeDtypeStruct((BATCH * N_KV,) + out_tail, BF16),
        compiler_params=_cparams(("parallel",)),
        name="nsa_compress_key" if key_extras is not None else "nsa_compress_value",
    )(*args)


def _group_queries(q_ref):
    return jnp.concatenate(
        [q_ref[0, r * HEAD_DIM:(r + 1) * HEAD_DIM, :] for r in range(GQA)], axis=1)


COL = 128


def _col_tiles():
    per_head = TQ // COL
    for ct in range(LANES_Q // COL):
        r, h = divmod(ct, per_head)
        yield r, slice(h * COL, (h + 1) * COL), slice(ct * COL, (ct + 1) * COL)


def _flash_scratch():
    return [
        pltpu.VMEM((2, TQ, LANES_Q), BF16),
        pltpu.VMEM((2, 1, LANES_Q), F32),
        pltpu.VMEM((1, LANES_Q), F32),
        pltpu.VMEM((1, LANES_Q), F32),
        pltpu.VMEM((HEAD_DIM, LANES_Q), F32),
    ]


class _FlashBranch:
    def __init__(self, q_ref, k_ref, v_ref, bufs, own, past_chunk, own_fix=None, past_fix=None,
                 own_rows=None, past_rows=None):
        self.q_ref, self.k_ref, self.v_ref, self.bufs = q_ref, k_ref, v_ref, bufs
        self.own_chunk, self.past_chunk = own, past_chunk
        self.own_fix, self.past_fix, self.own_rows, self.past_rows = own_fix, past_fix, own_rows, past_rows

    def _softmax_tile(self, s, cs, slot, init):
        p_buf, a_buf, m_ref, l_ref, _ = self.bufs
        m_loc = jnp.max(s, axis=0, keepdims=True)
        if init:
            m_new = m_loc
            a_buf[slot, :, cs] = jnp.ones((1, COL), F32)
        else:
            m_old = m_ref[:, cs]
            m_new = jnp.maximum(m_old, m_loc)
            alpha = jnp.exp2(m_old - m_new)
            a_buf[slot, :, cs] = alpha
        p = jnp.exp2(s - m_new)
        p_sum = jnp.sum(p, axis=0, keepdims=True)
        l_ref[:, cs] = p_sum if init else alpha * l_ref[:, cs] + p_sum
        m_ref[:, cs] = m_new
        p_buf[slot, :, cs] = p.astype(BF16)

    def _scores(self, k, r, qs, cs, rows, fix):
        q = self.q_ref[0, r * HEAD_DIM:(r + 1) * HEAD_DIM, qs]
        if rows is not None:
            q = jnp.concatenate([q, rows(cs)], axis=0)
        s = _dot(k, q)
        return s if fix is None else fix(s, qs)

    def _pv_tile(self, vT, slot, cs):
        p_buf, a_buf, _, _, acc_ref = self.bufs
        acc_ref[:, cs] = a_buf[slot, :, cs] * acc_ref[:, cs] + _dot(vT, p_buf[slot, :, cs])

    def own(self):
        acc_ref = self.bufs[4]
        acc_ref[...] = jnp.zeros(acc_ref.shape, F32)
        k_own = self.k_ref[0, 0, self.own_chunk]
        for r, qs, cs in _col_tiles():
            self._softmax_tile(self._scores(k_own, r, qs, cs, self.own_rows, self.own_fix), cs, 0, True)

    def trip(self, j, cur, prv):
        k = self.k_ref[0, 0, self.past_chunk(j)]
        prev = self.own_chunk if isinstance(j, int) and j == 0 else jnp.where(
            j == 0, self.own_chunk, self.past_chunk(jnp.maximum(j - 1, 0)))
        vT = self.v_ref[0, 0, prev]
        rows = None if self.past_rows is None else self.past_rows(j)
        for r, qs, cs in _col_tiles():
            s = self._scores(k, r, qs, cs, rows, self.past_fix)
            self._pv_tile(vT, prv, cs)
            self._softmax_tile(s, cs, cur, False)

    def run_past(self, n_past):
        def pair(pp, carry):
            self.trip(2 * pp, 1, 0)
            self.trip(2 * pp + 1, 0, 1)
            return carry

        lax.fori_loop(0, n_past // 2, pair, 0)

        @pl.when(n_past % 2 == 1)
        def _():
            self.trip(n_past - 1, 1, 0)

    def finish(self, n_past):
        last = jnp.where(n_past == 0, self.own_chunk, self.past_chunk(jnp.maximum(n_past - 1, 0)))
        vT = self.v_ref[0, 0, last]
        for _, _, cs in _col_tiles():
            self._pv_tile(vT, n_past % 2, cs)
        return self.bufs[4][...] / self.bufs[3][...]


def _flash_branch(q_ref, k_ref, v_ref, bufs, own, n_past, past_chunk, **kw):
    branch = _FlashBranch(q_ref, k_ref, v_ref, bufs, own, past_chunk, **kw)
    branch.own()
    branch.run_past(n_past)
    return branch.finish(n_past)


def _causal_fix(keep_lower):
    a_idx = lax.broadcasted_iota(jnp.int32, (TQ, COL), 0)
    lane = lax.broadcasted_iota(jnp.int32, (TQ, COL), 1)

    def fix(s, qs):
        lower = a_idx <= lane + qs.start
        return jnp.where(lower if keep_lower else ~lower, s, NEG)

    return fix


def _rank_select(score_ref, n_rows, j_idx, count):
    score = score_ref[...]
    cnt = jnp.zeros(score.shape, jnp.int32)
    for jp in range(n_rows):
        row = score_ref[jp:jp + 1, :]
        beats = (row > score) | ((row == score) & (jp < j_idx))
        cnt = cnt + beats.astype(jnp.int32)
    return cnt < count


def _store_out(o_ref, sz_ref, o):
    for r in range(GQA):
        rows = slice(r * HEAD_DIM, (r + 1) * HEAD_DIM)
        o_ref[0, rows, :] = (o[:, r * TQ:(r + 1) * TQ] * sz_ref[0, rows, :]).astype(BF16)


def _nsa_attn_kernel(q_ref, kc_ref, vc_ref, ks_ref, vs_ref, kw_ref, vw_ref, gt_ref, sz_ref,
                     selw_ref, o_ref, score_ref, bias_ref, *bufs):
    qi = pl.program_id(2)
    q4 = _group_queries(q_ref)

    sel_bufs, win_bufs = bufs[:len(bufs) // 2], bufs[len(bufs) // 2:]
    keep_lower = _causal_fix(True)
    a_idx = lax.broadcasted_iota(jnp.int32, (TQ, COL), 0)
    lane = lax.broadcasted_iota(jnp.int32, (TQ, COL), 1)
    no_prev = jnp.where(qi > 0, 0, TQ)

    def keep_upper(s, qs):
        return jnp.where(a_idx > lane + (qs.start + no_prev), s, NEG)

    win = _FlashBranch(q_ref, kw_ref, vw_ref, win_bufs, qi, lambda j: jnp.maximum(qi - 1, 0),
                       own_fix=keep_lower, past_fix=keep_upper)

    t = qi * TQ + (lax.broadcasted_iota(jnp.int32, (1, LANES_Q), 1) & (TQ - 1))
    n_idx = lax.broadcasted_iota(jnp.int32, (N_CMP_PAD, LANES_Q), 0)
    s_cmp = _dot(kc_ref[0, 0], q4)
    win.own()
    s = jnp.where(n_idx * CMP_STRIDE + (CMP_LEN - 1) <= t, s_cmp, NEG)
    m = jnp.max(s, axis=0, keepdims=True)
    p = jnp.exp2(s - m) * (m > 0.5 * NEG).astype(F32)
    p = p / jnp.maximum(jnp.sum(p, axis=0, keepdims=True), 1e-30)
    o_cmp = _dot(vc_ref[0, 0], p.astype(BF16))

    psum = p[:, 0:TQ]
    for r in range(1, GQA):
        psum = psum + p[:, r * TQ:(r + 1) * TQ]
    p_hi = psum.astype(BF16)
    p_lo = (psum - p_hi.astype(F32)).astype(BF16)
    imp = _dot(selw_ref[...], p_hi) + _dot(selw_ref[...], p_lo)
    j_idx = lax.broadcasted_iota(jnp.int32, (N_SEL, TQ), 0)
    cur = qi * (TQ // SEL_BLOCK) + (lax.broadcasted_iota(jnp.int32, (N_SEL, TQ), 1) >> 6)
    valid = j_idx <= cur
    forced = (j_idx == 0) | (j_idx == cur) | (j_idx == cur - 1)
    score_ref[...] = jnp.where(forced, jnp.inf, jnp.where(valid, imp, -jnp.inf))
    win.trip(0, 1, 0)
    sel = valid & _rank_select(score_ref, N_SEL, j_idx, SEL_COUNT)
    bias = jnp.where(sel, 0.0, NEG)
    bias = jnp.concatenate([bias] * GQA, axis=1)
    per_chunk = TQ // SEL_BLOCK
    bias_ref[...] = jnp.zeros(bias_ref.shape, F32)
    for jp in range(N_SEL):
        bias_ref[jp // per_chunk, jp % per_chunk:jp % per_chunk + 1, :] = bias[jp:jp + 1, :]

    o_win = win.finish(1)

    def sel_rows(j):
        return lambda cs: bias_ref[j, :, cs].astype(BF16)

    o_slc = _flash_branch(q_ref, ks_ref, vs_ref, sel_bufs, qi, qi, lambda j: j, own_fix=keep_lower,
                          own_rows=sel_rows(qi), past_rows=sel_rows)

    def gate(br):
        return jnp.concatenate(
            [gt_ref[0, br * GQA + r:br * GQA + r + 1, :] for r in range(GQA)], axis=1)

    o = gate(0) * o_cmp + gate(1) * o_slc + gate(2) * o_win
    _store_out(o_ref, sz_ref, o)


def _attn_specs():
    q_spec = pl.BlockSpec((1, GQA * HEAD_DIM, TQ), lambda b, g, i: (b, g, i))
    k_spec = lambda width: pl.BlockSpec((1, 1, N_CHUNK, TQ, width), lambda b, g, i: (b, g, 0, 0, 0))
    v_spec = pl.BlockSpec((1, 1, N_CHUNK, HEAD_DIM, TQ), lambda b, g, i: (b, g, 0, 0, 0))
    return q_spec, k_spec, v_spec


def _nsa_attn_call(q, kc, vc, ks, vs, kw, vw, gt, sz, selw):
    q_spec, k_spec, v_spec = _attn_specs()
    return pl.pallas_call(
        _nsa_attn_kernel,
        grid=(BATCH, N_KV, N_CHUNK),
        in_specs=[
            q_spec,
            pl.BlockSpec((1, 1, N_CMP_PAD, HEAD_DIM), lambda b, g, i: (b, g, 0, 0)),
            pl.BlockSpec((1, 1, HEAD_DIM, N_CMP_PAD), lambda b, g, i: (b, g, 0, 0)),
            k_spec(K_AUG), v_spec, k_spec(HEAD_DIM), v_spec,
            pl.BlockSpec((1, GATE_ROWS, TQ), lambda b, g, i: (b, g, i)),
            q_spec,
            pl.BlockSpec((N_SEL, N_CMP_PAD), lambda b, g, i: (0, 0)),
        ],
        out_specs=q_spec,
        out_shape=jax.ShapeDtypeStruct((BATCH, D_MODEL, SEQ), BF16),
        scratch_shapes=[
            pltpu.VMEM((N_SEL, TQ), F32),
            pltpu.VMEM((N_CHUNK, BIAS_ROWS, LANES_Q), F32),
        ] + _flash_scratch() + _flash_scratch(),
        compiler_params=_cparams(("parallel", "parallel", "arbitrary")),
        name="nsa_attention",
    )(q, kc, vc, ks, vs, kw, vw, gt, sz, selw)


def _moba_attn_kernel(q_ref, km_ref, k_ref, v_ref, sz_ref, o_ref,
                      score_ref, bias_ref, *bufs):
    qi = pl.program_id(2)
    q4 = _group_queries(q_ref)

    j_idx = lax.broadcasted_iota(jnp.int32, (N_CHUNK, LANES_Q), 0)
    past = j_idx < qi
    score_ref[...] = jnp.where(past, _dot(km_ref[0, 0], q4), -jnp.inf)
    sel = past & _rank_select(score_ref, N_CHUNK, j_idx, MOBA_TOPK)
    bias = jnp.where(sel, 0.0, NEG)
    bias_ref[...] = jnp.zeros(bias_ref.shape, F32)
    for jp in range(N_CHUNK):
        bias_ref[jp, 0:1, :] = bias[jp:jp + 1, :]

    o = _flash_branch(q_ref, k_ref, v_ref, bufs, qi, qi, lambda j: j, own_fix=_causal_fix(True),
                      own_rows=lambda cs: jnp.zeros((BIAS_ROWS, COL), BF16),
                      past_rows=lambda j: (lambda cs: bias_ref[j, :, cs].astype(BF16)))
    _store_out(o_ref, sz_ref, o)


def _moba_attn_call(q, km, k, v, sz):
    q_spec, k_spec, v_spec = _attn_specs()
    return pl.pallas_call(
        _moba_attn_kernel,
        grid=(BATCH, N_KV, N_CHUNK),
        in_specs=[
            q_spec,
            pl.BlockSpec((1, 1, N_CHUNK, HEAD_DIM), lambda b, g, i: (b, g, 0, 0)),
            k_spec(K_AUG), v_spec, q_spec,
        ],
        out_specs=q_spec,
        out_shape=jax.ShapeDtypeStruct((BATCH, D_MODEL, SEQ), BF16),
        scratch_shapes=[
            pltpu.VMEM((N_CHUNK, LANES_Q), F32),
            pltpu.VMEM((N_CHUNK, BIAS_ROWS, LANES_Q), F32),
        ] + _flash_scratch(),
        compiler_params=_cparams(("parallel", "parallel", "arbitrary")),
        name="moba_attention",
    )(q, km, k, v, sz)


def _out_kernel(oz_ref, w_ref, x_ref, gt_ref, o_ref):
    oz = oz_ref[0]
    for cblk in range(D_MODEL // 256):
        rows = slice(cblk * 256, (cblk + 1) * 256)
        y = _dot(w_ref[rows, :], oz)
        o_ref[0, rows, :] = x_ref[0, rows, :] + gt_ref[0, rows, :] * y


def _out_call(oz, wT, xT, gate):
    return pl.pallas_call(
        _out_kernel,
        grid=(BATCH, SEQ // TS),
        in_specs=[
            _fm_spec(D_MODEL),
            pl.BlockSpec((D_MODEL, D_MODEL), lambda b, s: (0, 0)),
            _fm_spec(D_MODEL), _bcol_spec(D_MODEL),
        ],
        out_specs=_fm_spec(D_MODEL),
        out_shape=jax.ShapeDtypeStruct((BATCH, D_MODEL, SEQ), F32),
        compiler_params=_cparams(("parallel", "parallel")),
        name="out_proj",
    )(oz, wT, xT, gate)


def _rope_tables(pos):
    inv_freq = ROPE_THETA ** (-jnp.arange(0, 2 * ROT_HALF, 2, dtype=F32) / (2 * ROT_HALF))
    ang = pos.astype(F32)[..., None] * inv_freq
    return jnp.cos(ang), jnp.sin(ang)


def _gate_perm():
    perm = np.full((N_KV * GATE_ROWS,), 3 * N_HEADS, dtype=np.int32)
    for g in range(N_KV):
        for br in range(3):
            for r in range(GQA):
                perm[g * GATE_ROWS + br * GQA + r] = (g * GQA + r) * 3 + br
    return perm


def _sel_weights_T():
    cs = np.arange(N_CMP)[:, None] * CMP_STRIDE
    ss = np.arange(N_SEL)[None, :] * SEL_BLOCK
    shared = np.clip(np.minimum(cs + CMP_LEN, ss + SEL_BLOCK) - np.maximum(cs, ss), 0, None)
    w = np.zeros((N_CMP_PAD, N_SEL), np.float32)
    w[:N_CMP] = shared / CMP_LEN
    return jnp.asarray(w.T, dtype=BF16)


def _col(v):
    return v.reshape(-1, 1)


def _nsa_layer(xT, ng, sc, sh, gate, cosT, sinT, cosc, sinc, w_in, w_out, q_g, k_g,
               cmp_pe, cmp_w1, cmp_b1, cmp_w2, gate_b):
    widths = [1024] + [256] * 6 + [3 * N_HEADS, 1024]
    q, kc, vc, ks, vs, kw, vw, gl, z = jnp.split(w_in, np.cumsum(widths)[:-1].tolist(), axis=1)
    perm = _gate_perm()
    gl_p = jnp.concatenate([gl, jnp.zeros((D_MODEL, 1), F32)], axis=1)[:, perm]
    gb_p = jnp.concatenate([gate_b, jnp.zeros((1,), F32)])[perm]
    wT = jnp.concatenate([q, ks, kw, kc, vc, vs, vw, gl_p, z], axis=1).T.astype(BF16)

    qT, ks_c, kw_c, kc_c, vc_c, vs_c, vw_c, gates, sz = _nsa_in_call(
        xT, ng, sc, sh, wT, _col(q_g), _col(k_g[1]), _col(k_g[2]), _col(gb_p), cosT, sinT)

    half_blocks = lambda t: t.reshape(BATCH * N_KV, N_CMP_PAD, CMP_FEAT)
    k_cmp = _compress_call(half_blocks(kc_c), cmp_pe[0], cmp_w1[0], cmp_b1[0], cmp_w2[0],
                           key_extras=(_col(k_g[0]), cosc, sinc))
    v_cmpT = _compress_call(half_blocks(vc_c), cmp_pe[1], cmp_w1[1], cmp_b1[1], cmp_w2[1])
    k_cmp = k_cmp.reshape(BATCH, N_KV, N_CMP_PAD, HEAD_DIM)
    v_cmpT = v_cmpT.reshape(BATCH, N_KV, HEAD_DIM, N_CMP_PAD)

    oz = _nsa_attn_call(qT, k_cmp, v_cmpT, ks_c, vs_c, kw_c, vw_c, gates, sz, _sel_weights_T())
    return _out_call(oz, w_out.T.astype(BF16), xT, gate)


def _moba_layer(xT, ng, sc, sh, gate, cosT, sinT, w_in, w_out, q_g, k_g):
    wT = w_in.T.astype(BF16)
    qT, k_c, km, v_c, sz = _moba_in_call(xT, ng, sc, sh, wT, _col(q_g), _col(k_g), cosT, sinT)
    nb = TS // TQ
    km = km[..., :nb].reshape(BATCH, SEQ // TS, N_KV, HEAD_DIM, nb)
    km = km.transpose(0, 2, 1, 4, 3).reshape(BATCH, N_KV, N_CHUNK, HEAD_DIM).astype(BF16)
    oz = _moba_attn_call(qT, km, k_c, v_c, sz)
    return _out_call(oz, w_out.T.astype(BF16), xT, gate)


@jax.jit
def _forward(x, c, positions, norm_g, ada_w, ada_b, nsa_w_in, nsa_w_out, nsa_q_norm, nsa_k_norm,
             nsa_cmp_pe, nsa_cmp_w1, nsa_cmp_b1, nsa_cmp_w2, nsa_gate_b,
             moba_w_in, moba_w_out, moba_q_norm, moba_k_norm):
    cos, sin = _rope_tables(positions)
    cosT = cos.transpose(0, 2, 1)
    sinT = sin.transpose(0, 2, 1)
    cmp_end = np.minimum(np.arange(N_CMP_PAD) * CMP_STRIDE + CMP_LEN - 1, SEQ - 1)
    cos_c, sin_c = _rope_tables(positions[:, cmp_end])

    def cmp_table(t):
        t = jnp.broadcast_to(t.transpose(2, 0, 1)[:, :, None, :],
                             (ROT_HALF, BATCH, N_KV, N_CMP_PAD))
        return t.reshape(ROT_HALF, CMP_COLS)

    cosc = cmp_table(cos_c)
    sinc = cmp_table(sin_c)

    mod = _ada_call(c, ada_w, ada_b)
    shift, scale, gate = jnp.split(mod[..., None], 3, axis=2)

    xT = x.transpose(0, 2, 1)
    for i in range(DEPTH):
        j = i // 2
        ng = _col(norm_g[i])
        if i % 2 == 0:
            xT = _nsa_layer(xT, ng, scale[i], shift[i], gate[i], cosT, sinT, cosc, sinc,
                            nsa_w_in[j], nsa_w_out[j], nsa_q_norm[j], nsa_k_norm[j],
                            nsa_cmp_pe[j], nsa_cmp_w1[j], nsa_cmp_b1[j], nsa_cmp_w2[j],
                            nsa_gate_b[j])
        else:
            xT = _moba_layer(xT, ng, scale[i], shift[i], gate[i], cosT, sinT,
                             moba_w_in[j], moba_w_out[j], moba_q_norm[j], moba_k_norm[j])
    return xT.transpose(0, 2, 1)


def kernel(x, c, positions, norm_g, ada_w, ada_b, nsa_w_in, nsa_w_out, nsa_q_norm, nsa_k_norm, nsa_cmp_pe, nsa_cmp_w1, nsa_cmp_b1, nsa_cmp_w2, nsa_gate_b, moba_w_in, moba_w_out, moba_q_norm, moba_k_norm):
    return _forward(x, c, positions, norm_g, ada_w, ada_b, nsa_w_in, nsa_w_out, nsa_q_norm,
                    nsa_k_norm, nsa_cmp_pe, nsa_cmp_w1, nsa_cmp_b1, nsa_cmp_w2, nsa_gate_b,
                    moba_w_in, moba_w_out, moba_q_norm, moba_k_norm)
```

```python
import functools

import numpy as np
import jax
import jax.numpy as jnp
from jax import lax
from jax.experimental import pallas as pl
from jax.experimental.pallas import tpu as pltpu

D_MODEL = 1024
BATCH = 16
SEQ = 2048
DEPTH = 4
HEAD_DIM = 64
N_HEADS = 16
N_KV = 4
GQA = 4
ROT_HALF = 8
ROPE_THETA = 500000.0
NORM_EPS = 1e-6
CMP_LEN = 32
CMP_STRIDE = 16
CMP_HID = 256
SEL_BLOCK = 64
SEL_COUNT = 8
N_SEL = SEQ // SEL_BLOCK
N_CMP = (SEQ - CMP_LEN) // CMP_STRIDE + 1
N_CMP_PAD = 128
MOBA_TOPK = 3

TQ = 256
N_CHUNK = SEQ // TQ
LANES_Q = GQA * TQ
TS = 512
NEG = -1e30
BIAS_ROWS = 16
K_AUG = HEAD_DIM + BIAS_ROWS
Q_SCALE = HEAD_DIM ** -0.5 * float(np.log2(np.e))
GATE_ROWS = 16

F32 = jnp.float32
BF16 = jnp.bfloat16

VMEM_LIMIT = 52 * 1024 * 1024


def _cparams(sem):
    return pltpu.CompilerParams(dimension_semantics=sem, vmem_limit_bytes=VMEM_LIMIT)


def _dot(a, b):
    return jnp.dot(a, b, preferred_element_type=F32)


def _ada_kernel(c_ref, w_ref, b_ref, o_ref):
    cond = c_ref[...]
    cond = cond * jax.nn.sigmoid(cond)
    o_ref[0] = jnp.dot(cond, w_ref[0], precision=lax.Precision.HIGHEST,
                       preferred_element_type=F32) + b_ref[0]


def _ada_call(c, ada_w, ada_b):
    nt = 1024
    return pl.pallas_call(
        _ada_kernel,
        grid=(DEPTH, 3 * D_MODEL // nt),
        in_specs=[
            pl.BlockSpec((BATCH, D_MODEL), lambda i, n: (0, 0)),
            pl.BlockSpec((1, D_MODEL, nt), lambda i, n: (i, 0, n)),
            pl.BlockSpec((1, 1, nt), lambda i, n: (i, 0, n)),
        ],
        out_specs=pl.BlockSpec((1, BATCH, nt), lambda i, n: (i, 0, n)),
        out_shape=jax.ShapeDtypeStruct((DEPTH, BATCH, 3 * D_MODEL), F32),
        compiler_params=_cparams(("parallel", "parallel")),
        name="ada_mod",
    )(c, ada_w, ada_b.reshape(DEPTH, 1, 3 * D_MODEL))


def _norm_mod(x_ref, ng_ref, sc_ref, sh_ref):
    x = x_ref[0]
    ms = jnp.mean(x * x, axis=0, keepdims=True)
    y = x * lax.rsqrt(ms + NORM_EPS)
    h = (y * ng_ref[...]) * (1.0 + sc_ref[0]) + sh_ref[0]
    return h.astype(BF16)


def _head_norm_rope(y, g, cos, sin):
    ms = jnp.mean(y * y, axis=0, keepdims=True)
    yn = (y * lax.rsqrt(ms + NORM_EPS)) * g
    x1 = yn[0:ROT_HALF]
    x2 = yn[ROT_HALF:2 * ROT_HALF]
    return jnp.concatenate([x1 * cos - x2 * sin, x2 * cos + x1 * sin, yn[2 * ROT_HALF:]], axis=0)


def _store_k_chunks(o_ref, gi, yh, ind=None):
    for it in range(TS // TQ):
        chunk = yh[:, it * TQ:(it + 1) * TQ]
        if ind is None:
            o_ref[0, gi, it] = chunk.T.astype(BF16)
        else:
            aug = jnp.concatenate([chunk, ind], axis=0).T
            o_ref[0, gi, it] = aug[:, :K_AUG].astype(BF16)


def _store_tiles(o_ref, gi, rows, y):
    for it in range(TS // TQ):
        o_ref[0, gi, it, rows, :] = y[:, it * TQ:(it + 1) * TQ]


def _project_queries(proj, off, qg_ref, cos, sin, q_ref):
    for gi in range(N_KV):
        y = proj(off + gi * 256, 256)
        for r in range(GQA):
            rows = slice(r * HEAD_DIM, (r + 1) * HEAD_DIM)
            yh = _head_norm_rope(y[rows], qg_ref[...], cos, sin)
            _store_tiles(q_ref, gi, rows, (yh * Q_SCALE).astype(BF16))


def _project_silu(proj, off, sz_ref):
    for gi in range(N_KV):
        z = proj(off + gi * 256, 256)
        _store_tiles(sz_ref, gi, slice(None), z * jax.nn.sigmoid(z))


def _block_indicator(blocks):
    row = lax.broadcasted_iota(jnp.int32, (HEAD_DIM, TQ), 0)
    lane = lax.broadcasted_iota(jnp.int32, (HEAD_DIM, TQ), 1)
    return (lane // (TQ // blocks) == row).astype(F32)


def _store_v_chunks(o_ref, y):
    yb = y.astype(BF16)
    for gi in range(N_KV):
        for it in range(TS // TQ):
            o_ref[0, gi, it] = yb[gi * HEAD_DIM:(gi + 1) * HEAD_DIM, it * TQ:(it + 1) * TQ]


NSA_OFF_Q = 0
NSA_OFF_KS = 1024
NSA_OFF_KW = 1280
NSA_OFF_KC = 1536
NSA_OFF_VC = 1792
NSA_OFF_VS = 2048
NSA_OFF_VW = 2304
NSA_OFF_GT = 2560
NSA_OFF_Z = NSA_OFF_GT + N_KV * GATE_ROWS
NSA_ROWS = NSA_OFF_Z + D_MODEL


def _nsa_in_kernel(x_ref, ng_ref, sc_ref, sh_ref, w_ref, qg_ref, ksg_ref, kwg_ref, gb_ref,
                   cos_ref, sin_ref,
                   q_ref, ks_ref, kw_ref, kc_ref, vc_ref, vs_ref, vw_ref, gt_ref, sz_ref):
    hb = _norm_mod(x_ref, ng_ref, sc_ref, sh_ref)
    cos = cos_ref[0]
    sin = sin_ref[0]

    def proj(r0, n):
        return _dot(w_ref[r0:r0 + n, :], hb)

    _project_queries(proj, NSA_OFF_Q, qg_ref, cos, sin, q_ref)
    sel_ind = _block_indicator(TQ // SEL_BLOCK)
    for off, g_ref, o_ref, ind in ((NSA_OFF_KS, ksg_ref, ks_ref, sel_ind),
                                   (NSA_OFF_KW, kwg_ref, kw_ref, None)):
        y = proj(off, 256)
        for gi in range(N_KV):
            yh = _head_norm_rope(y[gi * HEAD_DIM:(gi + 1) * HEAD_DIM], g_ref[...], cos, sin)
            _store_k_chunks(o_ref, gi, yh, ind)
    for off, o_ref in ((NSA_OFF_KC, kc_ref), (NSA_OFF_VC, vc_ref)):
        y = proj(off, 256)
        for gi in range(N_KV):
            for it in range(TS // TQ):
                o_ref[0, gi, it] = y[gi * HEAD_DIM:(gi + 1) * HEAD_DIM, it * TQ:(it + 1) * TQ].T
    _store_v_chunks(vs_ref, proj(NSA_OFF_VS, 256))
    _store_v_chunks(vw_ref, proj(NSA_OFF_VW, 256))
    gates = jax.nn.sigmoid(proj(NSA_OFF_GT, N_KV * GATE_ROWS) + gb_ref[...])
    for gi in range(N_KV):
        _store_tiles(gt_ref, gi, slice(None), gates[gi * GATE_ROWS:(gi + 1) * GATE_ROWS])
    _project_silu(proj, NSA_OFF_Z, sz_ref)


def _col_spec(rows):
    return pl.BlockSpec((rows, 1), lambda b, s: (0, 0))


def _bcol_spec(rows):
    return pl.BlockSpec((1, rows, 1), lambda b, s: (b, 0, 0))


def _fm_spec(rows):
    return pl.BlockSpec((1, rows, TS), lambda b, s: (b, 0, s))


def _k_chunk_spec(width):
    return pl.BlockSpec((1, N_KV, TS // TQ, TQ, width), lambda b, s: (b, 0, s, 0, 0))


def _k_chunk_shape(width, dtype=BF16):
    return jax.ShapeDtypeStruct((BATCH, N_KV, N_CHUNK, TQ, width), dtype)


def _tile_spec(rows):
    return pl.BlockSpec((1, N_KV, TS // TQ, rows, TQ), lambda b, s: (b, 0, s, 0, 0))


def _tile_shape(rows, dtype):
    return jax.ShapeDtypeStruct((BATCH, N_KV, N_CHUNK, rows, TQ), dtype)


_V_CHUNK_SPEC = pl.BlockSpec((1, N_KV, TS // TQ, HEAD_DIM, TQ), lambda b, s: (b, 0, s, 0, 0))
_V_CHUNK_SHAPE = jax.ShapeDtypeStruct((BATCH, N_KV, N_CHUNK, HEAD_DIM, TQ), BF16)


def _nsa_in_call(xT, ng, sc, sh, wT, qg, ksg, kwg, gb, cosT, sinT):
    fm = lambda rows, dt: jax.ShapeDtypeStruct((BATCH, rows, SEQ), dt)
    return pl.pallas_call(
        _nsa_in_kernel,
        grid=(BATCH, SEQ // TS),
        in_specs=[
            _fm_spec(D_MODEL), _col_spec(D_MODEL), _bcol_spec(D_MODEL), _bcol_spec(D_MODEL),
            pl.BlockSpec((NSA_ROWS, D_MODEL), lambda b, s: (0, 0)),
            _col_spec(HEAD_DIM), _col_spec(HEAD_DIM), _col_spec(HEAD_DIM),
            _col_spec(N_KV * GATE_ROWS),
            _fm_spec(ROT_HALF), _fm_spec(ROT_HALF),
        ],
        out_specs=[
            _tile_spec(GQA * HEAD_DIM), _k_chunk_spec(K_AUG), _k_chunk_spec(HEAD_DIM),
            _k_chunk_spec(HEAD_DIM), _k_chunk_spec(HEAD_DIM),
            _V_CHUNK_SPEC, _V_CHUNK_SPEC, _tile_spec(GATE_ROWS), _tile_spec(GQA * HEAD_DIM),
        ],
        out_shape=[
            _tile_shape(GQA * HEAD_DIM, BF16), _k_chunk_shape(K_AUG), _k_chunk_shape(HEAD_DIM),
            _k_chunk_shape(HEAD_DIM, F32), _k_chunk_shape(HEAD_DIM, F32),
            _V_CHUNK_SHAPE, _V_CHUNK_SHAPE, _tile_shape(GATE_ROWS, F32),
            _tile_shape(GQA * HEAD_DIM, F32),
        ],
        compiler_params=_cparams(("parallel", "parallel")),
        name="nsa_in_proj",
    )(xT, ng, sc, sh, wT, qg, ksg, kwg, gb, cosT, sinT)


MOBA_OFF_Q = 0
MOBA_OFF_K = 1024
MOBA_OFF_V = 1280
MOBA_OFF_Z = 1536
MOBA_ROWS = 2560
KM_LANES = 128


def _moba_in_kernel(x_ref, ng_ref, sc_ref, sh_ref, w_ref, qg_ref, kg_ref, cos_ref, sin_ref,
                    q_ref, k_ref, km_ref, v_ref, sz_ref):
    hb = _norm_mod(x_ref, ng_ref, sc_ref, sh_ref)
    cos = cos_ref[0]
    sin = sin_ref[0]

    def proj(r0, n):
        return _dot(w_ref[r0:r0 + n, :], hb)

    _project_queries(proj, MOBA_OFF_Q, qg_ref, cos, sin, q_ref)
    y = proj(MOBA_OFF_K, 256)
    lane = lax.broadcasted_iota(jnp.int32, (HEAD_DIM, KM_LANES), 1)
    for gi in range(N_KV):
        yh = _head_norm_rope(y[gi * HEAD_DIM:(gi + 1) * HEAD_DIM], kg_ref[...], cos, sin)
        _store_k_chunks(k_ref, gi, yh, _block_indicator(1))
        km = jnp.zeros((HEAD_DIM, KM_LANES), F32)
        for it in range(TS // TQ):
            mean = jnp.mean(yh[:, it * TQ:(it + 1) * TQ], axis=1, keepdims=True)
            km = jnp.where(lane == it, mean, km)
        km_ref[0, 0, gi * HEAD_DIM:(gi + 1) * HEAD_DIM, :] = km
    _store_v_chunks(v_ref, proj(MOBA_OFF_V, 256))
    _project_silu(proj, MOBA_OFF_Z, sz_ref)


def _moba_in_call(xT, ng, sc, sh, wT, qg, kg, cosT, sinT):
    fm = lambda rows, dt: jax.ShapeDtypeStruct((BATCH, rows, SEQ), dt)
    return pl.pallas_call(
        _moba_in_kernel,
        grid=(BATCH, SEQ // TS),
        in_specs=[
            _fm_spec(D_MODEL), _col_spec(D_MODEL), _bcol_spec(D_MODEL), _bcol_spec(D_MODEL),
            pl.BlockSpec((MOBA_ROWS, D_MODEL), lambda b, s: (0, 0)),
            _col_spec(HEAD_DIM), _col_spec(HEAD_DIM),
            _fm_spec(ROT_HALF), _fm_spec(ROT_HALF),
        ],
        out_specs=[
            _tile_spec(GQA * HEAD_DIM), _k_chunk_spec(K_AUG),
            pl.BlockSpec((1, 1, N_KV * HEAD_DIM, KM_LANES), lambda b, s: (b, s, 0, 0)),
            _V_CHUNK_SPEC, _tile_spec(GQA * HEAD_DIM),
        ],
        out_shape=[
            _tile_shape(GQA * HEAD_DIM, BF16), _k_chunk_shape(K_AUG),
            jax.ShapeDtypeStruct((BATCH, SEQ // TS, N_KV * HEAD_DIM, KM_LANES), F32),
            _V_CHUNK_SHAPE, _tile_shape(GQA * HEAD_DIM, F32),
        ],
        compiler_params=_cparams(("parallel", "parallel")),
        name="moba_in_proj",
    )(xT, ng, sc, sh, wT, qg, kg, cosT, sinT)


CMP_NB = 8
CMP_ROWS = CMP_NB * N_CMP_PAD
CMP_COLS = BATCH * N_KV * N_CMP_PAD
CMP_FEAT = CMP_STRIDE * HEAD_DIM


def _compress_mlp(x_ref, pea_ref, peb_ref, w1a_ref, w1b_ref, b1_ref, w2t_ref):
    x = x_ref[...].reshape(CMP_ROWS, CMP_FEAT)
    u = _dot((x + pea_ref[...]).astype(BF16), w1a_ref[...])
    v = _dot((x + peb_ref[...]).astype(BF16), w1b_ref[...])
    v = pltpu.roll(v, CMP_ROWS - 1, 0)
    h = u + v + b1_ref[...]
    h = 0.5 * h * (1.0 + jnp.tanh(np.sqrt(2.0 / np.pi) * (h + 0.044715 * (h * h * h))))
    return lax.dot_general(w2t_ref[...], h.astype(BF16), (((1,), (1,)), ((), ())),
                           preferred_element_type=F32)


def _compress_key_kernel(x_ref, pea_ref, peb_ref, w1a_ref, w1b_ref, b1_ref, w2t_ref, kg_ref,
                         cos_ref, sin_ref, o_ref):
    out = _compress_mlp(x_ref, pea_ref, peb_ref, w1a_ref, w1b_ref, b1_ref, w2t_ref)
    out = _head_norm_rope(out, kg_ref[...], cos_ref[...], sin_ref[...])
    for i in range(CMP_NB):
        o_ref[i] = out[:, i * N_CMP_PAD:(i + 1) * N_CMP_PAD].T.astype(BF16)


def _compress_value_kernel(x_ref, pea_ref, peb_ref, w1a_ref, w1b_ref, b1_ref, w2t_ref, o_ref):
    out = _compress_mlp(x_ref, pea_ref, peb_ref, w1a_ref, w1b_ref, b1_ref, w2t_ref)
    for i in range(CMP_NB):
        o_ref[i] = out[:, i * N_CMP_PAD:(i + 1) * N_CMP_PAD].astype(BF16)


def _compress_call(x, pe, w1, b1, w2, key_extras=None):
    full = lambda shape: pl.BlockSpec(shape, lambda t: (0,) * len(shape))
    in_specs = [
        pl.BlockSpec((CMP_NB, N_CMP_PAD, CMP_FEAT), lambda t: (t, 0, 0)),
        full((1, CMP_FEAT)), full((1, CMP_FEAT)),
        full((CMP_FEAT, CMP_HID)), full((CMP_FEAT, CMP_HID)),
        full((1, CMP_HID)), full((HEAD_DIM, CMP_HID)),
    ]
    pe = pe.reshape(2, 1, CMP_FEAT)
    w1 = w1.astype(BF16)
    args = [x, pe[0], pe[1], w1[:CMP_FEAT], w1[CMP_FEAT:], b1.reshape(1, CMP_HID), w2.T.astype(BF16)]
    if key_extras is not None:
        in_specs += [full((HEAD_DIM, 1)),
                     pl.BlockSpec((ROT_HALF, CMP_ROWS), lambda t: (0, t)),
                     pl.BlockSpec((ROT_HALF, CMP_ROWS), lambda t: (0, t))]
        args += list(key_extras)
        kernel, out_tail = _compress_key_kernel, (N_CMP_PAD, HEAD_DIM)
    else:
        kernel, out_tail = _compress_value_kernel, (HEAD_DIM, N_CMP_PAD)
    return pl.pallas_call(
        kernel,
        grid=(BATCH * N_KV // CMP_NB,),
        in_specs=in_specs,
        out_specs=pl.BlockSpec((CMP_NB,) + out_tail, lambda t: (t, 0, 0)),
        out_shape=jax.ShapeDtypeStruct((BATCH * N_KV,) + out_tail, BF16),
        compiler_params=_cparams(("parallel",)),
        name="nsa_compress_key" if key_extras is not None else "nsa_compress_value",
    )(*args)


def _group_queries(q_ref):
    return jnp.concatenate(
        [q_ref[0, r * HEAD_DIM:(r + 1) * HEAD_DIM, :] for r in range(GQA)], axis=1)


COL = 128


def _col_tiles():
    per_head = TQ // COL
    for ct in range(LANES_Q // COL):
        r, h = divmod(ct, per_head)
        yield r, slice(h * COL, (h + 1) * COL), slice(ct * COL, (ct + 1) * COL)


def _flash_scratch():
    return [
        pltpu.VMEM((2, TQ, LANES_Q), BF16),
        pltpu.VMEM((2, 1, LANES_Q), F32),
        pltpu.VMEM((1, LANES_Q), F32),
        pltpu.VMEM((1, LANES_Q), F32),
        pltpu.VMEM((HEAD_DIM, LANES_Q), F32),
    ]


class _FlashBranch:
    def __init__(self, q_ref, k_ref, v_ref, bufs, own, past_chunk, own_fix=None, past_fix=None,
                 own_rows=None, past_rows=None):
        self.q_ref, self.k_ref, self.v_ref, self.bufs = q_ref, k_ref, v_ref, bufs
        self.own_chunk, self.past_chunk = own, past_chunk
        self.own_fix, self.past_fix, self.own_rows, self.past_rows = own_fix, past_fix, own_rows, past_rows

    def _softmax_tile(self, s, cs, slot, init):
        p_buf, a_buf, m_ref, l_ref, _ = self.bufs
        m_loc = jnp.max(s, axis=0, keepdims=True)
        if init:
            m_new = m_loc
            a_buf[slot, :, cs] = jnp.ones((1, COL), F32)
        else:
            m_old = m_ref[:, cs]
            m_new = jnp.maximum(m_old, m_loc)
            alpha = jnp.exp2(m_old - m_new)
            a_buf[slot, :, cs] = alpha
        p = jnp.exp2(s - m_new)
        p_sum = jnp.sum(p, axis=0, keepdims=True)
        l_ref[:, cs] = p_sum if init else alpha * l_ref[:, cs] + p_sum
        m_ref[:, cs] = m_new
        p_buf[slot, :, cs] = p.astype(BF16)

    def _scores(self, k, r, qs, cs, rows, fix):
        q = self.q_ref[0, r * HEAD_DIM:(r + 1) * HEAD_DIM, qs]
        if rows is not None:
            q = jnp.concatenate([q, rows(cs)], axis=0)
        s = _dot(k, q)
        return s if fix is None else fix(s, qs)

    def _pv_tile(self, vT, slot, cs):
        p_buf, a_buf, _, _, acc_ref = self.bufs
        acc_ref[:, cs] = a_buf[slot, :, cs] * acc_ref[:, cs] + _dot(vT, p_buf[slot, :, cs])

    def own(self):
        acc_ref = self.bufs[4]
        acc_ref[...] = jnp.zeros(acc_ref.shape, F32)
        k_own = self.k_ref[0, 0, self.own_chunk]
        for r, qs, cs in _col_tiles():
            self._softmax_tile(self._scores(k_own, r, qs, cs, self.own_rows, self.own_fix), cs, 0, True)

    def trip(self, j, cur, prv):
        k = self.k_ref[0, 0, self.past_chunk(j)]
        prev = self.own_chunk if isinstance(j, int) and j == 0 else jnp.where(
            j == 0, self.own_chunk, self.past_chunk(jnp.maximum(j - 1, 0)))
        vT = self.v_ref[0, 0, prev]
        rows = None if self.past_rows is None else self.past_rows(j)
        for r, qs, cs in _col_tiles():
            s = self._scores(k, r, qs, cs, rows, self.past_fix)
            self._pv_tile(vT, prv, cs)
            self._softmax_tile(s, cs, cur, False)

    def run_past(self, n_past):
        def pair(pp, carry):
            self.trip(2 * pp, 1, 0)
            self.trip(2 * pp + 1, 0, 1)
            return carry

        lax.fori_loop(0, n_past // 2, pair, 0)

        @pl.when(n_past % 2 == 1)
        def _():
            self.trip(n_past - 1, 1, 0)

    def finish(self, n_past):
        last = jnp.where(n_past == 0, self.own_chunk, self.past_chunk(jnp.maximum(n_past - 1, 0)))
        vT = self.v_ref[0, 0, last]
        for _, _, cs in _col_tiles():
            self._pv_tile(vT, n_past % 2, cs)
        return self.bufs[4][...] / self.bufs[3][...]


def _flash_branch(q_ref, k_ref, v_ref, bufs, own, n_past, past_chunk, **kw):
    branch = _FlashBranch(q_ref, k_ref, v_ref, bufs, own, past_chunk, **kw)
    branch.own()
    branch.run_past(n_past)
    return branch.finish(n_past)


def _causal_fix(keep_lower):
    a_idx = lax.broadcasted_iota(jnp.int32, (TQ, COL), 0)
    lane = lax.broadcasted_iota(jnp.int32, (TQ, COL), 1)

    def fix(s, qs):
        lower = a_idx <= lane + qs.start
        return jnp.where(lower if keep_lower else ~lower, s, NEG)

    return fix


def _rank_select(score_ref, n_rows, j_idx, count):
    score = score_ref[...]
    cnt = jnp.zeros(score.shape, jnp.int32)
    for jp in range(n_rows):
        row = score_ref[jp:jp + 1, :]
        beats = (row > score) | ((row == score) & (jp < j_idx))
        cnt = cnt + beats.astype(jnp.int32)
    return cnt < count


class _TileView:
    def __init__(self, ref, tile):
        self.ref, self.tile = ref, tile

    def __getitem__(self, idx):
        return self.ref[(0, 0, self.tile) + tuple(idx[1:])]

    def __setitem__(self, idx, value):
        self.ref[(0, 0, self.tile) + tuple(idx[1:])] = value


def _for_each_query_tile(tile_fn, tiled_refs, other_refs):
    def body(qi, carry):
        tile_fn(qi, *[_TileView(ref, qi) for ref in tiled_refs], *other_refs)
        return carry

    lax.fori_loop(0, N_CHUNK, body, 0)


def _store_out(o_ref, sz_ref, o):
    for r in range(GQA):
        rows = slice(r * HEAD_DIM, (r + 1) * HEAD_DIM)
        o_ref[0, rows, :] = (o[:, r * TQ:(r + 1) * TQ] * sz_ref[0, rows, :]).astype(BF16)


def _nsa_attn_kernel(q_ref, kc_ref, vc_ref, ks_ref, vs_ref, kw_ref, vw_ref, gt_ref, sz_ref,
                     selw_ref, o_ref, score_ref, bias_ref, *bufs):
    _for_each_query_tile(_nsa_attn_tile, (q_ref, gt_ref, sz_ref, o_ref),
                         (kc_ref, vc_ref, ks_ref, vs_ref, kw_ref, vw_ref, selw_ref,
                          score_ref, bias_ref) + tuple(bufs))


def _nsa_attn_tile(qi, q_ref, gt_ref, sz_ref, o_ref, kc_ref, vc_ref, ks_ref, vs_ref, kw_ref, vw_ref,
                   selw_ref, score_ref, bias_ref, *bufs):
    q4 = _group_queries(q_ref)

    sel_bufs, win_bufs = bufs[:len(bufs) // 2], bufs[len(bufs) // 2:]
    keep_lower = _causal_fix(True)
    a_idx = lax.broadcasted_iota(jnp.int32, (TQ, COL), 0)
    lane = lax.broadcasted_iota(jnp.int32, (TQ, COL), 1)
    no_prev = jnp.where(qi > 0, 0, TQ)

    def keep_upper(s, qs):
        return jnp.where(a_idx > lane + (qs.start + no_prev), s, NEG)

    win = _FlashBranch(q_ref, kw_ref, vw_ref, win_bufs, qi, lambda j: jnp.maximum(qi - 1, 0),
                       own_fix=keep_lower, past_fix=keep_upper)

    t = qi * TQ + (lax.broadcasted_iota(jnp.int32, (1, LANES_Q), 1) & (TQ - 1))
    n_idx = lax.broadcasted_iota(jnp.int32, (N_CMP_PAD, LANES_Q), 0)
    s_cmp = _dot(kc_ref[0, 0], q4)
    win.own()
    s = jnp.where(n_idx * CMP_STRIDE + (CMP_LEN - 1) <= t, s_cmp, NEG)
    m = jnp.max(s, axis=0, keepdims=True)
    p = jnp.exp2(s - m) * (m > 0.5 * NEG).astype(F32)
    p = p / jnp.maximum(jnp.sum(p, axis=0, keepdims=True), 1e-30)
    o_cmp = _dot(vc_ref[0, 0], p.astype(BF16))

    psum = p[:, 0:TQ]
    for r in range(1, GQA):
        psum = psum + p[:, r * TQ:(r + 1) * TQ]
    p_hi = psum.astype(BF16)
    p_lo = (psum - p_hi.astype(F32)).astype(BF16)
    imp = _dot(selw_ref[...], p_hi) + _dot(selw_ref[...], p_lo)
    j_idx = lax.broadcasted_iota(jnp.int32, (N_SEL, TQ), 0)
    cur = qi * (TQ // SEL_BLOCK) + (lax.broadcasted_iota(jnp.int32, (N_SEL, TQ), 1) >> 6)
    valid = j_idx <= cur
    forced = (j_idx == 0) | (j_idx == cur) | (j_idx == cur - 1)
    score_ref[...] = jnp.where(forced, jnp.inf, jnp.where(valid, imp, -jnp.inf))
    win.trip(0, 1, 0)
    sel = valid & _rank_select(score_ref, N_SEL, j_idx, SEL_COUNT)
    bias = jnp.where(sel, 0.0, NEG)
    bias = jnp.concatenate([bias] * GQA, axis=1)
    per_chunk = TQ // SEL_BLOCK
    bias_ref[...] = jnp.zeros(bias_ref.shape, F32)
    for jp in range(N_SEL):
        bias_ref[jp // per_chunk, jp % per_chunk:jp % per_chunk + 1, :] = bias[jp:jp + 1, :]

    o_win = win.finish(1)

    def sel_rows(j):
        return lambda cs: bias_ref[j, :, cs].astype(BF16)

    o_slc = _flash_branch(q_ref, ks_ref, vs_ref, sel_bufs, qi, qi, lambda j: j, own_fix=keep_lower,
                          own_rows=sel_rows(qi), past_rows=sel_rows)

    def gate(br):
        return jnp.concatenate(
            [gt_ref[0, br * GQA + r:br * GQA + r + 1, :] for r in range(GQA)], axis=1)

    o = gate(0) * o_cmp + gate(1) * o_slc + gate(2) * o_win
    _store_out(o_ref, sz_ref, o)


def _group_spec(*tail):
    return pl.BlockSpec((1, 1) + tail, lambda b, g: (b, g) + (0,) * len(tail))


def _attn_specs():
    q_spec = _group_spec(N_CHUNK, GQA * HEAD_DIM, TQ)
    k_spec = lambda width: _group_spec(N_CHUNK, TQ, width)
    v_spec = _group_spec(N_CHUNK, HEAD_DIM, TQ)
    return q_spec, k_spec, v_spec


def _nsa_attn_call(q, kc, vc, ks, vs, kw, vw, gt, sz, selw):
    q_spec, k_spec, v_spec = _attn_specs()
    return pl.pallas_call(
        _nsa_attn_kernel,
        grid=(BATCH, N_KV),
        in_specs=[
            q_spec,
            _group_spec(N_CMP_PAD, HEAD_DIM), _group_spec(HEAD_DIM, N_CMP_PAD),
            k_spec(K_AUG), v_spec, k_spec(HEAD_DIM), v_spec,
            _group_spec(N_CHUNK, GATE_ROWS, TQ),
            q_spec,
            pl.BlockSpec((N_SEL, N_CMP_PAD), lambda b, g: (0, 0)),
        ],
        out_specs=q_spec,
        out_shape=_tile_shape(GQA * HEAD_DIM, BF16),
        scratch_shapes=[
            pltpu.VMEM((N_SEL, TQ), F32),
            pltpu.VMEM((N_CHUNK, BIAS_ROWS, LANES_Q), F32),
        ] + _flash_scratch() + _flash_scratch(),
        compiler_params=_cparams(("parallel", "parallel")),
        name="nsa_attention",
    )(q, kc, vc, ks, vs, kw, vw, gt, sz, selw)


def _moba_attn_kernel(q_ref, km_ref, k_ref, v_ref, sz_ref, o_ref,
                      score_ref, bias_ref, *bufs):
    _for_each_query_tile(_moba_attn_tile, (q_ref, sz_ref, o_ref),
                         (km_ref, k_ref, v_ref, score_ref, bias_ref) + tuple(bufs))


def _moba_attn_tile(qi, q_ref, sz_ref, o_ref, km_ref, k_ref, v_ref, score_ref, bias_ref, *bufs):
    q4 = _group_queries(q_ref)

    j_idx = lax.broadcasted_iota(jnp.int32, (N_CHUNK, LANES_Q), 0)
    past = j_idx < qi
    score_ref[...] = jnp.where(past, _dot(km_ref[0, 0], q4), -jnp.inf)
    sel = past & _rank_select(score_ref, N_CHUNK, j_idx, MOBA_TOPK)
    bias = jnp.where(sel, 0.0, NEG)
    bias_ref[...] = jnp.zeros(bias_ref.shape, F32)
    for jp in range(N_CHUNK):
        bias_ref[jp, 0:1, :] = bias[jp:jp + 1, :]

    o = _flash_branch(q_ref, k_ref, v_ref, bufs, qi, qi, lambda j: j, own_fix=_causal_fix(True),
                      own_rows=lambda cs: jnp.zeros((BIAS_ROWS, COL), BF16),
                      past_rows=lambda j: (lambda cs: bias_ref[j, :, cs].astype(BF16)))
    _store_out(o_ref, sz_ref, o)


def _moba_attn_call(q, km, k, v, sz):
    q_spec, k_spec, v_spec = _attn_specs()
    return pl.pallas_call(
        _moba_attn_kernel,
        grid=(BATCH, N_KV),
        in_specs=[
            q_spec,
            _group_spec(N_CHUNK, HEAD_DIM),
            k_spec(K_AUG), v_spec, q_spec,
        ],
        out_specs=q_spec,
        out_shape=_tile_shape(GQA * HEAD_DIM, BF16),
        scratch_shapes=[
            pltpu.VMEM((N_CHUNK, LANES_Q), F32),
            pltpu.VMEM((N_CHUNK, BIAS_ROWS, LANES_Q), F32),
        ] + _flash_scratch(),
        compiler_params=_cparams(("parallel", "parallel")),
        name="moba_attention",
    )(q, km, k, v, sz)


def _out_kernel(oz_ref, w_ref, x_ref, gt_ref, o_ref):
    group_rows = GQA * HEAD_DIM
    for it in range(TS // TQ):
        lanes = slice(it * TQ, (it + 1) * TQ)
        for cblk in range(D_MODEL // 256):
            rows = slice(cblk * 256, (cblk + 1) * 256)
            y = _dot(w_ref[rows, 0:group_rows], oz_ref[0, 0, it])
            for gi in range(1, N_KV):
                y = y + _dot(w_ref[rows, gi * group_rows:(gi + 1) * group_rows], oz_ref[0, gi, it])
            o_ref[0, rows, lanes] = x_ref[0, rows, lanes] + gt_ref[0, rows, :] * y


def _out_call(oz, wT, xT, gate):
    return pl.pallas_call(
        _out_kernel,
        grid=(BATCH, SEQ // TS),
        in_specs=[
            _tile_spec(GQA * HEAD_DIM),
            pl.BlockSpec((D_MODEL, D_MODEL), lambda b, s: (0, 0)),
            _fm_spec(D_MODEL), _bcol_spec(D_MODEL),
        ],
        out_specs=_fm_spec(D_MODEL),
        out_shape=jax.ShapeDtypeStruct((BATCH, D_MODEL, SEQ), F32),
        compiler_params=_cparams(("parallel", "parallel")),
        name="out_proj",
    )(oz, wT, xT, gate)


def _rope_tables(pos):
    inv_freq = ROPE_THETA ** (-jnp.arange(0, 2 * ROT_HALF, 2, dtype=F32) / (2 * ROT_HALF))
    ang = pos.astype(F32)[..., None] * inv_freq
    return jnp.cos(ang), jnp.sin(ang)


def _gate_perm():
    perm = np.full((N_KV * GATE_ROWS,), 3 * N_HEADS, dtype=np.int32)
    for g in range(N_KV):
        for br in range(3):
            for r in range(GQA):
                perm[g * GATE_ROWS + br * GQA + r] = (g * GQA + r) * 3 + br
    return perm


def _sel_weights_T():
    cs = np.arange(N_CMP)[:, None] * CMP_STRIDE
    ss = np.arange(N_SEL)[None, :] * SEL_BLOCK
    shared = np.clip(np.minimum(cs + CMP_LEN, ss + SEL_BLOCK) - np.maximum(cs, ss), 0, None)
    w = np.zeros((N_CMP_PAD, N_SEL), np.float32)
    w[:N_CMP] = shared / CMP_LEN
    return jnp.asarray(w.T, dtype=BF16)


def _col(v):
    return v.reshape(-1, 1)


def _nsa_layer(xT, ng, sc, sh, gate, cosT, sinT, cosc, sinc, w_in, w_out, q_g, k_g,
               cmp_pe, cmp_w1, cmp_b1, cmp_w2, gate_b):
    widths = [1024] + [256] * 6 + [3 * N_HEADS, 1024]
    q, kc, vc, ks, vs, kw, vw, gl, z = jnp.split(w_in, np.cumsum(widths)[:-1].tolist(), axis=1)
    perm = _gate_perm()
    gl_p = jnp.concatenate([gl, jnp.zeros((D_MODEL, 1), F32)], axis=1)[:, perm]
    gb_p = jnp.concatenate([gate_b, jnp.zeros((1,), F32)])[perm]
    wT = jnp.concatenate([q, ks, kw, kc, vc, vs, vw, gl_p, z], axis=1).T.astype(BF16)

    qT, ks_c, kw_c, kc_c, vc_c, vs_c, vw_c, gates, sz = _nsa_in_call(
        xT, ng, sc, sh, wT, _col(q_g), _col(k_g[1]), _col(k_g[2]), _col(gb_p), cosT, sinT)

    half_blocks = lambda t: t.reshape(BATCH * N_KV, N_CMP_PAD, CMP_FEAT)
    k_cmp = _compress_call(half_blocks(kc_c), cmp_pe[0], cmp_w1[0], cmp_b1[0], cmp_w2[0],
                           key_extras=(_col(k_g[0]), cosc, sinc))
    v_cmpT = _compress_call(half_blocks(vc_c), cmp_pe[1], cmp_w1[1], cmp_b1[1], cmp_w2[1])
    k_cmp = k_cmp.reshape(BATCH, N_KV, N_CMP_PAD, HEAD_DIM)
    v_cmpT = v_cmpT.reshape(BATCH, N_KV, HEAD_DIM, N_CMP_PAD)

    oz = _nsa_attn_call(qT, k_cmp, v_cmpT, ks_c, vs_c, kw_c, vw_c, gates, sz, _sel_weights_T())
    return _out_call(oz, w_out.T.astype(BF16), xT, gate)


def _moba_layer(xT, ng, sc, sh, gate, cosT, sinT, w_in, w_out, q_g, k_g):
    wT = w_in.T.astype(BF16)
    qT, k_c, km, v_c, sz = _moba_in_call(xT, ng, sc, sh, wT, _col(q_g), _col(k_g), cosT, sinT)
    nb = TS // TQ
    km = km[..., :nb].reshape(BATCH, SEQ // TS, N_KV, HEAD_DIM, nb)
    km = km.transpose(0, 2, 1, 4, 3).reshape(BATCH, N_KV, N_CHUNK, HEAD_DIM).astype(BF16)
    oz = _moba_attn_call(qT, km, k_c, v_c, sz)
    return _out_call(oz, w_out.T.astype(BF16), xT, gate)


@jax.jit
def _forward(x, c, positions, norm_g, ada_w, ada_b, nsa_w_in, nsa_w_out, nsa_q_norm, nsa_k_norm,
             nsa_cmp_pe, nsa_cmp_w1, nsa_cmp_b1, nsa_cmp_w2, nsa_gate_b,
             moba_w_in, moba_w_out, moba_q_norm, moba_k_norm):
    cos, sin = _rope_tables(positions)
    cosT = cos.transpose(0, 2, 1)
    sinT = sin.transpose(0, 2, 1)
    cmp_end = np.minimum(np.arange(N_CMP_PAD) * CMP_STRIDE + CMP_LEN - 1, SEQ - 1)
    cos_c, sin_c = _rope_tables(positions[:, cmp_end])

    def cmp_table(t):
        t = jnp.broadcast_to(t.transpose(2, 0, 1)[:, :, None, :],
                             (ROT_HALF, BATCH, N_KV, N_CMP_PAD))
        return t.reshape(ROT_HALF, CMP_COLS)

    cosc = cmp_table(cos_c)
    sinc = cmp_table(sin_c)

    mod = _ada_call(c, ada_w, ada_b)
    shift, scale, gate = jnp.split(mod[..., None], 3, axis=2)

    xT = x.transpose(0, 2, 1)
    for i in range(DEPTH):
        j = i // 2
        ng = _col(norm_g[i])
        if i % 2 == 0:
            xT = _nsa_layer(xT, ng, scale[i], shift[i], gate[i], cosT, sinT, cosc, sinc,
                            nsa_w_in[j], nsa_w_out[j], nsa_q_norm[j], nsa_k_norm[j],
                            nsa_cmp_pe[j], nsa_cmp_w1[j], nsa_cmp_b1[j], nsa_cmp_w2[j],
                            nsa_gate_b[j])
        else:
            xT = _moba_layer(xT, ng, scale[i], shift[i], gate[i], cosT, sinT,
                             moba_w_in[j], moba_w_out[j], moba_q_norm[j], moba_k_norm[j])
    return xT.transpose(0, 2, 1)


def kernel(x, c, positions, norm_g, ada_w, ada_b, nsa_w_in, nsa_w_out, nsa_q_norm, nsa_k_norm, nsa_cmp_pe, nsa_cmp_w1, nsa_cmp_b1, nsa_cmp_w2, nsa_gate_b, moba_w_in, moba_w_out, moba_q_norm, moba_k_norm):
    return _forward(x, c, positions, norm_g, ada_w, ada_b, nsa_w_in, nsa_w_out, nsa_q_norm,
                    nsa_k_norm, nsa_cmp_pe, nsa_cmp_w1, nsa_cmp_b1, nsa_cmp_w2, nsa_gate_b,
                    moba_w_in, moba_w_out, moba_q_norm, moba_k_norm)
```

```python
import functools

import numpy as np
import jax
import jax.numpy as jnp
from jax import lax
from jax.experimental import pallas as pl
from jax.experimental.pallas import tpu as pltpu

D_MODEL = 1024
BATCH = 16
SEQ = 2048
DEPTH = 4
HEAD_DIM = 64
N_HEADS = 16
N_KV = 4
GQA = 4
ROT_HALF = 8
ROPE_THETA = 500000.0
NORM_EPS = 1e-6
CMP_LEN = 32
CMP_STRIDE = 16
CMP_HID = 256
SEL_BLOCK = 64
SEL_COUNT = 8
N_SEL = SEQ // SEL_BLOCK
N_CMP = (SEQ - CMP_LEN) // CMP_STRIDE + 1
N_CMP_PAD = 128
MOBA_TOPK = 3

TQ = 256
N_CHUNK = SEQ // TQ
LANES_Q = GQA * TQ
TS = 512
NEG = -1e30
BIAS_ROWS = 16
K_AUG = HEAD_DIM + BIAS_ROWS
V_AUG = HEAD_DIM + 16
PV_COL = 256
Q_SCALE = HEAD_DIM ** -0.5 * float(np.log2(np.e))
GATE_ROWS = 16

F32 = jnp.float32
BF16 = jnp.bfloat16

VMEM_LIMIT = 52 * 1024 * 1024


def _cparams(sem):
    return pltpu.CompilerParams(dimension_semantics=sem, vmem_limit_bytes=VMEM_LIMIT)


def _dot(a, b):
    return jnp.dot(a, b, preferred_element_type=F32)


def _ada_kernel(c_ref, w_ref, b_ref, o_ref):
    cond = c_ref[...]
    cond = cond * jax.nn.sigmoid(cond)
    o_ref[0] = jnp.dot(cond, w_ref[0], precision=lax.Precision.HIGHEST,
                       preferred_element_type=F32) + b_ref[0]


def _ada_call(c, ada_w, ada_b):
    nt = 1024
    return pl.pallas_call(
        _ada_kernel,
        grid=(DEPTH, 3 * D_MODEL // nt),
        in_specs=[
            pl.BlockSpec((BATCH, D_MODEL), lambda i, n: (0, 0)),
            pl.BlockSpec((1, D_MODEL, nt), lambda i, n: (i, 0, n)),
            pl.BlockSpec((1, 1, nt), lambda i, n: (i, 0, n)),
        ],
        out_specs=pl.BlockSpec((1, BATCH, nt), lambda i, n: (i, 0, n)),
        out_shape=jax.ShapeDtypeStruct((DEPTH, BATCH, 3 * D_MODEL), F32),
        compiler_params=_cparams(("parallel", "parallel")),
        name="ada_mod",
    )(c, ada_w, ada_b.reshape(DEPTH, 1, 3 * D_MODEL))


def _norm_mod(x_ref, ng_ref, sc_ref, sh_ref):
    x = x_ref[0]
    ms = jnp.mean(x * x, axis=0, keepdims=True)
    y = x * lax.rsqrt(ms + NORM_EPS)
    h = (y * ng_ref[...]) * (1.0 + sc_ref[0]) + sh_ref[0]
    return h.astype(BF16)


def _head_norm_rope(y, g, cos, sin):
    ms = jnp.mean(y * y, axis=0, keepdims=True)
    yn = (y * lax.rsqrt(ms + NORM_EPS)) * g
    x1 = yn[0:ROT_HALF]
    x2 = yn[ROT_HALF:2 * ROT_HALF]
    return jnp.concatenate([x1 * cos - x2 * sin, x2 * cos + x1 * sin, yn[2 * ROT_HALF:]], axis=0)


def _store_k_chunks(o_ref, gi, yh, ind=None):
    for it in range(TS // TQ):
        chunk = yh[:, it * TQ:(it + 1) * TQ]
        if ind is None:
            o_ref[0, gi, it] = chunk.T.astype(BF16)
        else:
            aug = jnp.concatenate([chunk, ind], axis=0).T
            o_ref[0, gi, it] = aug[:, :K_AUG].astype(BF16)


def _store_tiles(o_ref, gi, rows, y):
    for it in range(TS // TQ):
        o_ref[0, gi, it, rows, :] = y[:, it * TQ:(it + 1) * TQ]


def _project_queries(proj, off, qg_ref, cos, sin, q_ref):
    for gi in range(N_KV):
        y = proj(off + gi * 256, 256)
        for r in range(GQA):
            rows = slice(r * HEAD_DIM, (r + 1) * HEAD_DIM)
            yh = _head_norm_rope(y[rows], qg_ref[...], cos, sin)
            _store_tiles(q_ref, gi, rows, (yh * Q_SCALE).astype(BF16))


def _project_silu(proj, off, sz_ref):
    for gi in range(N_KV):
        z = proj(off + gi * 256, 256)
        _store_tiles(sz_ref, gi, slice(None), z * jax.nn.sigmoid(z))


def _block_indicator(blocks):
    row = lax.broadcasted_iota(jnp.int32, (HEAD_DIM, TQ), 0)
    lane = lax.broadcasted_iota(jnp.int32, (HEAD_DIM, TQ), 1)
    return (lane // (TQ // blocks) == row).astype(F32)


def _store_v_chunks(o_ref, y):
    yb = y.astype(BF16)
    row = lax.broadcasted_iota(jnp.int32, (V_AUG - HEAD_DIM, TQ), 0)
    ones_row = (row == 0).astype(BF16)
    for gi in range(N_KV):
        for it in range(TS // TQ):
            v = yb[gi * HEAD_DIM:(gi + 1) * HEAD_DIM, it * TQ:(it + 1) * TQ]
            o_ref[0, gi, it] = jnp.concatenate([v, ones_row], axis=0)


NSA_OFF_Q = 0
NSA_OFF_KS = 1024
NSA_OFF_KW = 1280
NSA_OFF_KC = 1536
NSA_OFF_VC = 1792
NSA_OFF_VS = 2048
NSA_OFF_VW = 2304
NSA_OFF_GT = 2560
NSA_OFF_Z = NSA_OFF_GT + N_KV * GATE_ROWS
NSA_ROWS = NSA_OFF_Z + D_MODEL


def _nsa_in_kernel(x_ref, ng_ref, sc_ref, sh_ref, w_ref, qg_ref, ksg_ref, kwg_ref, gb_ref,
                   cos_ref, sin_ref,
                   q_ref, ks_ref, kw_ref, kc_ref, vc_ref, vs_ref, vw_ref, gt_ref, sz_ref):
    hb = _norm_mod(x_ref, ng_ref, sc_ref, sh_ref)
    cos = cos_ref[0]
    sin = sin_ref[0]

    def proj(r0, n):
        return _dot(w_ref[r0:r0 + n, :], hb)

    _project_queries(proj, NSA_OFF_Q, qg_ref, cos, sin, q_ref)
    sel_ind = _block_indicator(TQ // SEL_BLOCK)
    for off, g_ref, o_ref, ind in ((NSA_OFF_KS, ksg_ref, ks_ref, sel_ind),
                                   (NSA_OFF_KW, kwg_ref, kw_ref, None)):
        y = proj(off, 256)
        for gi in range(N_KV):
            yh = _head_norm_rope(y[gi * HEAD_DIM:(gi + 1) * HEAD_DIM], g_ref[...], cos, sin)
            _store_k_chunks(o_ref, gi, yh, ind)
    for off, o_ref in ((NSA_OFF_KC, kc_ref), (NSA_OFF_VC, vc_ref)):
        y = proj(off, 256)
        for gi in range(N_KV):
            for it in range(TS // TQ):
                o_ref[0, gi, it] = y[gi * HEAD_DIM:(gi + 1) * HEAD_DIM, it * TQ:(it + 1) * TQ].T
    _store_v_chunks(vs_ref, proj(NSA_OFF_VS, 256))
    _store_v_chunks(vw_ref, proj(NSA_OFF_VW, 256))
    gates = jax.nn.sigmoid(proj(NSA_OFF_GT, N_KV * GATE_ROWS) + gb_ref[...])
    for gi in range(N_KV):
        _store_tiles(gt_ref, gi, slice(None), gates[gi * GATE_ROWS:(gi + 1) * GATE_ROWS])
    _project_silu(proj, NSA_OFF_Z, sz_ref)


def _col_spec(rows):
    return pl.BlockSpec((rows, 1), lambda b, s: (0, 0))


def _bcol_spec(rows):
    return pl.BlockSpec((1, rows, 1), lambda b, s: (b, 0, 0))


def _fm_spec(rows):
    return pl.BlockSpec((1, rows, TS), lambda b, s: (b, 0, s))


def _k_chunk_spec(width):
    return pl.BlockSpec((1, N_KV, TS // TQ, TQ, width), lambda b, s: (b, 0, s, 0, 0))


def _k_chunk_shape(width, dtype=BF16):
    return jax.ShapeDtypeStruct((BATCH, N_KV, N_CHUNK, TQ, width), dtype)


def _tile_spec(rows):
    return pl.BlockSpec((1, N_KV, TS // TQ, rows, TQ), lambda b, s: (b, 0, s, 0, 0))


def _tile_shape(rows, dtype):
    return jax.ShapeDtypeStruct((BATCH, N_KV, N_CHUNK, rows, TQ), dtype)


_V_CHUNK_SPEC = pl.BlockSpec((1, N_KV, TS // TQ, V_AUG, TQ), lambda b, s: (b, 0, s, 0, 0))
_V_CHUNK_SHAPE = jax.ShapeDtypeStruct((BATCH, N_KV, N_CHUNK, V_AUG, TQ), BF16)


def _nsa_in_call(xT, ng, sc, sh, wT, qg, ksg, kwg, gb, cosT, sinT):
    fm = lambda rows, dt: jax.ShapeDtypeStruct((BATCH, rows, SEQ), dt)
    return pl.pallas_call(
        _nsa_in_kernel,
        grid=(BATCH, SEQ // TS),
        in_specs=[
            _fm_spec(D_MODEL), _col_spec(D_MODEL), _bcol_spec(D_MODEL), _bcol_spec(D_MODEL),
            pl.BlockSpec((NSA_ROWS, D_MODEL), lambda b, s: (0, 0)),
            _col_spec(HEAD_DIM), _col_spec(HEAD_DIM), _col_spec(HEAD_DIM),
            _col_spec(N_KV * GATE_ROWS),
            _fm_spec(ROT_HALF), _fm_spec(ROT_HALF),
        ],
        out_specs=[
            _tile_spec(GQA * HEAD_DIM), _k_chunk_spec(K_AUG), _k_chunk_spec(HEAD_DIM),
            _k_chunk_spec(HEAD_DIM), _k_chunk_spec(HEAD_DIM),
            _V_CHUNK_SPEC, _V_CHUNK_SPEC, _tile_spec(GATE_ROWS), _tile_spec(GQA * HEAD_DIM),
        ],
        out_shape=[
            _tile_shape(GQA * HEAD_DIM, BF16), _k_chunk_shape(K_AUG), _k_chunk_shape(HEAD_DIM),
            _k_chunk_shape(HEAD_DIM, F32), _k_chunk_shape(HEAD_DIM, F32),
            _V_CHUNK_SHAPE, _V_CHUNK_SHAPE, _tile_shape(GATE_ROWS, F32),
            _tile_shape(GQA * HEAD_DIM, F32),
        ],
        compiler_params=_cparams(("parallel", "parallel")),
        name="nsa_in_proj",
    )(xT, ng, sc, sh, wT, qg, ksg, kwg, gb, cosT, sinT)


MOBA_OFF_Q = 0
MOBA_OFF_K = 1024
MOBA_OFF_V = 1280
MOBA_OFF_Z = 1536
MOBA_ROWS = 2560
KM_LANES = 128


def _moba_in_kernel(x_ref, ng_ref, sc_ref, sh_ref, w_ref, qg_ref, kg_ref, cos_ref, sin_ref,
                    q_ref, k_ref, km_ref, v_ref, sz_ref):
    hb = _norm_mod(x_ref, ng_ref, sc_ref, sh_ref)
    cos = cos_ref[0]
    sin = sin_ref[0]

    def proj(r0, n):
        return _dot(w_ref[r0:r0 + n, :], hb)

    _project_queries(proj, MOBA_OFF_Q, qg_ref, cos, sin, q_ref)
    y = proj(MOBA_OFF_K, 256)
    lane = lax.broadcasted_iota(jnp.int32, (HEAD_DIM, KM_LANES), 1)
    for gi in range(N_KV):
        yh = _head_norm_rope(y[gi * HEAD_DIM:(gi + 1) * HEAD_DIM], kg_ref[...], cos, sin)
        _store_k_chunks(k_ref, gi, yh, _block_indicator(1))
        km = jnp.zeros((HEAD_DIM, KM_LANES), F32)
        for it in range(TS // TQ):
            mean = jnp.mean(yh[:, it * TQ:(it + 1) * TQ], axis=1, keepdims=True)
            km = jnp.where(lane == it, mean, km)
        km_ref[0, 0, gi * HEAD_DIM:(gi + 1) * HEAD_DIM, :] = km
    _store_v_chunks(v_ref, proj(MOBA_OFF_V, 256))
    _project_silu(proj, MOBA_OFF_Z, sz_ref)


def _moba_in_call(xT, ng, sc, sh, wT, qg, kg, cosT, sinT):
    fm = lambda rows, dt: jax.ShapeDtypeStruct((BATCH, rows, SEQ), dt)
    return pl.pallas_call(
        _moba_in_kernel,
        grid=(BATCH, SEQ // TS),
        in_specs=[
            _fm_spec(D_MODEL), _col_spec(D_MODEL), _bcol_spec(D_MODEL), _bcol_spec(D_MODEL),
            pl.BlockSpec((MOBA_ROWS, D_MODEL), lambda b, s: (0, 0)),
            _col_spec(HEAD_DIM), _col_spec(HEAD_DIM),
            _fm_spec(ROT_HALF), _fm_spec(ROT_HALF),
        ],
        out_specs=[
            _tile_spec(GQA * HEAD_DIM), _k_chunk_spec(K_AUG),
            pl.BlockSpec((1, 1, N_KV * HEAD_DIM, KM_LANES), lambda b, s: (b, s, 0, 0)),
            _V_CHUNK_SPEC, _tile_spec(GQA * HEAD_DIM),
        ],
        out_shape=[
            _tile_shape(GQA * HEAD_DIM, BF16), _k_chunk_shape(K_AUG),
            jax.ShapeDtypeStruct((BATCH, SEQ // TS, N_KV * HEAD_DIM, KM_LANES), F32),
            _V_CHUNK_SHAPE, _tile_shape(GQA * HEAD_DIM, F32),
        ],
        compiler_params=_cparams(("parallel", "parallel")),
        name="moba_in_proj",
    )(xT, ng, sc, sh, wT, qg, kg, cosT, sinT)


CMP_NB = 8
CMP_ROWS = CMP_NB * N_CMP_PAD
CMP_COLS = BATCH * N_KV * N_CMP_PAD
CMP_FEAT = CMP_STRIDE * HEAD_DIM


def _compress_mlp(x_ref, pea_ref, peb_ref, w1a_ref, w1b_ref, b1_ref, w2t_ref):
    x = x_ref[...].reshape(CMP_ROWS, CMP_FEAT)
    u = _dot((x + pea_ref[...]).astype(BF16), w1a_ref[...])
    v = _dot((x + peb_ref[...]).astype(BF16), w1b_ref[...])
    v = pltpu.roll(v, CMP_ROWS - 1, 0)
    h = u + v + b1_ref[...]
    h = 0.5 * h * (1.0 + jnp.tanh(np.sqrt(2.0 / np.pi) * (h + 0.044715 * (h * h * h))))
    return lax.dot_general(w2t_ref[...], h.astype(BF16), (((1,), (1,)), ((), ())),
                           preferred_element_type=F32)


def _compress_key_kernel(x_ref, pea_ref, peb_ref, w1a_ref, w1b_ref, b1_ref, w2t_ref, kg_ref,
                         cos_ref, sin_ref, o_ref):
    out = _compress_mlp(x_ref, pea_ref, peb_ref, w1a_ref, w1b_ref, b1_ref, w2t_ref)
    out = _head_norm_rope(out, kg_ref[...], cos_ref[...], sin_ref[...])
    for i in range(CMP_NB):
        o_ref[i] = out[:, i * N_CMP_PAD:(i + 1) * N_CMP_PAD].T.astype(BF16)


def _compress_value_kernel(x_ref, pea_ref, peb_ref, w1a_ref, w1b_ref, b1_ref, w2t_ref, o_ref):
    out = _compress_mlp(x_ref, pea_ref, peb_ref, w1a_ref, w1b_ref, b1_ref, w2t_ref)
    for i in range(CMP_NB):
        o_ref[i] = out[:, i * N_CMP_PAD:(i + 1) * N_CMP_PAD].astype(BF16)


def _compress_call(x, pe, w1, b1, w2, key_extras=None):
    full = lambda shape: pl.BlockSpec(shape, lambda t: (0,) * len(shape))
    in_specs = [
        pl.BlockSpec((CMP_NB, N_CMP_PAD, CMP_FEAT), lambda t: (t, 0, 0)),
        full((1, CMP_FEAT)), full((1, CMP_FEAT)),
        full((CMP_FEAT, CMP_HID)), full((CMP_FEAT, CMP_HID)),
        full((1, CMP_HID)), full((HEAD_DIM, CMP_HID)),
    ]
    pe = pe.reshape(2, 1, CMP_FEAT)
    w1 = w1.astype(BF16)
    args = [x, pe[0], pe[1], w1[:CMP_FEAT], w1[CMP_FEAT:], b1.reshape(1, CMP_HID), w2.T.astype(BF16)]
    if key_extras is not None:
        in_specs += [full((HEAD_DIM, 1)),
                     pl.BlockSpec((ROT_HALF, CMP_ROWS), lambda t: (0, t)),
                     pl.BlockSpec((ROT_HALF, CMP_ROWS), lambda t: (0, t))]
        args += list(key_extras)
        kernel, out_tail = _compress_key_kernel, (N_CMP_PAD, HEAD_DIM)
    else:
        kernel, out_tail = _compress_value_kernel, (HEAD_DIM, N_CMP_PAD)
    return pl.pallas_call(
        kernel,
        grid=(BATCH * N_KV // CMP_NB,),
        in_specs=in_specs,
        out_specs=pl.BlockSpec((CMP_NB,) + out_tail, lambda t: (t, 0, 0)),
        out_shape=jax.ShapeDtypeStruct((BATCH * N_KV,) + out_tail, BF16),
        compiler_params=_cparams(("parallel",)),
        name="nsa_compress_key" if key_extras is not None else "nsa_compress_value",
    )(*args)


def _group_queries(q_ref):
    return jnp.concatenate(
        [q_ref[0, r * HEAD_DIM:(r + 1) * HEAD_DIM, :] for r in range(GQA)], axis=1)


COL = 128


def _col_tiles():
    per_head = TQ // COL
    for ct in range(LANES_Q // COL):
        r, h = divmod(ct, per_head)
        yield r, slice(h * COL, (h + 1) * COL), slice(ct * COL, (ct + 1) * COL)


def _flash_scratch():
    return [
        pltpu.VMEM((2, TQ, LANES_Q), BF16),
        pltpu.VMEM((2, 1, LANES_Q), F32),
        pltpu.VMEM((1, LANES_Q), F32),
        pltpu.VMEM((V_AUG, LANES_Q), F32),
    ]


class _FlashBranch:
    def __init__(self, q_ref, k_ref, v_ref, bufs, own, past_chunk, own_fix=None, past_fix=None,
                 own_rows=None, past_rows=None):
        self.q_ref, self.k_ref, self.v_ref, self.bufs = q_ref, k_ref, v_ref, bufs
        self.own_chunk, self.past_chunk = own, past_chunk
        self.own_fix, self.past_fix, self.own_rows, self.past_rows = own_fix, past_fix, own_rows, past_rows

    def _softmax_tile(self, s, cs, slot, init):
        p_buf, a_buf, m_ref, _ = self.bufs
        m_loc = jnp.max(s, axis=0, keepdims=True)
        if init:
            m_new = m_loc
            a_buf[slot, :, cs] = jnp.ones((1, COL), F32)
        else:
            m_old = m_ref[:, cs]
            m_new = jnp.maximum(m_old, m_loc)
            a_buf[slot, :, cs] = jnp.exp2(m_old - m_new)
        m_ref[:, cs] = m_new
        p_buf[slot, :, cs] = jnp.exp2(s - m_new).astype(BF16)

    def _scores(self, k, r, qs, cs, rows, fix):
        q = self.q_ref[0, r * HEAD_DIM:(r + 1) * HEAD_DIM, qs]
        if rows is not None:
            q = jnp.concatenate([q, rows(cs)], axis=0)
        s = _dot(k, q)
        return s if fix is None else fix(s, qs)

    def _pv_slab(self, vT, slot, ct):
        if (ct * COL) % PV_COL:
            return
        p_buf, a_buf, _, acc_ref = self.bufs
        cs = slice(ct * COL, ct * COL + PV_COL)
        acc_ref[:, cs] = a_buf[slot, :, cs] * acc_ref[:, cs] + _dot(vT, p_buf[slot, :, cs])

    def own(self):
        acc_ref = self.bufs[3]
        acc_ref[...] = jnp.zeros(acc_ref.shape, F32)
        k_own = self.k_ref[0, 0, self.own_chunk]
        for r, qs, cs in _col_tiles():
            self._softmax_tile(self._scores(k_own, r, qs, cs, self.own_rows, self.own_fix), cs, 0, True)

    def trip(self, j, cur, prv):
        k = self.k_ref[0, 0, self.past_chunk(j)]
        prev = self.own_chunk if isinstance(j, int) and j == 0 else jnp.where(
            j == 0, self.own_chunk, self.past_chunk(jnp.maximum(j - 1, 0)))
        vT = self.v_ref[0, 0, prev]
        rows = None if self.past_rows is None else self.past_rows(j)
        for ct, (r, qs, cs) in enumerate(_col_tiles()):
            s = self._scores(k, r, qs, cs, rows, self.past_fix)
            self._pv_slab(vT, prv, ct)
            self._softmax_tile(s, cs, cur, False)

    def run_past(self, n_past):
        def pair(pp, carry):
            self.trip(2 * pp, 1, 0)
            self.trip(2 * pp + 1, 0, 1)
            return carry

        lax.fori_loop(0, n_past // 2, pair, 0)

        @pl.when(n_past % 2 == 1)
        def _():
            self.trip(n_past - 1, 1, 0)

    def finish(self, n_past):
        last = jnp.where(n_past == 0, self.own_chunk, self.past_chunk(jnp.maximum(n_past - 1, 0)))
        vT = self.v_ref[0, 0, last]
        for ct in range(LANES_Q // COL):
            self._pv_slab(vT, n_past % 2, ct)
        acc = self.bufs[3][...]
        return acc[:HEAD_DIM] / acc[HEAD_DIM:HEAD_DIM + 1]


def _flash_branch(q_ref, k_ref, v_ref, bufs, own, n_past, past_chunk, **kw):
    branch = _FlashBranch(q_ref, k_ref, v_ref, bufs, own, past_chunk, **kw)
    branch.own()
    branch.run_past(n_past)
    return branch.finish(n_past)


def _causal_fix(keep_lower):
    a_idx = lax.broadcasted_iota(jnp.int32, (TQ, COL), 0)
    lane = lax.broadcasted_iota(jnp.int32, (TQ, COL), 1)

    def fix(s, qs):
        lower = a_idx <= lane + qs.start
        return jnp.where(lower if keep_lower else ~lower, s, NEG)

    return fix


def _rank_select(score_ref, n_rows, j_idx, count):
    score = score_ref[...]
    cnt = jnp.zeros(score.shape, jnp.int32)
    for jp in range(n_rows):
        row = score_ref[jp:jp + 1, :]
        beats = (row > score) | ((row == score) & (jp < j_idx))
        cnt = cnt + beats.astype(jnp.int32)
    return cnt < count


class _TileView:
    def __init__(self, ref, tile):
        self.ref, self.tile = ref, tile

    def __getitem__(self, idx):
        return self.ref[(0, 0, self.tile) + tuple(idx[1:])]

    def __setitem__(self, idx, value):
        self.ref[(0, 0, self.tile) + tuple(idx[1:])] = value


def _for_each_query_tile(tile_fn, tiled_refs, other_refs):
    def body(qi, carry):
        tile_fn(qi, *[_TileView(ref, qi) for ref in tiled_refs], *other_refs)
        return carry

    lax.fori_loop(0, N_CHUNK, body, 0)


def _store_out(o_ref, sz_ref, o):
    for r in range(GQA):
        rows = slice(r * HEAD_DIM, (r + 1) * HEAD_DIM)
        o_ref[0, rows, :] = (o[:, r * TQ:(r + 1) * TQ] * sz_ref[0, rows, :]).astype(BF16)


def _nsa_attn_kernel(q_ref, kc_ref, vc_ref, ks_ref, vs_ref, kw_ref, vw_ref, gt_ref, sz_ref,
                     selw_ref, o_ref, score_ref, bias_ref, *bufs):
    _for_each_query_tile(_nsa_attn_tile, (q_ref, gt_ref, sz_ref, o_ref),
                         (kc_ref, vc_ref, ks_ref, vs_ref, kw_ref, vw_ref, selw_ref,
                          score_ref, bias_ref) + tuple(bufs))


def _nsa_attn_tile(qi, q_ref, gt_ref, sz_ref, o_ref, kc_ref, vc_ref, ks_ref, vs_ref, kw_ref, vw_ref,
                   selw_ref, score_ref, bias_ref, *bufs):
    q4 = _group_queries(q_ref)

    sel_bufs, win_bufs = bufs[:len(bufs) // 2], bufs[len(bufs) // 2:]
    keep_lower = _causal_fix(True)
    a_idx = lax.broadcasted_iota(jnp.int32, (TQ, COL), 0)
    lane = lax.broadcasted_iota(jnp.int32, (TQ, COL), 1)
    no_prev = jnp.where(qi > 0, 0, TQ)

    def keep_upper(s, qs):
        return jnp.where(a_idx > lane + (qs.start + no_prev), s, NEG)

    win = _FlashBranch(q_ref, kw_ref, vw_ref, win_bufs, qi, lambda j: jnp.maximum(qi - 1, 0),
                       own_fix=keep_lower, past_fix=keep_upper)

    t = qi * TQ + (lax.broadcasted_iota(jnp.int32, (1, LANES_Q), 1) & (TQ - 1))
    n_idx = lax.broadcasted_iota(jnp.int32, (N_CMP_PAD, LANES_Q), 0)
    s_cmp = _dot(kc_ref[0, 0], q4)
    win.own()
    s = jnp.where(n_idx * CMP_STRIDE + (CMP_LEN - 1) <= t, s_cmp, NEG)
    m = jnp.max(s, axis=0, keepdims=True)
    p = jnp.exp2(s - m) * (m > 0.5 * NEG).astype(F32)
    p = p / jnp.maximum(jnp.sum(p, axis=0, keepdims=True), 1e-30)
    o_cmp = _dot(vc_ref[0, 0], p.astype(BF16))

    psum = p[:, 0:TQ]
    for r in range(1, GQA):
        psum = psum + p[:, r * TQ:(r + 1) * TQ]
    p_hi = psum.astype(BF16)
    p_lo = (psum - p_hi.astype(F32)).astype(BF16)
    imp = _dot(selw_ref[...], p_hi) + _dot(selw_ref[...], p_lo)
    j_idx = lax.broadcasted_iota(jnp.int32, (N_SEL, TQ), 0)
    cur = qi * (TQ // SEL_BLOCK) + (lax.broadcasted_iota(jnp.int32, (N_SEL, TQ), 1) >> 6)
    valid = j_idx <= cur
    forced = (j_idx == 0) | (j_idx == cur) | (j_idx == cur - 1)
    score_ref[...] = jnp.where(forced, jnp.inf, jnp.where(valid, imp, -jnp.inf))
    win.trip(0, 1, 0)
    sel = valid & _rank_select(score_ref, N_SEL, j_idx, SEL_COUNT)
    bias = jnp.where(sel, 0.0, NEG)
    bias = jnp.concatenate([bias] * GQA, axis=1)
    per_chunk = TQ // SEL_BLOCK
    bias_ref[...] = jnp.zeros(bias_ref.shape, F32)
    for jp in range(N_SEL):
        bias_ref[jp // per_chunk, jp % per_chunk:jp % per_chunk + 1, :] = bias[jp:jp + 1, :]

    o_win = win.finish(1)

    def sel_rows(j):
        return lambda cs: bias_ref[j, :, cs].astype(BF16)

    o_slc = _flash_branch(q_ref, ks_ref, vs_ref, sel_bufs, qi, qi, lambda j: j, own_fix=keep_lower,
                          own_rows=sel_rows(qi), past_rows=sel_rows)

    def gate(br):
        return jnp.concatenate(
            [gt_ref[0, br * GQA + r:br * GQA + r + 1, :] for r in range(GQA)], axis=1)

    o = gate(0) * o_cmp + gate(1) * o_slc + gate(2) * o_win
    _store_out(o_ref, sz_ref, o)


def _group_spec(*tail):
    return pl.BlockSpec((1, 1) + tail, lambda b, g: (b, g) + (0,) * len(tail))


def _attn_specs():
    q_spec = _group_spec(N_CHUNK, GQA * HEAD_DIM, TQ)
    k_spec = lambda width: _group_spec(N_CHUNK, TQ, width)
    v_spec = _group_spec(N_CHUNK, V_AUG, TQ)
    return q_spec, k_spec, v_spec


def _nsa_attn_call(q, kc, vc, ks, vs, kw, vw, gt, sz, selw):
    q_spec, k_spec, v_spec = _attn_specs()
    return pl.pallas_call(
        _nsa_attn_kernel,
        grid=(BATCH, N_KV),
        in_specs=[
            q_spec,
            _group_spec(N_CMP_PAD, HEAD_DIM), _group_spec(HEAD_DIM, N_CMP_PAD),
            k_spec(K_AUG), v_spec, k_spec(HEAD_DIM), v_spec,
            _group_spec(N_CHUNK, GATE_ROWS, TQ),
            q_spec,
            pl.BlockSpec((N_SEL, N_CMP_PAD), lambda b, g: (0, 0)),
        ],
        out_specs=q_spec,
        out_shape=_tile_shape(GQA * HEAD_DIM, BF16),
        scratch_shapes=[
            pltpu.VMEM((N_SEL, TQ), F32),
            pltpu.VMEM((N_CHUNK, BIAS_ROWS, LANES_Q), F32),
        ] + _flash_scratch() + _flash_scratch(),
        compiler_params=_cparams(("parallel", "parallel")),
        name="nsa_attention",
    )(q, kc, vc, ks, vs, kw, vw, gt, sz, selw)


def _moba_attn_kernel(q_ref, km_ref, k_ref, v_ref, sz_ref, o_ref,
                      score_ref, bias_ref, *bufs):
    _for_each_query_tile(_moba_attn_tile, (q_ref, sz_ref, o_ref),
                         (km_ref, k_ref, v_ref, score_ref, bias_ref) + tuple(bufs))


def _moba_attn_tile(qi, q_ref, sz_ref, o_ref, km_ref, k_ref, v_ref, score_ref, bias_ref, *bufs):
    q4 = _group_queries(q_ref)

    j_idx = lax.broadcasted_iota(jnp.int32, (N_CHUNK, LANES_Q), 0)
    past = j_idx < qi
    score_ref[...] = jnp.where(past, _dot(km_ref[0, 0], q4), -jnp.inf)
    sel = past & _rank_select(score_ref, N_CHUNK, j_idx, MOBA_TOPK)
    bias = jnp.where(sel, 0.0, NEG)
    bias_ref[...] = jnp.zeros(bias_ref.shape, F32)
    for jp in range(N_CHUNK):
        bias_ref[jp, 0:1, :] = bias[jp:jp + 1, :]

    o = _flash_branch(q_ref, k_ref, v_ref, bufs, qi, qi, lambda j: j, own_fix=_causal_fix(True),
                      own_rows=lambda cs: jnp.zeros((BIAS_ROWS, COL), BF16),
                      past_rows=lambda j: (lambda cs: bias_ref[j, :, cs].astype(BF16)))
    _store_out(o_ref, sz_ref, o)


def _moba_attn_call(q, km, k, v, sz):
    q_spec, k_spec, v_spec = _attn_specs()
    return pl.pallas_call(
        _moba_attn_kernel,
        grid=(BATCH, N_KV),
        in_specs=[
            q_spec,
            _group_spec(N_CHUNK, HEAD_DIM),
            k_spec(K_AUG), v_spec, q_spec,
        ],
        out_specs=q_spec,
        out_shape=_tile_shape(GQA * HEAD_DIM, BF16),
        scratch_shapes=[
            pltpu.VMEM((N_CHUNK, LANES_Q), F32),
            pltpu.VMEM((N_CHUNK, BIAS_ROWS, LANES_Q), F32),
        ] + _flash_scratch(),
        compiler_params=_cparams(("parallel", "parallel")),
        name="moba_attention",
    )(q, km, k, v, sz)


def _out_kernel(oz_ref, w_ref, x_ref, gt_ref, o_ref):
    group_rows = GQA * HEAD_DIM
    for it in range(TS // TQ):
        lanes = slice(it * TQ, (it + 1) * TQ)
        for cblk in range(D_MODEL // 256):
            rows = slice(cblk * 256, (cblk + 1) * 256)
            y = _dot(w_ref[rows, 0:group_rows], oz_ref[0, 0, it])
            for gi in range(1, N_KV):
                y = y + _dot(w_ref[rows, gi * group_rows:(gi + 1) * group_rows], oz_ref[0, gi, it])
            o_ref[0, rows, lanes] = x_ref[0, rows, lanes] + gt_ref[0, rows, :] * y


def _out_call(oz, wT, xT, gate):
    return pl.pallas_call(
        _out_kernel,
        grid=(BATCH, SEQ // TS),
        in_specs=[
            _tile_spec(GQA * HEAD_DIM),
            pl.BlockSpec((D_MODEL, D_MODEL), lambda b, s: (0, 0)),
            _fm_spec(D_MODEL), _bcol_spec(D_MODEL),
        ],
        out_specs=_fm_spec(D_MODEL),
        out_shape=jax.ShapeDtypeStruct((BATCH, D_MODEL, SEQ), F32),
        compiler_params=_cparams(("parallel", "parallel")),
        name="out_proj",
    )(oz, wT, xT, gate)


def _rope_tables(pos):
    inv_freq = ROPE_THETA ** (-jnp.arange(0, 2 * ROT_HALF, 2, dtype=F32) / (2 * ROT_HALF))
    ang = pos.astype(F32)[..., None] * inv_freq
    return jnp.cos(ang), jnp.sin(ang)


def _gate_perm():
    perm = np.full((N_KV * GATE_ROWS,), 3 * N_HEADS, dtype=np.int32)
    for g in range(N_KV):
        for br in range(3):
            for r in range(GQA):
                perm[g * GATE_ROWS + br * GQA + r] = (g * GQA + r) * 3 + br
    return perm


def _sel_weights_T():
    cs = np.arange(N_CMP)[:, None] * CMP_STRIDE
    ss = np.arange(N_SEL)[None, :] * SEL_BLOCK
    shared = np.clip(np.minimum(cs + CMP_LEN, ss + SEL_BLOCK) - np.maximum(cs, ss), 0, None)
    w = np.zeros((N_CMP_PAD, N_SEL), np.float32)
    w[:N_CMP] = shared / CMP_LEN
    return jnp.asarray(w.T, dtype=BF16)


def _col(v):
    return v.reshape(-1, 1)


def _nsa_layer(xT, ng, sc, sh, gate, cosT, sinT, cosc, sinc, w_in, w_out, q_g, k_g,
               cmp_pe, cmp_w1, cmp_b1, cmp_w2, gate_b):
    widths = [1024] + [256] * 6 + [3 * N_HEADS, 1024]
    q, kc, vc, ks, vs, kw, vw, gl, z = jnp.split(w_in, np.cumsum(widths)[:-1].tolist(), axis=1)
    perm = _gate_perm()
    gl_p = jnp.concatenate([gl, jnp.zeros((D_MODEL, 1), F32)], axis=1)[:, perm]
    gb_p = jnp.concatenate([gate_b, jnp.zeros((1,), F32)])[perm]
    wT = jnp.concatenate([q, ks, kw, kc, vc, vs, vw, gl_p, z], axis=1).T.astype(BF16)

    qT, ks_c, kw_c, kc_c, vc_c, vs_c, vw_c, gates, sz = _nsa_in_call(
        xT, ng, sc, sh, wT, _col(q_g), _col(k_g[1]), _col(k_g[2]), _col(gb_p), cosT, sinT)

    half_blocks = lambda t: t.reshape(BATCH * N_KV, N_CMP_PAD, CMP_FEAT)
    k_cmp = _compress_call(half_blocks(kc_c), cmp_pe[0], cmp_w1[0], cmp_b1[0], cmp_w2[0],
                           key_extras=(_col(k_g[0]), cosc, sinc))
    v_cmpT = _compress_call(half_blocks(vc_c), cmp_pe[1], cmp_w1[1], cmp_b1[1], cmp_w2[1])
    k_cmp = k_cmp.reshape(BATCH, N_KV, N_CMP_PAD, HEAD_DIM)
    v_cmpT = v_cmpT.reshape(BATCH, N_KV, HEAD_DIM, N_CMP_PAD)

    oz = _nsa_attn_call(qT, k_cmp, v_cmpT, ks_c, vs_c, kw_c, vw_c, gates, sz, _sel_weights_T())
    return _out_call(oz, w_out.T.astype(BF16), xT, gate)


def _moba_layer(xT, ng, sc, sh, gate, cosT, sinT, w_in, w_out, q_g, k_g):
    wT = w_in.T.astype(BF16)
    qT, k_c, km, v_c, sz = _moba_in_call(xT, ng, sc, sh, wT, _col(q_g), _col(k_g), cosT, sinT)
    nb = TS // TQ
    km = km[..., :nb].reshape(BATCH, SEQ // TS, N_KV, HEAD_DIM, nb)
    km = km.transpose(0, 2, 1, 4, 3).reshape(BATCH, N_KV, N_CHUNK, HEAD_DIM).astype(BF16)
    oz = _moba_attn_call(qT, km, k_c, v_c, sz)
    return _out_call(oz, w_out.T.astype(BF16), xT, gate)


@jax.jit
def _forward(x, c, positions, norm_g, ada_w, ada_b, nsa_w_in, nsa_w_out, nsa_q_norm, nsa_k_norm,
             nsa_cmp_pe, nsa_cmp_w1, nsa_cmp_b1, nsa_cmp_w2, nsa_gate_b,
             moba_w_in, moba_w_out, moba_q_norm, moba_k_norm):
    cos, sin = _rope_tables(positions)
    cosT = cos.transpose(0, 2, 1)
    sinT = sin.transpose(0, 2, 1)
    cmp_end = np.minimum(np.arange(N_CMP_PAD) * CMP_STRIDE + CMP_LEN - 1, SEQ - 1)
    cos_c, sin_c = _rope_tables(positions[:, cmp_end])

    def cmp_table(t):
        t = jnp.broadcast_to(t.transpose(2, 0, 1)[:, :, None, :],
                             (ROT_HALF, BATCH, N_KV, N_CMP_PAD))
        return t.reshape(ROT_HALF, CMP_COLS)

    cosc = cmp_table(cos_c)
    sinc = cmp_table(sin_c)

    mod = _ada_call(c, ada_w, ada_b)
    shift, scale, gate = jnp.split(mod[..., None], 3, axis=2)

    xT = x.transpose(0, 2, 1)
    for i in range(DEPTH):
        j = i // 2
        ng = _col(norm_g[i])
        if i % 2 == 0:
            xT = _nsa_layer(xT, ng, scale[i], shift[i], gate[i], cosT, sinT, cosc, sinc,
                            nsa_w_in[j], nsa_w_out[j], nsa_q_norm[j], nsa_k_norm[j],
                            nsa_cmp_pe[j], nsa_cmp_w1[j], nsa_cmp_b1[j], nsa_cmp_w2[j],
                            nsa_gate_b[j])
        else:
            xT = _moba_layer(xT, ng, scale[i], shift[i], gate[i], cosT, sinT,
                             moba_w_in[j], moba_w_out[j], moba_q_norm[j], moba_k_norm[j])
    return xT.transpose(0, 2, 1)


def kernel(x, c, positions, norm_g, ada_w, ada_b, nsa_w_in, nsa_w_out, nsa_q_norm, nsa_k_norm, nsa_cmp_pe, nsa_cmp_w1, nsa_cmp_b1, nsa_cmp_w2, nsa_gate_b, moba_w_in, moba_w_out, moba_q_norm, moba_k_norm):
    return _forward(x, c, positions, norm_g, ada_w, ada_b, nsa_w_in, nsa_w_out, nsa_q_norm,
                    nsa_k_norm, nsa_cmp_pe, nsa_cmp_w1, nsa_cmp_b1, nsa_cmp_w2, nsa_gate_b,
                    moba_w_in, moba_w_out, moba_q_norm, moba_k_norm)
```

```python
import functools

import numpy as np
import jax
import jax.numpy as jnp
from jax import lax
from jax.experimental import pallas as pl
from jax.experimental.pallas import tpu as pltpu

D_MODEL = 1024
BATCH = 16
SEQ = 2048
DEPTH = 4
HEAD_DIM = 64
N_HEADS = 16
N_KV = 4
GQA = 4
ROT_HALF = 8
ROPE_THETA = 500000.0
NORM_EPS = 1e-6
CMP_LEN = 32
CMP_STRIDE = 16
CMP_HID = 256
SEL_BLOCK = 64
SEL_COUNT = 8
N_SEL = SEQ // SEL_BLOCK
N_CMP = (SEQ - CMP_LEN) // CMP_STRIDE + 1
N_CMP_PAD = 128
MOBA_TOPK = 3

TQ = 256
N_CHUNK = SEQ // TQ
LANES_Q = GQA * TQ
TS = 512
NEG = -1e30
BIAS_ROWS = 16
K_AUG = HEAD_DIM + BIAS_ROWS
V_AUG = HEAD_DIM + 16
PV_COL = 256
Q_SCALE = HEAD_DIM ** -0.5 * float(np.log2(np.e))
GATE_ROWS = 16

F32 = jnp.float32
BF16 = jnp.bfloat16

VMEM_LIMIT = 52 * 1024 * 1024


def _cparams(sem):
    return pltpu.CompilerParams(dimension_semantics=sem, vmem_limit_bytes=VMEM_LIMIT)


def _dot(a, b):
    return jnp.dot(a, b, preferred_element_type=F32)


def _ada_kernel(c_ref, w_ref, b_ref, o_ref):
    cond = c_ref[...]
    cond = cond * jax.nn.sigmoid(cond)
    o_ref[0] = jnp.dot(cond, w_ref[0], precision=lax.Precision.HIGHEST,
                       preferred_element_type=F32) + b_ref[0]


def _ada_call(c, ada_w, ada_b):
    nt = 1024
    return pl.pallas_call(
        _ada_kernel,
        grid=(DEPTH, 3 * D_MODEL // nt),
        in_specs=[
            pl.BlockSpec((BATCH, D_MODEL), lambda i, n: (0, 0)),
            pl.BlockSpec((1, D_MODEL, nt), lambda i, n: (i, 0, n)),
            pl.BlockSpec((1, 1, nt), lambda i, n: (i, 0, n)),
        ],
        out_specs=pl.BlockSpec((1, BATCH, nt), lambda i, n: (i, 0, n)),
        out_shape=jax.ShapeDtypeStruct((DEPTH, BATCH, 3 * D_MODEL), F32),
        compiler_params=_cparams(("parallel", "parallel")),
        name="ada_mod",
    )(c, ada_w, ada_b.reshape(DEPTH, 1, 3 * D_MODEL))


def _norm_mod(x_ref, ng_ref, sc_ref, sh_ref):
    x = x_ref[0]
    ms = jnp.mean(x * x, axis=0, keepdims=True)
    y = x * lax.rsqrt(ms + NORM_EPS)
    h = (y * ng_ref[...]) * (1.0 + sc_ref[0]) + sh_ref[0]
    return h.astype(BF16)


def _head_norm_rope(y, g, cos, sin):
    ms = jnp.mean(y * y, axis=0, keepdims=True)
    yn = (y * lax.rsqrt(ms + NORM_EPS)) * g
    x1 = yn[0:ROT_HALF]
    x2 = yn[ROT_HALF:2 * ROT_HALF]
    return jnp.concatenate([x1 * cos - x2 * sin, x2 * cos + x1 * sin, yn[2 * ROT_HALF:]], axis=0)


def _store_k_chunks(o_ref, gi, yh, ind=None):
    for it in range(TS // TQ):
        chunk = yh[:, it * TQ:(it + 1) * TQ]
        if ind is None:
            o_ref[0, gi, it] = chunk.T.astype(BF16)
        else:
            aug = jnp.concatenate([chunk, ind], axis=0).T
            o_ref[0, gi, it] = aug[:, :K_AUG].astype(BF16)


def _store_tiles(o_ref, gi, rows, y):
    for it in range(TS // TQ):
        o_ref[0, gi, it, rows, :] = y[:, it * TQ:(it + 1) * TQ]


def _project_queries(proj, off, qg_ref, cos, sin, q_ref):
    for gi in range(N_KV):
        y = proj(off + gi * 256, 256)
        for r in range(GQA):
            rows = slice(r * HEAD_DIM, (r + 1) * HEAD_DIM)
            yh = _head_norm_rope(y[rows], qg_ref[...], cos, sin)
            _store_tiles(q_ref, gi, rows, (yh * Q_SCALE).astype(BF16))


def _project_silu(proj, off, sz_ref):
    for gi in range(N_KV):
        z = proj(off + gi * 256, 256)
        _store_tiles(sz_ref, gi, slice(None), z * jax.nn.sigmoid(z))


def _block_indicator(blocks):
    row = lax.broadcasted_iota(jnp.int32, (HEAD_DIM, TQ), 0)
    lane = lax.broadcasted_iota(jnp.int32, (HEAD_DIM, TQ), 1)
    return (lane // (TQ // blocks) == row).astype(F32)


def _store_v_chunks(o_ref, y):
    yb = y.astype(BF16)
    row = lax.broadcasted_iota(jnp.int32, (V_AUG - HEAD_DIM, TQ), 0)
    ones_row = (row == 0).astype(BF16)
    for gi in range(N_KV):
        for it in range(TS // TQ):
            v = yb[gi * HEAD_DIM:(gi + 1) * HEAD_DIM, it * TQ:(it + 1) * TQ]
            o_ref[0, gi, it] = jnp.concatenate([v, ones_row], axis=0)


NSA_OFF_Q = 0
NSA_OFF_KS = 1024
NSA_OFF_KW = 1280
NSA_OFF_KC = 1536
NSA_OFF_VC = 1792
NSA_OFF_VS = 2048
NSA_OFF_VW = 2304
NSA_OFF_GT = 2560
NSA_OFF_Z = NSA_OFF_GT + N_KV * GATE_ROWS
NSA_ROWS = NSA_OFF_Z + D_MODEL


def _nsa_in_kernel(x_ref, ng_ref, sc_ref, sh_ref, w_ref, qg_ref, ksg_ref, kwg_ref, gb_ref,
                   cos_ref, sin_ref,
                   q_ref, ks_ref, kw_ref, kc_ref, vc_ref, vs_ref, vw_ref, gt_ref, sz_ref):
    hb = _norm_mod(x_ref, ng_ref, sc_ref, sh_ref)
    cos = cos_ref[0]
    sin = sin_ref[0]

    def proj(r0, n):
        return _dot(w_ref[r0:r0 + n, :], hb)

    _project_queries(proj, NSA_OFF_Q, qg_ref, cos, sin, q_ref)
    sel_ind = _block_indicator(TQ // SEL_BLOCK)
    for off, g_ref, o_ref, ind in ((NSA_OFF_KS, ksg_ref, ks_ref, sel_ind),
                                   (NSA_OFF_KW, kwg_ref, kw_ref, None)):
        y = proj(off, 256)
        for gi in range(N_KV):
            yh = _head_norm_rope(y[gi * HEAD_DIM:(gi + 1) * HEAD_DIM], g_ref[...], cos, sin)
            _store_k_chunks(o_ref, gi, yh, ind)
    for off, o_ref in ((NSA_OFF_KC, kc_ref), (NSA_OFF_VC, vc_ref)):
        y = proj(off, 256)
        for gi in range(N_KV):
            for it in range(TS // TQ):
                o_ref[0, gi, it] = y[gi * HEAD_DIM:(gi + 1) * HEAD_DIM, it * TQ:(it + 1) * TQ].T
    _store_v_chunks(vs_ref, proj(NSA_OFF_VS, 256))
    _store_v_chunks(vw_ref, proj(NSA_OFF_VW, 256))
    gates = jax.nn.sigmoid(proj(NSA_OFF_GT, N_KV * GATE_ROWS) + gb_ref[...])
    for gi in range(N_KV):
        _store_tiles(gt_ref, gi, slice(None), gates[gi * GATE_ROWS:(gi + 1) * GATE_ROWS])
    _project_silu(proj, NSA_OFF_Z, sz_ref)


def _col_spec(rows):
    return pl.BlockSpec((rows, 1), lambda b, s: (0, 0))


def _bcol_spec(rows):
    return pl.BlockSpec((1, rows, 1), lambda b, s: (b, 0, 0))


def _fm_spec(rows):
    return pl.BlockSpec((1, rows, TS), lambda b, s: (b, 0, s))


def _k_chunk_spec(width):
    return pl.BlockSpec((1, N_KV, TS // TQ, TQ, width), lambda b, s: (b, 0, s, 0, 0))


def _k_chunk_shape(width, dtype=BF16):
    return jax.ShapeDtypeStruct((BATCH, N_KV, N_CHUNK, TQ, width), dtype)


def _tile_spec(rows):
    return pl.BlockSpec((1, N_KV, TS // TQ, rows, TQ), lambda b, s: (b, 0, s, 0, 0))


def _tile_shape(rows, dtype):
    return jax.ShapeDtypeStruct((BATCH, N_KV, N_CHUNK, rows, TQ), dtype)


_V_CHUNK_SPEC = pl.BlockSpec((1, N_KV, TS // TQ, V_AUG, TQ), lambda b, s: (b, 0, s, 0, 0))
_V_CHUNK_SHAPE = jax.ShapeDtypeStruct((BATCH, N_KV, N_CHUNK, V_AUG, TQ), BF16)


def _nsa_in_call(xT, ng, sc, sh, wT, qg, ksg, kwg, gb, cosT, sinT):
    fm = lambda rows, dt: jax.ShapeDtypeStruct((BATCH, rows, SEQ), dt)
    return pl.pallas_call(
        _nsa_in_kernel,
        grid=(BATCH, SEQ // TS),
        in_specs=[
            _fm_spec(D_MODEL), _col_spec(D_MODEL), _bcol_spec(D_MODEL), _bcol_spec(D_MODEL),
            pl.BlockSpec((NSA_ROWS, D_MODEL), lambda b, s: (0, 0)),
            _col_spec(HEAD_DIM), _col_spec(HEAD_DIM), _col_spec(HEAD_DIM),
            _col_spec(N_KV * GATE_ROWS),
            _fm_spec(ROT_HALF), _fm_spec(ROT_HALF),
        ],
        out_specs=[
            _tile_spec(GQA * HEAD_DIM), _k_chunk_spec(K_AUG), _k_chunk_spec(HEAD_DIM),
            _k_chunk_spec(HEAD_DIM), _k_chunk_spec(HEAD_DIM),
            _V_CHUNK_SPEC, _V_CHUNK_SPEC, _tile_spec(GATE_ROWS), _tile_spec(GQA * HEAD_DIM),
        ],
        out_shape=[
            _tile_shape(GQA * HEAD_DIM, BF16), _k_chunk_shape(K_AUG), _k_chunk_shape(HEAD_DIM),
            _k_chunk_shape(HEAD_DIM, F32), _k_chunk_shape(HEAD_DIM, F32),
            _V_CHUNK_SHAPE, _V_CHUNK_SHAPE, _tile_shape(GATE_ROWS, F32),
            _tile_shape(GQA * HEAD_DIM, F32),
        ],
        compiler_params=_cparams(("parallel", "parallel")),
        name="nsa_in_proj",
    )(xT, ng, sc, sh, wT, qg, ksg, kwg, gb, cosT, sinT)


MOBA_OFF_Q = 0
MOBA_OFF_K = 1024
MOBA_OFF_V = 1280
MOBA_OFF_Z = 1536
MOBA_ROWS = 2560
KM_LANES = 128


def _moba_in_kernel(x_ref, ng_ref, sc_ref, sh_ref, w_ref, qg_ref, kg_ref, cos_ref, sin_ref,
                    q_ref, k_ref, km_ref, v_ref, sz_ref):
    hb = _norm_mod(x_ref, ng_ref, sc_ref, sh_ref)
    cos = cos_ref[0]
    sin = sin_ref[0]

    def proj(r0, n):
        return _dot(w_ref[r0:r0 + n, :], hb)

    _project_queries(proj, MOBA_OFF_Q, qg_ref, cos, sin, q_ref)
    y = proj(MOBA_OFF_K, 256)
    lane = lax.broadcasted_iota(jnp.int32, (HEAD_DIM, KM_LANES), 1)
    for gi in range(N_KV):
        yh = _head_norm_rope(y[gi * HEAD_DIM:(gi + 1) * HEAD_DIM], kg_ref[...], cos, sin)
        _store_k_chunks(k_ref, gi, yh, _block_indicator(1))
        km = jnp.zeros((HEAD_DIM, KM_LANES), F32)
        for it in range(TS // TQ):
            mean = jnp.mean(yh[:, it * TQ:(it + 1) * TQ], axis=1, keepdims=True)
            km = jnp.where(lane == it, mean, km)
        km_ref[0, 0, gi * HEAD_DIM:(gi + 1) * HEAD_DIM, :] = km
    _store_v_chunks(v_ref, proj(MOBA_OFF_V, 256))
    _project_silu(proj, MOBA_OFF_Z, sz_ref)


def _moba_in_call(xT, ng, sc, sh, wT, qg, kg, cosT, sinT):
    fm = lambda rows, dt: jax.ShapeDtypeStruct((BATCH, rows, SEQ), dt)
    return pl.pallas_call(
        _moba_in_kernel,
        grid=(BATCH, SEQ // TS),
        in_specs=[
            _fm_spec(D_MODEL), _col_spec(D_MODEL), _bcol_spec(D_MODEL), _bcol_spec(D_MODEL),
            pl.BlockSpec((MOBA_ROWS, D_MODEL), lambda b, s: (0, 0)),
            _col_spec(HEAD_DIM), _col_spec(HEAD_DIM),
            _fm_spec(ROT_HALF), _fm_spec(ROT_HALF),
        ],
        out_specs=[
            _tile_spec(GQA * HEAD_DIM), _k_chunk_spec(K_AUG),
            pl.BlockSpec((1, 1, N_KV * HEAD_DIM, KM_LANES), lambda b, s: (b, s, 0, 0)),
            _V_CHUNK_SPEC, _tile_spec(GQA * HEAD_DIM),
        ],
        out_shape=[
            _tile_shape(GQA * HEAD_DIM, BF16), _k_chunk_shape(K_AUG),
            jax.ShapeDtypeStruct((BATCH, SEQ // TS, N_KV * HEAD_DIM, KM_LANES), F32),
            _V_CHUNK_SHAPE, _tile_shape(GQA * HEAD_DIM, F32),
        ],
        compiler_params=_cparams(("parallel", "parallel")),
        name="moba_in_proj",
    )(xT, ng, sc, sh, wT, qg, kg, cosT, sinT)


CMP_NB = 8
CMP_ROWS = CMP_NB * N_CMP_PAD
CMP_COLS = BATCH * N_KV * N_CMP_PAD
CMP_FEAT = CMP_STRIDE * HEAD_DIM


def _compress_mlp(x_ref, pea_ref, peb_ref, w1a_ref, w1b_ref, b1_ref, w2t_ref):
    x = x_ref[...].reshape(CMP_ROWS, CMP_FEAT)
    u = _dot((x + pea_ref[...]).astype(BF16), w1a_ref[...])
    v = _dot((x + peb_ref[...]).astype(BF16), w1b_ref[...])
    v = pltpu.roll(v, CMP_ROWS - 1, 0)
    h = u + v + b1_ref[...]
    h = 0.5 * h * (1.0 + jnp.tanh(np.sqrt(2.0 / np.pi) * (h + 0.044715 * (h * h * h))))
    return lax.dot_general(w2t_ref[...], h.astype(BF16), (((1,), (1,)), ((), ())),
                           preferred_element_type=F32)


def _compress_key_kernel(x_ref, pea_ref, peb_ref, w1a_ref, w1b_ref, b1_ref, w2t_ref, kg_ref,
                         cos_ref, sin_ref, o_ref):
    out = _compress_mlp(x_ref, pea_ref, peb_ref, w1a_ref, w1b_ref, b1_ref, w2t_ref)
    out = _head_norm_rope(out, kg_ref[...], cos_ref[...], sin_ref[...])
    for i in range(CMP_NB):
        o_ref[i] = out[:, i * N_CMP_PAD:(i + 1) * N_CMP_PAD].T.astype(BF16)


def _compress_value_kernel(x_ref, pea_ref, peb_ref, w1a_ref, w1b_ref, b1_ref, w2t_ref, o_ref):
    out = _compress_mlp(x_ref, pea_ref, peb_ref, w1a_ref, w1b_ref, b1_ref, w2t_ref)
    for i in range(CMP_NB):
        o_ref[i] = out[:, i * N_CMP_PAD:(i + 1) * N_CMP_PAD].astype(BF16)


def _compress_call(x, pe, w1, b1, w2, key_extras=None):
    full = lambda shape: pl.BlockSpec(shape, lambda t: (0,) * len(shape))
    in_specs = [
        pl.BlockSpec((CMP_NB, N_CMP_PAD, CMP_FEAT), lambda t: (t, 0, 0)),
        full((1, CMP_FEAT)), full((1, CMP_FEAT)),
        full((CMP_FEAT, CMP_HID)), full((CMP_FEAT, CMP_HID)),
        full((1, CMP_HID)), full((HEAD_DIM, CMP_HID)),
    ]
    pe = pe.reshape(2, 1, CMP_FEAT)
    w1 = w1.astype(BF16)
    args = [x, pe[0], pe[1], w1[:CMP_FEAT], w1[CMP_FEAT:], b1.reshape(1, CMP_HID), w2.T.astype(BF16)]
    if key_extras is not None:
        in_specs += [full((HEAD_DIM, 1)),
                     pl.BlockSpec((ROT_HALF, CMP_ROWS), lambda t: (0, t)),
                     pl.BlockSpec((ROT_HALF, CMP_ROWS), lambda t: (0, t))]
        args += list(key_extras)
        kernel, out_tail = _compress_key_kernel, (N_CMP_PAD, HEAD_DIM)
    else:
        kernel, out_tail = _compress_value_kernel, (HEAD_DIM, N_CMP_PAD)
    return pl.pallas_call(
        kernel,
        grid=(BATCH * N_KV // CMP_NB,),
        in_specs=in_specs,
        out_specs=pl.BlockSpec((CMP_NB,) + out_tail, lambda t: (t, 0, 0)),
        out_shape=jax.ShapeDtypeStruct((BATCH * N_KV,) + out_tail, BF16),
        compiler_params=_cparams(("parallel",)),
        name="nsa_compress_key" if key_extras is not None else "nsa_compress_value",
    )(*args)


def _group_queries(q_ref):
    return jnp.concatenate(
        [q_ref[0, r * HEAD_DIM:(r + 1) * HEAD_DIM, :] for r in range(GQA)], axis=1)


COL = 128


def _col_tiles():
    per_head = TQ // COL
    for ct in range(LANES_Q // COL):
        r, h = divmod(ct, per_head)
        yield r, slice(h * COL, (h + 1) * COL), slice(ct * COL, (ct + 1) * COL)


def _flash_scratch():
    return [
        pltpu.VMEM((2, TQ, LANES_Q), BF16),
        pltpu.VMEM((2, 1, LANES_Q), F32),
        pltpu.VMEM((1, LANES_Q), F32),
        pltpu.VMEM((V_AUG, LANES_Q), F32),
    ]


class _FlashBranch:
    def __init__(self, q_ref, k_ref, v_ref, bufs, own, past_chunk, own_fix=None, past_fix=None,
                 own_rows=None, past_rows=None):
        self.q_ref, self.k_ref, self.v_ref, self.bufs = q_ref, k_ref, v_ref, bufs
        self.own_chunk, self.past_chunk = own, past_chunk
        self.own_fix, self.past_fix, self.own_rows, self.past_rows = own_fix, past_fix, own_rows, past_rows

    def _softmax_tile(self, s, cs, slot, init):
        p_buf, a_buf, m_ref, _ = self.bufs
        m_loc = jnp.max(s, axis=0, keepdims=True)
        if init:
            m_new = m_loc
            a_buf[slot, :, cs] = jnp.ones((1, COL), F32)
        else:
            m_old = m_ref[:, cs]
            m_new = jnp.maximum(m_old, m_loc)
            a_buf[slot, :, cs] = jnp.exp2(m_old - m_new)
        m_ref[:, cs] = m_new
        p_buf[slot, :, cs] = jnp.exp2(s - m_new).astype(BF16)

    def _scores(self, k, r, qs, cs, rows, fix):
        q = self.q_ref[0, r * HEAD_DIM:(r + 1) * HEAD_DIM, qs]
        if rows is not None:
            q = jnp.concatenate([q, rows(cs)], axis=0)
        s = _dot(k, q)
        return s if fix is None else fix(s, qs)

    def _pv_slab(self, vT, slot, ct):
        if (ct * COL) % PV_COL:
            return
        p_buf, a_buf, _, acc_ref = self.bufs
        cs = slice(ct * COL, ct * COL + PV_COL)
        acc_ref[:, cs] = a_buf[slot, :, cs] * acc_ref[:, cs] + _dot(vT, p_buf[slot, :, cs])

    def own(self):
        acc_ref = self.bufs[3]
        acc_ref[...] = jnp.zeros(acc_ref.shape, F32)
        k_own = self.k_ref[0, 0, self.own_chunk]
        for r, qs, cs in _col_tiles():
            self._softmax_tile(self._scores(k_own, r, qs, cs, self.own_rows, self.own_fix), cs, 0, True)

    def trip(self, j, cur, prv):
        k = self.k_ref[0, 0, self.past_chunk(j)]
        vT = self.v_ref[0, 0, self._chunk_before(j)]
        rows = None if self.past_rows is None else self.past_rows(j)
        for ct, (r, qs, cs) in enumerate(_col_tiles()):
            s = self._scores(k, r, qs, cs, rows, self.past_fix)
            self._pv_slab(vT, prv, ct)
            self._softmax_tile(s, cs, cur, False)

    def run_past(self, n_past):
        if isinstance(n_past, int):
            for j in range(n_past):
                self.trip(j, 1 - j % 2, j % 2)
            return

        def pair(pp, carry):
            self.trip(2 * pp, 1, 0)
            self.trip(2 * pp + 1, 0, 1)
            return carry

        lax.fori_loop(0, n_past // 2, pair, 0)

        @pl.when(n_past % 2 == 1)
        def _():
            self.trip(n_past - 1, 1, 0)

    def _chunk_before(self, j):
        if isinstance(j, int):
            return self.own_chunk if j == 0 else self.past_chunk(j - 1)
        return jnp.where(j == 0, self.own_chunk, self.past_chunk(jnp.maximum(j - 1, 0)))

    def finish(self, n_past):
        vT = self.v_ref[0, 0, self._chunk_before(n_past)]
        for ct in range(LANES_Q // COL):
            self._pv_slab(vT, n_past % 2, ct)
        acc = self.bufs[3][...]
        return acc[:HEAD_DIM] / acc[HEAD_DIM:HEAD_DIM + 1]


def _flash_branch(q_ref, k_ref, v_ref, bufs, own, n_past, past_chunk, **kw):
    branch = _FlashBranch(q_ref, k_ref, v_ref, bufs, own, past_chunk, **kw)
    branch.own()
    branch.run_past(n_past)
    return branch.finish(n_past)


def _causal_fix(keep_lower):
    a_idx = lax.broadcasted_iota(jnp.int32, (TQ, COL), 0)
    lane = lax.broadcasted_iota(jnp.int32, (TQ, COL), 1)

    def fix(s, qs):
        lower = a_idx <= lane + qs.start
        return jnp.where(lower if keep_lower else ~lower, s, NEG)

    return fix


def _rank_select(score_ref, n_rows, j_idx, count):
    score = score_ref[...]
    cnt = jnp.zeros(score.shape, jnp.int32)
    for jp in range(n_rows):
        row = score_ref[jp:jp + 1, :]
        beats = (row > score) | ((row == score) & (jp < j_idx))
        cnt = cnt + beats.astype(jnp.int32)
    return cnt < count


class _TileView:
    def __init__(self, ref, tile):
        self.ref, self.tile = ref, tile

    def __getitem__(self, idx):
        return self.ref[(0, 0, self.tile) + tuple(idx[1:])]

    def __setitem__(self, idx, value):
        self.ref[(0, 0, self.tile) + tuple(idx[1:])] = value


def _for_each_query_tile(tile_fn, tiled_refs, other_refs):
    for qi in range(N_CHUNK):
        tile_fn(qi, *[_TileView(ref, qi) for ref in tiled_refs], *other_refs)


def _store_out(o_ref, sz_ref, o):
    for r in range(GQA):
        rows = slice(r * HEAD_DIM, (r + 1) * HEAD_DIM)
        o_ref[0, rows, :] = (o[:, r * TQ:(r + 1) * TQ] * sz_ref[0, rows, :]).astype(BF16)


def _nsa_attn_kernel(q_ref, kc_ref, vc_ref, ks_ref, vs_ref, kw_ref, vw_ref, gt_ref, sz_ref,
                     selw_ref, o_ref, score_ref, bias_ref, *bufs):
    _for_each_query_tile(_nsa_attn_tile, (q_ref, gt_ref, sz_ref, o_ref),
                         (kc_ref, vc_ref, ks_ref, vs_ref, kw_ref, vw_ref, selw_ref,
                          score_ref, bias_ref) + tuple(bufs))


def _nsa_attn_tile(qi, q_ref, gt_ref, sz_ref, o_ref, kc_ref, vc_ref, ks_ref, vs_ref, kw_ref, vw_ref,
                   selw_ref, score_ref, bias_ref, *bufs):
    q4 = _group_queries(q_ref)

    sel_bufs, win_bufs = bufs[:len(bufs) // 2], bufs[len(bufs) // 2:]
    keep_lower = _causal_fix(True)

    n_win = min(qi, 1)
    win = _FlashBranch(q_ref, kw_ref, vw_ref, win_bufs, qi, lambda j: qi - 1,
                       own_fix=keep_lower, past_fix=_causal_fix(False))

    t = qi * TQ + (lax.broadcasted_iota(jnp.int32, (1, LANES_Q), 1) & (TQ - 1))
    n_idx = lax.broadcasted_iota(jnp.int32, (N_CMP_PAD, LANES_Q), 0)
    s_cmp = _dot(kc_ref[0, 0], q4)
    win.own()
    s = jnp.where(n_idx * CMP_STRIDE + (CMP_LEN - 1) <= t, s_cmp, NEG)
    m = jnp.max(s, axis=0, keepdims=True)
    p = jnp.exp2(s - m) * (m > 0.5 * NEG).astype(F32)
    p = p / jnp.maximum(jnp.sum(p, axis=0, keepdims=True), 1e-30)
    o_cmp = _dot(vc_ref[0, 0], p.astype(BF16))

    psum = p[:, 0:TQ]
    for r in range(1, GQA):
        psum = psum + p[:, r * TQ:(r + 1) * TQ]
    p_hi = psum.astype(BF16)
    p_lo = (psum - p_hi.astype(F32)).astype(BF16)
    imp = _dot(selw_ref[...], p_hi) + _dot(selw_ref[...], p_lo)
    j_idx = lax.broadcasted_iota(jnp.int32, (N_SEL, TQ), 0)
    cur = qi * (TQ // SEL_BLOCK) + (lax.broadcasted_iota(jnp.int32, (N_SEL, TQ), 1) >> 6)
    valid = j_idx <= cur
    forced = (j_idx == 0) | (j_idx == cur) | (j_idx == cur - 1)
    score_ref[...] = jnp.where(forced, jnp.inf, jnp.where(valid, imp, -jnp.inf))
    win.run_past(n_win)
    sel = valid & _rank_select(score_ref, N_SEL, j_idx, SEL_COUNT)
    bias = jnp.where(sel, 0.0, NEG)
    bias = jnp.concatenate([bias] * GQA, axis=1)
    per_chunk = TQ // SEL_BLOCK
    bias_ref[...] = jnp.zeros(bias_ref.shape, F32)
    for jp in range(N_SEL):
        bias_ref[jp // per_chunk, jp % per_chunk:jp % per_chunk + 1, :] = bias[jp:jp + 1, :]

    o_win = win.finish(n_win)

    def sel_rows(j):
        return lambda cs: bias_ref[j, :, cs].astype(BF16)

    o_slc = _flash_branch(q_ref, ks_ref, vs_ref, sel_bufs, qi, qi, lambda j: j, own_fix=keep_lower,
                          own_rows=sel_rows(qi), past_rows=sel_rows)

    def gate(br):
        return jnp.concatenate(
            [gt_ref[0, br * GQA + r:br * GQA + r + 1, :] for r in range(GQA)], axis=1)

    o = gate(0) * o_cmp + gate(1) * o_slc + gate(2) * o_win
    _store_out(o_ref, sz_ref, o)


def _group_spec(*tail):
    return pl.BlockSpec((1, 1) + tail, lambda b, g: (b, g) + (0,) * len(tail))


def _attn_specs():
    q_spec = _group_spec(N_CHUNK, GQA * HEAD_DIM, TQ)
    k_spec = lambda width: _group_spec(N_CHUNK, TQ, width)
    v_spec = _group_spec(N_CHUNK, V_AUG, TQ)
    return q_spec, k_spec, v_spec


def _nsa_attn_call(q, kc, vc, ks, vs, kw, vw, gt, sz, selw):
    q_spec, k_spec, v_spec = _attn_specs()
    return pl.pallas_call(
        _nsa_attn_kernel,
        grid=(BATCH, N_KV),
        in_specs=[
            q_spec,
            _group_spec(N_CMP_PAD, HEAD_DIM), _group_spec(HEAD_DIM, N_CMP_PAD),
            k_spec(K_AUG), v_spec, k_spec(HEAD_DIM), v_spec,
            _group_spec(N_CHUNK, GATE_ROWS, TQ),
            q_spec,
            pl.BlockSpec((N_SEL, N_CMP_PAD), lambda b, g: (0, 0)),
        ],
        out_specs=q_spec,
        out_shape=_tile_shape(GQA * HEAD_DIM, BF16),
        scratch_shapes=[
            pltpu.VMEM((N_SEL, TQ), F32),
            pltpu.VMEM((N_CHUNK, BIAS_ROWS, LANES_Q), F32),
        ] + _flash_scratch() + _flash_scratch(),
        compiler_params=_cparams(("parallel", "parallel")),
        name="nsa_attention",
    )(q, kc, vc, ks, vs, kw, vw, gt, sz, selw)


def _moba_attn_kernel(q_ref, km_ref, k_ref, v_ref, sz_ref, o_ref,
                      score_ref, bias_ref, *bufs):
    _for_each_query_tile(_moba_attn_tile, (q_ref, sz_ref, o_ref),
                         (km_ref, k_ref, v_ref, score_ref, bias_ref) + tuple(bufs))


def _moba_attn_tile(qi, q_ref, sz_ref, o_ref, km_ref, k_ref, v_ref, score_ref, bias_ref, *bufs):
    q4 = _group_queries(q_ref)

    j_idx = lax.broadcasted_iota(jnp.int32, (N_CHUNK, LANES_Q), 0)
    past = j_idx < qi
    score_ref[...] = jnp.where(past, _dot(km_ref[0, 0], q4), -jnp.inf)
    sel = past & _rank_select(score_ref, N_CHUNK, j_idx, MOBA_TOPK)
    bias = jnp.where(sel, 0.0, NEG)
    bias_ref[...] = jnp.zeros(bias_ref.shape, F32)
    for jp in range(N_CHUNK):
        bias_ref[jp, 0:1, :] = bias[jp:jp + 1, :]

    o = _flash_branch(q_ref, k_ref, v_ref, bufs, qi, qi, lambda j: j, own_fix=_causal_fix(True),
                      own_rows=lambda cs: jnp.zeros((BIAS_ROWS, COL), BF16),
                      past_rows=lambda j: (lambda cs: bias_ref[j, :, cs].astype(BF16)))
    _store_out(o_ref, sz_ref, o)


def _moba_attn_call(q, km, k, v, sz):
    q_spec, k_spec, v_spec = _attn_specs()
    return pl.pallas_call(
        _moba_attn_kernel,
        grid=(BATCH, N_KV),
        in_specs=[
            q_spec,
            _group_spec(N_CHUNK, HEAD_DIM),
            k_spec(K_AUG), v_spec, q_spec,
        ],
        out_specs=q_spec,
        out_shape=_tile_shape(GQA * HEAD_DIM, BF16),
        scratch_shapes=[
            pltpu.VMEM((N_CHUNK, LANES_Q), F32),
            pltpu.VMEM((N_CHUNK, BIAS_ROWS, LANES_Q), F32),
        ] + _flash_scratch(),
        compiler_params=_cparams(("parallel", "parallel")),
        name="moba_attention",
    )(q, km, k, v, sz)


def _out_kernel(oz_ref, w_ref, x_ref, gt_ref, o_ref):
    group_rows = GQA * HEAD_DIM
    for it in range(TS // TQ):
        lanes = slice(it * TQ, (it + 1) * TQ)
        for cblk in range(D_MODEL // 256):
            rows = slice(cblk * 256, (cblk + 1) * 256)
            y = _dot(w_ref[rows, 0:group_rows], oz_ref[0, 0, it])
            for gi in range(1, N_KV):
                y = y + _dot(w_ref[rows, gi * group_rows:(gi + 1) * group_rows], oz_ref[0, gi, it])
            o_ref[0, rows, lanes] = x_ref[0, rows, lanes] + gt_ref[0, rows, :] * y


def _out_call(oz, wT, xT, gate):
    return pl.pallas_call(
        _out_kernel,
        grid=(BATCH, SEQ // TS),
        in_specs=[
            _tile_spec(GQA * HEAD_DIM),
            pl.BlockSpec((D_MODEL, D_MODEL), lambda b, s: (0, 0)),
            _fm_spec(D_MODEL), _bcol_spec(D_MODEL),
        ],
        out_specs=_fm_spec(D_MODEL),
        out_shape=jax.ShapeDtypeStruct((BATCH, D_MODEL, SEQ), F32),
        compiler_params=_cparams(("parallel", "parallel")),
        name="out_proj",
    )(oz, wT, xT, gate)


def _rope_tables(pos):
    inv_freq = ROPE_THETA ** (-jnp.arange(0, 2 * ROT_HALF, 2, dtype=F32) / (2 * ROT_HALF))
    ang = pos.astype(F32)[..., None] * inv_freq
    return jnp.cos(ang), jnp.sin(ang)


def _gate_perm():
    perm = np.full((N_KV * GATE_ROWS,), 3 * N_HEADS, dtype=np.int32)
    for g in range(N_KV):
        for br in range(3):
            for r in range(GQA):
                perm[g * GATE_ROWS + br * GQA + r] = (g * GQA + r) * 3 + br
    return perm


def _sel_weights_T():
    cs = np.arange(N_CMP)[:, None] * CMP_STRIDE
    ss = np.arange(N_SEL)[None, :] * SEL_BLOCK
    shared = np.clip(np.minimum(cs + CMP_LEN, ss + SEL_BLOCK) - np.maximum(cs, ss), 0, None)
    w = np.zeros((N_CMP_PAD, N_SEL), np.float32)
    w[:N_CMP] = shared / CMP_LEN
    return jnp.asarray(w.T, dtype=BF16)


def _col(v):
    return v.reshape(-1, 1)


def _nsa_layer(xT, ng, sc, sh, gate, cosT, sinT, cosc, sinc, w_in, w_out, q_g, k_g,
               cmp_pe, cmp_w1, cmp_b1, cmp_w2, gate_b):
    widths = [1024] + [256] * 6 + [3 * N_HEADS, 1024]
    q, kc, vc, ks, vs, kw, vw, gl, z = jnp.split(w_in, np.cumsum(widths)[:-1].tolist(), axis=1)
    perm = _gate_perm()
    gl_p = jnp.concatenate([gl, jnp.zeros((D_MODEL, 1), F32)], axis=1)[:, perm]
    gb_p = jnp.concatenate([gate_b, jnp.zeros((1,), F32)])[perm]
    wT = jnp.concatenate([q, ks, kw, kc, vc, vs, vw, gl_p, z], axis=1).T.astype(BF16)

    qT, ks_c, kw_c, kc_c, vc_c, vs_c, vw_c, gates, sz = _nsa_in_call(
        xT, ng, sc, sh, wT, _col(q_g), _col(k_g[1]), _col(k_g[2]), _col(gb_p), cosT, sinT)

    half_blocks = lambda t: t.reshape(BATCH * N_KV, N_CMP_PAD, CMP_FEAT)
    k_cmp = _compress_call(half_blocks(kc_c), cmp_pe[0], cmp_w1[0], cmp_b1[0], cmp_w2[0],
                           key_extras=(_col(k_g[0]), cosc, sinc))
    v_cmpT = _compress_call(half_blocks(vc_c), cmp_pe[1], cmp_w1[1], cmp_b1[1], cmp_w2[1])
    k_cmp = k_cmp.reshape(BATCH, N_KV, N_CMP_PAD, HEAD_DIM)
    v_cmpT = v_cmpT.reshape(BATCH, N_KV, HEAD_DIM, N_CMP_PAD)

    oz = _nsa_attn_call(qT, k_cmp, v_cmpT, ks_c, vs_c, kw_c, vw_c, gates, sz, _sel_weights_T())
    return _out_call(oz, w_out.T.astype(BF16), xT, gate)


def _moba_layer(xT, ng, sc, sh, gate, cosT, sinT, w_in, w_out, q_g, k_g):
    wT = w_in.T.astype(BF16)
    qT, k_c, km, v_c, sz = _moba_in_call(xT, ng, sc, sh, wT, _col(q_g), _col(k_g), cosT, sinT)
    nb = TS // TQ
    km = km[..., :nb].reshape(BATCH, SEQ // TS, N_KV, HEAD_DIM, nb)
    km = km.transpose(0, 2, 1, 4, 3).reshape(BATCH, N_KV, N_CHUNK, HEAD_DIM).astype(BF16)
    oz = _moba_attn_call(qT, km, k_c, v_c, sz)
    return _out_call(oz, w_out.T.astype(BF16), xT, gate)


@jax.jit
def _forward(x, c, positions, norm_g, ada_w, ada_b, nsa_w_in, nsa_w_out, nsa_q_norm, nsa_k_norm,
             nsa_cmp_pe, nsa_cmp_w1, nsa_cmp_b1, nsa_cmp_w2, nsa_gate_b,
             moba_w_in, moba_w_out, moba_q_norm, moba_k_norm):
    cos, sin = _rope_tables(positions)
    cosT = cos.transpose(0, 2, 1)
    sinT = sin.transpose(0, 2, 1)
    cmp_end = np.minimum(np.arange(N_CMP_PAD) * CMP_STRIDE + CMP_LEN - 1, SEQ - 1)
    cos_c, sin_c = _rope_tables(positions[:, cmp_end])

    def cmp_table(t):
        t = jnp.broadcast_to(t.transpose(2, 0, 1)[:, :, None, :],
                             (ROT_HALF, BATCH, N_KV, N_CMP_PAD))
        return t.reshape(ROT_HALF, CMP_COLS)

    cosc = cmp_table(cos_c)
    sinc = cmp_table(sin_c)

    mod = _ada_call(c, ada_w, ada_b)
    shift, scale, gate = jnp.split(mod[..., None], 3, axis=2)

    xT = x.transpose(0, 2, 1)
    for i in range(DEPTH):
        j = i // 2
        ng = _col(norm_g[i])
        if i % 2 == 0:
            xT = _nsa_layer(xT, ng, scale[i], shift[i], gate[i], cosT, sinT, cosc, sinc,
                            nsa_w_in[j], nsa_w_out[j], nsa_q_norm[j], nsa_k_norm[j],
                            nsa_cmp_pe[j], nsa_cmp_w1[j], nsa_cmp_b1[j], nsa_cmp_w2[j],
                            nsa_gate_b[j])
        else:
            xT = _moba_layer(xT, ng, scale[i], shift[i], gate[i], cosT, sinT,
                             moba_w_in[j], moba_w_out[j], moba_q_norm[j], moba_k_norm[j])
    return xT.transpose(0, 2, 1)


def kernel(x, c, positions, norm_g, ada_w, ada_b, nsa_w_in, nsa_w_out, nsa_q_norm, nsa_k_norm, nsa_cmp_pe, nsa_cmp_w1, nsa_cmp_b1, nsa_cmp_w2, nsa_gate_b, moba_w_in, moba_w_out, moba_q_norm, moba_k_norm):
    return _forward(x, c, positions, norm_g, ada_w, ada_b, nsa_w_in, nsa_w_out, nsa_q_norm,
                    nsa_k_norm, nsa_cmp_pe, nsa_cmp_w1, nsa_cmp_b1, nsa_cmp_w2, nsa_gate_b,
                    moba_w_in, moba_w_out, moba_q_norm, moba_k_norm)
```

```python
import functools

import numpy as np
import jax
import jax.numpy as jnp
from jax import lax
from jax.experimental import pallas as pl
from jax.experimental.pallas import tpu as pltpu

D_MODEL = 1024
BATCH = 16
SEQ = 2048
DEPTH = 4
HEAD_DIM = 64
N_HEADS = 16
N_KV = 4
GQA = 4
ROT_HALF = 8
ROPE_THETA = 500000.0
NORM_EPS = 1e-6
CMP_LEN = 32
CMP_STRIDE = 16
CMP_HID = 256
SEL_BLOCK = 64
SEL_COUNT = 8
N_SEL = SEQ // SEL_BLOCK
N_CMP = (SEQ - CMP_LEN) // CMP_STRIDE + 1
N_CMP_PAD = 128
MOBA_TOPK = 3

TQ = 256
N_CHUNK = SEQ // TQ
LANES_Q = GQA * TQ
TS = 512
NEG = -1e30
BIAS_ROWS = 16
K_AUG = HEAD_DIM + BIAS_ROWS
V_AUG = HEAD_DIM + 16
PV_COL = 256
Q_SCALE = HEAD_DIM ** -0.5 * float(np.log2(np.e))
GATE_ROWS = 16

F32 = jnp.float32
BF16 = jnp.bfloat16

VMEM_LIMIT = 52 * 1024 * 1024


def _cparams(sem):
    return pltpu.CompilerParams(dimension_semantics=sem, vmem_limit_bytes=VMEM_LIMIT)


def _dot(a, b):
    return jnp.dot(a, b, preferred_element_type=F32)


def _ada_kernel(c_ref, w_ref, b_ref, o_ref):
    cond = c_ref[...]
    cond = cond * jax.nn.sigmoid(cond)
    o_ref[0] = jnp.dot(cond, w_ref[0], precision=lax.Precision.HIGHEST,
                       preferred_element_type=F32) + b_ref[0]


def _ada_call(c, ada_w, ada_b):
    nt = 1024
    return pl.pallas_call(
        _ada_kernel,
        grid=(DEPTH, 3 * D_MODEL // nt),
        in_specs=[
            pl.BlockSpec((BATCH, D_MODEL), lambda i, n: (0, 0)),
            pl.BlockSpec((1, D_MODEL, nt), lambda i, n: (i, 0, n)),
            pl.BlockSpec((1, 1, nt), lambda i, n: (i, 0, n)),
        ],
        out_specs=pl.BlockSpec((1, BATCH, nt), lambda i, n: (i, 0, n)),
        out_shape=jax.ShapeDtypeStruct((DEPTH, BATCH, 3 * D_MODEL), F32),
        compiler_params=_cparams(("parallel", "parallel")),
        name="ada_mod",
    )(c, ada_w, ada_b.reshape(DEPTH, 1, 3 * D_MODEL))


def _norm_mod(x_ref, ng_ref, sc_ref, sh_ref, xT_out=()):
    x = x_ref[0]
    if xT_out:
        x = x.T
        xT_out[0][0] = x
    ms = jnp.mean(x * x, axis=0, keepdims=True)
    y = x * lax.rsqrt(ms + NORM_EPS)
    h = (y * ng_ref[...]) * (1.0 + sc_ref[0]) + sh_ref[0]
    return h.astype(BF16)


def _head_norm_rope(y, g, cos, sin):
    ms = jnp.mean(y * y, axis=0, keepdims=True)
    yn = (y * lax.rsqrt(ms + NORM_EPS)) * g
    x1 = yn[0:ROT_HALF]
    x2 = yn[ROT_HALF:2 * ROT_HALF]
    return jnp.concatenate([x1 * cos - x2 * sin, x2 * cos + x1 * sin, yn[2 * ROT_HALF:]], axis=0)


def _store_k_chunks(o_ref, gi, yh, ind=None):
    for it in range(TS // TQ):
        chunk = yh[:, it * TQ:(it + 1) * TQ]
        if ind is None:
            o_ref[0, gi, it] = chunk.T.astype(BF16)
        else:
            aug = jnp.concatenate([chunk, ind], axis=0).T
            o_ref[0, gi, it] = aug[:, :K_AUG].astype(BF16)


def _store_tiles(o_ref, gi, rows, y):
    for it in range(TS // TQ):
        o_ref[0, gi, it, rows, :] = y[:, it * TQ:(it + 1) * TQ]


def _project_queries(proj, off, qg_ref, cos, sin, q_ref):
    for gi in range(N_KV):
        y = proj(off + gi * 256, 256)
        for r in range(GQA):
            rows = slice(r * HEAD_DIM, (r + 1) * HEAD_DIM)
            yh = _head_norm_rope(y[rows], qg_ref[...], cos, sin)
            _store_tiles(q_ref, gi, rows, (yh * Q_SCALE).astype(BF16))


def _project_silu(proj, off, sz_ref):
    for gi in range(N_KV):
        z = proj(off + gi * 256, 256)
        _store_tiles(sz_ref, gi, slice(None), z * jax.nn.sigmoid(z))


def _block_indicator(blocks):
    row = lax.broadcasted_iota(jnp.int32, (HEAD_DIM, TQ), 0)
    lane = lax.broadcasted_iota(jnp.int32, (HEAD_DIM, TQ), 1)
    return (lane // (TQ // blocks) == row).astype(F32)


def _store_v_chunks(o_ref, y):
    yb = y.astype(BF16)
    row = lax.broadcasted_iota(jnp.int32, (V_AUG - HEAD_DIM, TQ), 0)
    ones_row = (row == 0).astype(BF16)
    for gi in range(N_KV):
        for it in range(TS // TQ):
            v = yb[gi * HEAD_DIM:(gi + 1) * HEAD_DIM, it * TQ:(it + 1) * TQ]
            o_ref[0, gi, it] = jnp.concatenate([v, ones_row], axis=0)


NSA_OFF_Q = 0
NSA_OFF_KS = 1024
NSA_OFF_KW = 1280
NSA_OFF_KC = 1536
NSA_OFF_VC = 1792
NSA_OFF_VS = 2048
NSA_OFF_VW = 2304
NSA_OFF_GT = 2560
NSA_OFF_Z = NSA_OFF_GT + N_KV * GATE_ROWS
NSA_ROWS = NSA_OFF_Z + D_MODEL


def _nsa_in_kernel(x_ref, ng_ref, sc_ref, sh_ref, w_ref, qg_ref, ksg_ref, kwg_ref, gb_ref,
                   cos_ref, sin_ref,
                   q_ref, ks_ref, kw_ref, kc_ref, vc_ref, vs_ref, vw_ref, gt_ref, sz_ref, *xT_out):
    hb = _norm_mod(x_ref, ng_ref, sc_ref, sh_ref, xT_out)
    cos = cos_ref[0]
    sin = sin_ref[0]

    def proj(r0, n):
        return _dot(w_ref[r0:r0 + n, :], hb)

    _project_queries(proj, NSA_OFF_Q, qg_ref, cos, sin, q_ref)
    sel_ind = _block_indicator(TQ // SEL_BLOCK)
    for off, g_ref, o_ref, ind in ((NSA_OFF_KS, ksg_ref, ks_ref, sel_ind),
                                   (NSA_OFF_KW, kwg_ref, kw_ref, None)):
        y = proj(off, 256)
        for gi in range(N_KV):
            yh = _head_norm_rope(y[gi * HEAD_DIM:(gi + 1) * HEAD_DIM], g_ref[...], cos, sin)
            _store_k_chunks(o_ref, gi, yh, ind)
    for off, o_ref in ((NSA_OFF_KC, kc_ref), (NSA_OFF_VC, vc_ref)):
        y = proj(off, 256)
        for gi in range(N_KV):
            for it in range(TS // TQ):
                o_ref[0, gi, it] = y[gi * HEAD_DIM:(gi + 1) * HEAD_DIM, it * TQ:(it + 1) * TQ].T
    _store_v_chunks(vs_ref, proj(NSA_OFF_VS, 256))
    _store_v_chunks(vw_ref, proj(NSA_OFF_VW, 256))
    gates = jax.nn.sigmoid(proj(NSA_OFF_GT, N_KV * GATE_ROWS) + gb_ref[...])
    for gi in range(N_KV):
        _store_tiles(gt_ref, gi, slice(None), gates[gi * GATE_ROWS:(gi + 1) * GATE_ROWS])
    _project_silu(proj, NSA_OFF_Z, sz_ref)


def _col_spec(rows):
    return pl.BlockSpec((rows, 1), lambda b, s: (0, 0))


def _bcol_spec(rows):
    return pl.BlockSpec((1, rows, 1), lambda b, s: (b, 0, 0))


def _fm_spec(rows):
    return pl.BlockSpec((1, rows, TS), lambda b, s: (b, 0, s))


def _k_chunk_spec(width):
    return pl.BlockSpec((1, N_KV, TS // TQ, TQ, width), lambda b, s: (b, 0, s, 0, 0))


def _k_chunk_shape(width, dtype=BF16):
    return jax.ShapeDtypeStruct((BATCH, N_KV, N_CHUNK, TQ, width), dtype)


def _tile_spec(rows):
    return pl.BlockSpec((1, N_KV, TS // TQ, rows, TQ), lambda b, s: (b, 0, s, 0, 0))


def _tile_shape(rows, dtype):
    return jax.ShapeDtypeStruct((BATCH, N_KV, N_CHUNK, rows, TQ), dtype)


_V_CHUNK_SPEC = pl.BlockSpec((1, N_KV, TS // TQ, V_AUG, TQ), lambda b, s: (b, 0, s, 0, 0))
_V_CHUNK_SHAPE = jax.ShapeDtypeStruct((BATCH, N_KV, N_CHUNK, V_AUG, TQ), BF16)


def _nsa_in_call(x, ng, sc, sh, wT, qg, ksg, kwg, gb, cosT, sinT, x_token_major=False):
    x_spec = _fm_spec(D_MODEL)
    if x_token_major:
        x_spec = pl.BlockSpec((1, TS, D_MODEL), lambda b, s: (b, s, 0))
    extra_specs = [_fm_spec(D_MODEL)] if x_token_major else []
    extra_shapes = [jax.ShapeDtypeStruct((BATCH, D_MODEL, SEQ), F32)] if x_token_major else []
    return pl.pallas_call(
        _nsa_in_kernel,
        grid=(BATCH, SEQ // TS),
        in_specs=[
            x_spec, _col_spec(D_MODEL), _bcol_spec(D_MODEL), _bcol_spec(D_MODEL),
            pl.BlockSpec((NSA_ROWS, D_MODEL), lambda b, s: (0, 0)),
            _col_spec(HEAD_DIM), _col_spec(HEAD_DIM), _col_spec(HEAD_DIM),
            _col_spec(N_KV * GATE_ROWS),
            _fm_spec(ROT_HALF), _fm_spec(ROT_HALF),
        ],
        out_specs=[
            _tile_spec(GQA * HEAD_DIM), _k_chunk_spec(K_AUG), _k_chunk_spec(HEAD_DIM),
            _k_chunk_spec(HEAD_DIM), _k_chunk_spec(HEAD_DIM),
            _V_CHUNK_SPEC, _V_CHUNK_SPEC, _tile_spec(GATE_ROWS), _tile_spec(GQA * HEAD_DIM),
        ] + extra_specs,
        out_shape=[
            _tile_shape(GQA * HEAD_DIM, BF16), _k_chunk_shape(K_AUG), _k_chunk_shape(HEAD_DIM),
            _k_chunk_shape(HEAD_DIM, F32), _k_chunk_shape(HEAD_DIM, F32),
            _V_CHUNK_SHAPE, _V_CHUNK_SHAPE, _tile_shape(GATE_ROWS, F32),
            _tile_shape(GQA * HEAD_DIM, F32),
        ] + extra_shapes,
        compiler_params=_cparams(("parallel", "parallel")),
        name="nsa_in_proj",
    )(x, ng, sc, sh, wT, qg, ksg, kwg, gb, cosT, sinT)


MOBA_OFF_Q = 0
MOBA_OFF_K = 1024
MOBA_OFF_V = 1280
MOBA_OFF_Z = 1536
MOBA_ROWS = 2560
KM_LANES = 128


def _moba_in_kernel(x_ref, ng_ref, sc_ref, sh_ref, w_ref, qg_ref, kg_ref, cos_ref, sin_ref,
                    q_ref, k_ref, km_ref, v_ref, sz_ref):
    hb = _norm_mod(x_ref, ng_ref, sc_ref, sh_ref)
    cos = cos_ref[0]
    sin = sin_ref[0]

    def proj(r0, n):
        return _dot(w_ref[r0:r0 + n, :], hb)

    _project_queries(proj, MOBA_OFF_Q, qg_ref, cos, sin, q_ref)
    y = proj(MOBA_OFF_K, 256)
    lane = lax.broadcasted_iota(jnp.int32, (HEAD_DIM, KM_LANES), 1)
    for gi in range(N_KV):
        yh = _head_norm_rope(y[gi * HEAD_DIM:(gi + 1) * HEAD_DIM], kg_ref[...], cos, sin)
        _store_k_chunks(k_ref, gi, yh, _block_indicator(1))
        km = jnp.zeros((HEAD_DIM, KM_LANES), F32)
        for it in range(TS // TQ):
            mean = jnp.mean(yh[:, it * TQ:(it + 1) * TQ], axis=1, keepdims=True)
            km = jnp.where(lane == it, mean, km)
        km_ref[0, 0, gi * HEAD_DIM:(gi + 1) * HEAD_DIM, :] = km
    _store_v_chunks(v_ref, proj(MOBA_OFF_V, 256))
    _project_silu(proj, MOBA_OFF_Z, sz_ref)


def _moba_in_call(xT, ng, sc, sh, wT, qg, kg, cosT, sinT):
    fm = lambda rows, dt: jax.ShapeDtypeStruct((BATCH, rows, SEQ), dt)
    return pl.pallas_call(
        _moba_in_kernel,
        grid=(BATCH, SEQ // TS),
        in_specs=[
            _fm_spec(D_MODEL), _col_spec(D_MODEL), _bcol_spec(D_MODEL), _bcol_spec(D_MODEL),
            pl.BlockSpec((MOBA_ROWS, D_MODEL), lambda b, s: (0, 0)),
            _col_spec(HEAD_DIM), _col_spec(HEAD_DIM),
            _fm_spec(ROT_HALF), _fm_spec(ROT_HALF),
        ],
        out_specs=[
            _tile_spec(GQA * HEAD_DIM), _k_chunk_spec(K_AUG),
            pl.BlockSpec((1, 1, N_KV * HEAD_DIM, KM_LANES), lambda b, s: (b, s, 0, 0)),
            _V_CHUNK_SPEC, _tile_spec(GQA * HEAD_DIM),
        ],
        out_shape=[
            _tile_shape(GQA * HEAD_DIM, BF16), _k_chunk_shape(K_AUG),
            jax.ShapeDtypeStruct((BATCH, SEQ // TS, N_KV * HEAD_DIM, KM_LANES), F32),
            _V_CHUNK_SHAPE, _tile_shape(GQA * HEAD_DIM, F32),
        ],
        compiler_params=_cparams(("parallel", "parallel")),
        name="moba_in_proj",
    )(xT, ng, sc, sh, wT, qg, kg, cosT, sinT)


CMP_NB = 8
CMP_ROWS = CMP_NB * N_CMP_PAD
CMP_COLS = BATCH * N_KV * N_CMP_PAD
CMP_FEAT = CMP_STRIDE * HEAD_DIM


def _compress_mlp(x_ref, pea_ref, peb_ref, w1a_ref, w1b_ref, b1_ref, w2t_ref):
    x = x_ref[...].reshape(CMP_ROWS, CMP_FEAT)
    u = _dot((x + pea_ref[...]).astype(BF16), w1a_ref[...])
    v = _dot((x + peb_ref[...]).astype(BF16), w1b_ref[...])
    v = pltpu.roll(v, CMP_ROWS - 1, 0)
    h = u + v + b1_ref[...]
    h = 0.5 * h * (1.0 + jnp.tanh(np.sqrt(2.0 / np.pi) * (h + 0.044715 * (h * h * h))))
    return lax.dot_general(w2t_ref[...], h.astype(BF16), (((1,), (1,)), ((), ())),
                           preferred_element_type=F32)


def _compress_key_kernel(x_ref, pea_ref, peb_ref, w1a_ref, w1b_ref, b1_ref, w2t_ref, kg_ref,
                         cos_ref, sin_ref, o_ref):
    out = _compress_mlp(x_ref, pea_ref, peb_ref, w1a_ref, w1b_ref, b1_ref, w2t_ref)
    out = _head_norm_rope(out, kg_ref[...], cos_ref[...], sin_ref[...])
    for i in range(CMP_NB):
        o_ref[i] = out[:, i * N_CMP_PAD:(i + 1) * N_CMP_PAD].T.astype(BF16)


def _compress_value_kernel(x_ref, pea_ref, peb_ref, w1a_ref, w1b_ref, b1_ref, w2t_ref, o_ref):
    out = _compress_mlp(x_ref, pea_ref, peb_ref, w1a_ref, w1b_ref, b1_ref, w2t_ref)
    for i in range(CMP_NB):
        o_ref[i] = out[:, i * N_CMP_PAD:(i + 1) * N_CMP_PAD].astype(BF16)


def _compress_call(x, pe, w1, b1, w2, key_extras=None):
    full = lambda shape: pl.BlockSpec(shape, lambda t: (0,) * len(shape))
    in_specs = [
        pl.BlockSpec((CMP_NB, N_CMP_PAD, CMP_FEAT), lambda t: (t, 0, 0)),
        full((1, CMP_FEAT)), full((1, CMP_FEAT)),
        full((CMP_FEAT, CMP_HID)), full((CMP_FEAT, CMP_HID)),
        full((1, CMP_HID)), full((HEAD_DIM, CMP_HID)),
    ]
    pe = pe.reshape(2, 1, CMP_FEAT)
    w1 = w1.astype(BF16)
    args = [x, pe[0], pe[1], w1[:CMP_FEAT], w1[CMP_FEAT:], b1.reshape(1, CMP_HID), w2.T.astype(BF16)]
    if key_extras is not None:
        in_specs += [full((HEAD_DIM, 1)),
                     pl.BlockSpec((ROT_HALF, CMP_ROWS), lambda t: (0, t)),
                     pl.BlockSpec((ROT_HALF, CMP_ROWS), lambda t: (0, t))]
        args += list(key_extras)
        kernel, out_tail = _compress_key_kernel, (N_CMP_PAD, HEAD_DIM)
    else:
        kernel, out_tail = _compress_value_kernel, (HEAD_DIM, N_CMP_PAD)
    return pl.pallas_call(
        kernel,
        grid=(BATCH * N_KV // CMP_NB,),
        in_specs=in_specs,
        out_specs=pl.BlockSpec((CMP_NB,) + out_tail, lambda t: (t, 0, 0)),
        out_shape=jax.ShapeDtypeStruct((BATCH * N_KV,) + out_tail, BF16),
        compiler_params=_cparams(("parallel",)),
        name="nsa_compress_key" if key_extras is not None else "nsa_compress_value",
    )(*args)


def _group_queries(q_ref):
    return jnp.concatenate(
        [q_ref[0, r * HEAD_DIM:(r + 1) * HEAD_DIM, :] for r in range(GQA)], axis=1)


COL = 128


def _col_tiles():
    per_head = TQ // COL
    for ct in range(LANES_Q // COL):
        r, h = divmod(ct, per_head)
        yield r, slice(h * COL, (h + 1) * COL), slice(ct * COL, (ct + 1) * COL)


def _flash_scratch():
    return [
        pltpu.VMEM((2, TQ, LANES_Q), BF16),
        pltpu.VMEM((2, 1, LANES_Q), F32),
        pltpu.VMEM((1, LANES_Q), F32),
        pltpu.VMEM((V_AUG, LANES_Q), F32),
    ]


class _FlashBranch:
    def __init__(self, q_ref, k_ref, v_ref, bufs, own, past_chunk, own_fix=None, past_fix=None,
                 own_rows=None, past_rows=None):
        self.q_ref, self.k_ref, self.v_ref, self.bufs = q_ref, k_ref, v_ref, bufs
        self.own_chunk, self.past_chunk = own, past_chunk
        self.own_fix, self.past_fix, self.own_rows, self.past_rows = own_fix, past_fix, own_rows, past_rows

    def _softmax_tile(self, s, krows, cs, slot, init):
        p_buf, a_buf, m_ref, _ = self.bufs
        m_loc = jnp.max(s, axis=0, keepdims=True)
        if init:
            m_new = m_loc
            a_buf[slot, :, cs] = jnp.ones((1, COL), F32)
        else:
            m_old = m_ref[:, cs]
            m_new = jnp.maximum(m_old, m_loc)
            a_buf[slot, :, cs] = jnp.exp2(m_old - m_new)
        m_ref[:, cs] = m_new
        p_buf[slot, krows, cs] = jnp.exp2(s - m_new).astype(BF16)
        if krows.start > 0:
            p_buf[slot, 0:krows.start, cs] = jnp.zeros((krows.start, COL), BF16)
        if krows.stop < TQ:
            p_buf[slot, krows.stop:TQ, cs] = jnp.zeros((TQ - krows.stop, COL), BF16)

    def _scores(self, k, r, qs, cs, rows, fix):
        q = self.q_ref[0, r * HEAD_DIM:(r + 1) * HEAD_DIM, qs]
        if rows is not None:
            q = jnp.concatenate([q, rows(cs)], axis=0)
        krows = slice(0, TQ) if fix is None else fix.key_rows(qs)
        s = _dot(k[krows, :], q)
        if fix is not None:
            s = fix(s, qs, krows)
        return s, krows

    def _pv_slab(self, vT, slot, ct):
        if (ct * COL) % PV_COL:
            return
        p_buf, a_buf, _, acc_ref = self.bufs
        cs = slice(ct * COL, ct * COL + PV_COL)
        acc_ref[:, cs] = a_buf[slot, :, cs] * acc_ref[:, cs] + _dot(vT, p_buf[slot, :, cs])

    def own(self):
        acc_ref = self.bufs[3]
        acc_ref[...] = jnp.zeros(acc_ref.shape, F32)
        k_own = self.k_ref[0, 0, self.own_chunk]
        for r, qs, cs in _col_tiles():
            s, krows = self._scores(k_own, r, qs, cs, self.own_rows, self.own_fix)
            self._softmax_tile(s, krows, cs, 0, True)

    def trip(self, j, cur, prv):
        k = self.k_ref[0, 0, self.past_chunk(j)]
        vT = self.v_ref[0, 0, self._chunk_before(j)]
        rows = None if self.past_rows is None else self.past_rows(j)
        for ct, (r, qs, cs) in enumerate(_col_tiles()):
            s, krows = self._scores(k, r, qs, cs, rows, self.past_fix)
            self._pv_slab(vT, prv, ct)
            self._softmax_tile(s, krows, cs, cur, False)

    def run_past(self, n_past):
        if isinstance(n_past, int):
            for j in range(n_past):
                self.trip(j, 1 - j % 2, j % 2)
            return

        def pair(pp, carry):
            self.trip(2 * pp, 1, 0)
            self.trip(2 * pp + 1, 0, 1)
            return carry

        lax.fori_loop(0, n_past // 2, pair, 0)

        @pl.when(n_past % 2 == 1)
        def _():
            self.trip(n_past - 1, 1, 0)

    def _chunk_before(self, j):
        if isinstance(j, int):
            return self.own_chunk if j == 0 else self.past_chunk(j - 1)
        return jnp.where(j == 0, self.own_chunk, self.past_chunk(jnp.maximum(j - 1, 0)))

    def finish(self, n_past):
        vT = self.v_ref[0, 0, self._chunk_before(n_past)]
        for ct in range(LANES_Q // COL):
            self._pv_slab(vT, n_past % 2, ct)
        acc = self.bufs[3][...]
        return acc[:HEAD_DIM] / acc[HEAD_DIM:HEAD_DIM + 1]


def _flash_branch(q_ref, k_ref, v_ref, bufs, own, n_past, past_chunk, **kw):
    branch = _FlashBranch(q_ref, k_ref, v_ref, bufs, own, past_chunk, **kw)
    branch.own()
    branch.run_past(n_past)
    return branch.finish(n_past)


class _CausalFix:
    def __init__(self, keep_lower):
        self.keep_lower = keep_lower

    def key_rows(self, qs):
        half = TQ // 2
        if self.keep_lower and qs.stop <= half:
            return slice(0, half)
        if not self.keep_lower and qs.start >= half:
            return slice(half, TQ)
        return slice(0, TQ)

    def __call__(self, s, qs, rows):
        n = rows.stop - rows.start
        a_idx = lax.broadcasted_iota(jnp.int32, (n, COL), 0) + rows.start
        lane = lax.broadcasted_iota(jnp.int32, (n, COL), 1) + qs.start
        return jnp.where(a_idx <= lane if self.keep_lower else a_idx > lane, s, NEG)


def _causal_fix(keep_lower):
    return _CausalFix(keep_lower)


def _rank_select(score_ref, n_rows, j_idx, count):
    score = score_ref[...]
    cnt = jnp.zeros(score.shape, jnp.int32)
    for jp in range(n_rows):
        row = score_ref[jp:jp + 1, :]
        beats = (row > score) | ((row == score) & (jp < j_idx))
        cnt = cnt + beats.astype(jnp.int32)
    return cnt < count


class _TileView:
    def __init__(self, ref, tile):
        self.ref, self.tile = ref, tile

    def __getitem__(self, idx):
        return self.ref[(0, 0, self.tile) + tuple(idx[1:])]

    def __setitem__(self, idx, value):
        self.ref[(0, 0, self.tile) + tuple(idx[1:])] = value


def _for_each_query_tile(tile_fn, tiled_refs, other_refs):
    for qi in range(N_CHUNK):
        tile_fn(qi, *[_TileView(ref, qi) for ref in tiled_refs], *other_refs)


def _store_out(o_ref, sz_ref, o):
    for r in range(GQA):
        rows = slice(r * HEAD_DIM, (r + 1) * HEAD_DIM)
        o_ref[0, rows, :] = (o[:, r * TQ:(r + 1) * TQ] * sz_ref[0, rows, :]).astype(BF16)


def _nsa_attn_kernel(q_ref, kc_ref, vc_ref, ks_ref, vs_ref, kw_ref, vw_ref, gt_ref, sz_ref,
                     selw_ref, o_ref, score_ref, bias_ref, *bufs):
    _for_each_query_tile(_nsa_attn_tile, (q_ref, gt_ref, sz_ref, o_ref),
                         (kc_ref, vc_ref, ks_ref, vs_ref, kw_ref, vw_ref, selw_ref,
                          score_ref, bias_ref) + tuple(bufs))


def _nsa_attn_tile(qi, q_ref, gt_ref, sz_ref, o_ref, kc_ref, vc_ref, ks_ref, vs_ref, kw_ref, vw_ref,
                   selw_ref, score_ref, bias_ref, *bufs):
    q4 = _group_queries(q_ref)

    sel_bufs, win_bufs = bufs[:len(bufs) // 2], bufs[len(bufs) // 2:]
    keep_lower = _causal_fix(True)

    n_win = min(qi, 1)
    win = _FlashBranch(q_ref, kw_ref, vw_ref, win_bufs, qi, lambda j: qi - 1,
                       own_fix=keep_lower, past_fix=_causal_fix(False))

    t = qi * TQ + (lax.broadcasted_iota(jnp.int32, (1, LANES_Q), 1) & (TQ - 1))
    n_idx = lax.broadcasted_iota(jnp.int32, (N_CMP_PAD, LANES_Q), 0)
    s_cmp = _dot(kc_ref[0, 0], q4)
    win.own()
    s = jnp.where(n_idx * CMP_STRIDE + (CMP_LEN - 1) <= t, s_cmp, NEG)
    m = jnp.max(s, axis=0, keepdims=True)
    p = jnp.exp2(s - m) * (m > 0.5 * NEG).astype(F32)
    p = p / jnp.maximum(jnp.sum(p, axis=0, keepdims=True), 1e-30)
    o_cmp = _dot(vc_ref[0, 0], p.astype(BF16))

    psum = p[:, 0:TQ]
    for r in range(1, GQA):
        psum = psum + p[:, r * TQ:(r + 1) * TQ]
    p_hi = psum.astype(BF16)
    p_lo = (psum - p_hi.astype(F32)).astype(BF16)
    imp = _dot(selw_ref[...], p_hi) + _dot(selw_ref[...], p_lo)
    j_idx = lax.broadcasted_iota(jnp.int32, (N_SEL, TQ), 0)
    cur = qi * (TQ // SEL_BLOCK) + (lax.broadcasted_iota(jnp.int32, (N_SEL, TQ), 1) >> 6)
    valid = j_idx <= cur
    forced = (j_idx == 0) | (j_idx == cur) | (j_idx == cur - 1)
    score_ref[...] = jnp.where(forced, jnp.inf, jnp.where(valid, imp, -jnp.inf))
    win.run_past(n_win)
    sel = valid & _rank_select(score_ref, N_SEL, j_idx, SEL_COUNT)
    bias = jnp.where(sel, 0.0, NEG)
    bias = jnp.concatenate([bias] * GQA, axis=1)
    per_chunk = TQ // SEL_BLOCK
    bias_ref[...] = jnp.zeros(bias_ref.shape, F32)
    for jp in range(N_SEL):
        bias_ref[jp // per_chunk, jp % per_chunk:jp % per_chunk + 1, :] = bias[jp:jp + 1, :]

    o_win = win.finish(n_win)

    def sel_rows(j):
        return lambda cs: bias_ref[j, :, cs].astype(BF16)

    o_slc = _flash_branch(q_ref, ks_ref, vs_ref, sel_bufs, qi, qi, lambda j: j, own_fix=keep_lower,
                          own_rows=sel_rows(qi), past_rows=sel_rows)

    def gate(br):
        return jnp.concatenate(
            [gt_ref[0, br * GQA + r:br * GQA + r + 1, :] for r in range(GQA)], axis=1)

    o = gate(0) * o_cmp + gate(1) * o_slc + gate(2) * o_win
    _store_out(o_ref, sz_ref, o)


def _group_spec(*tail):
    return pl.BlockSpec((1, 1) + tail, lambda b, g: (b, g) + (0,) * len(tail))


def _attn_specs():
    q_spec = _group_spec(N_CHUNK, GQA * HEAD_DIM, TQ)
    k_spec = lambda width: _group_spec(N_CHUNK, TQ, width)
    v_spec = _group_spec(N_CHUNK, V_AUG, TQ)
    return q_spec, k_spec, v_spec


def _nsa_attn_call(q, kc, vc, ks, vs, kw, vw, gt, sz, selw):
    q_spec, k_spec, v_spec = _attn_specs()
    return pl.pallas_call(
        _nsa_attn_kernel,
        grid=(BATCH, N_KV),
        in_specs=[
            q_spec,
            _group_spec(N_CMP_PAD, HEAD_DIM), _group_spec(HEAD_DIM, N_CMP_PAD),
            k_spec(K_AUG), v_spec, k_spec(HEAD_DIM), v_spec,
            _group_spec(N_CHUNK, GATE_ROWS, TQ),
            q_spec,
            pl.BlockSpec((N_SEL, N_CMP_PAD), lambda b, g: (0, 0)),
        ],
        out_specs=q_spec,
        out_shape=_tile_shape(GQA * HEAD_DIM, BF16),
        scratch_shapes=[
            pltpu.VMEM((N_SEL, TQ), F32),
            pltpu.VMEM((N_CHUNK, BIAS_ROWS, LANES_Q), F32),
        ] + _flash_scratch() + _flash_scratch(),
        compiler_params=_cparams(("parallel", "parallel")),
        name="nsa_attention",
    )(q, kc, vc, ks, vs, kw, vw, gt, sz, selw)


def _moba_attn_kernel(q_ref, km_ref, k_ref, v_ref, sz_ref, o_ref,
                      score_ref, bias_ref, *bufs):
    _for_each_query_tile(_moba_attn_tile, (q_ref, sz_ref, o_ref),
                         (km_ref, k_ref, v_ref, score_ref, bias_ref) + tuple(bufs))


def _moba_attn_tile(qi, q_ref, sz_ref, o_ref, km_ref, k_ref, v_ref, score_ref, bias_ref, *bufs):
    q4 = _group_queries(q_ref)

    j_idx = lax.broadcasted_iota(jnp.int32, (N_CHUNK, LANES_Q), 0)
    past = j_idx < qi
    score_ref[...] = jnp.where(past, _dot(km_ref[0, 0], q4), -jnp.inf)
    sel = past & _rank_select(score_ref, N_CHUNK, j_idx, MOBA_TOPK)
    bias = jnp.where(sel, 0.0, NEG)
    bias_ref[...] = jnp.zeros(bias_ref.shape, F32)
    for jp in range(N_CHUNK):
        bias_ref[jp, 0:1, :] = bias[jp:jp + 1, :]

    o = _flash_branch(q_ref, k_ref, v_ref, bufs, qi, qi, lambda j: j, own_fix=_causal_fix(True),
                      own_rows=lambda cs: jnp.zeros((BIAS_ROWS, COL), BF16),
                      past_rows=lambda j: (lambda cs: bias_ref[j, :, cs].astype(BF16)))
    _store_out(o_ref, sz_ref, o)


def _moba_attn_call(q, km, k, v, sz):
    q_spec, k_spec, v_spec = _attn_specs()
    return pl.pallas_call(
        _moba_attn_kernel,
        grid=(BATCH, N_KV),
        in_specs=[
            q_spec,
            _group_spec(N_CHUNK, HEAD_DIM),
            k_spec(K_AUG), v_spec, q_spec,
        ],
        out_specs=q_spec,
        out_shape=_tile_shape(GQA * HEAD_DIM, BF16),
        scratch_shapes=[
            pltpu.VMEM((N_CHUNK, LANES_Q), F32),
            pltpu.VMEM((N_CHUNK, BIAS_ROWS, LANES_Q), F32),
        ] + _flash_scratch(),
        compiler_params=_cparams(("parallel", "parallel")),
        name="moba_attention",
    )(q, km, k, v, sz)


def _out_kernel(oz_ref, w_ref, x_ref, gt_ref, o_ref, *, token_major_out):
    group_rows = GQA * HEAD_DIM
    for it in range(TS // TQ):
        lanes = slice(it * TQ, (it + 1) * TQ)
        for cblk in range(D_MODEL // 256):
            rows = slice(cblk * 256, (cblk + 1) * 256)
            y = _dot(w_ref[rows, 0:group_rows], oz_ref[0, 0, it])
            for gi in range(1, N_KV):
                y = y + _dot(w_ref[rows, gi * group_rows:(gi + 1) * group_rows], oz_ref[0, gi, it])
            new_x = x_ref[0, rows, lanes] + gt_ref[0, rows, :] * y
            if token_major_out:
                o_ref[0, lanes, rows] = new_x.T
            else:
                o_ref[0, rows, lanes] = new_x


def _out_call(oz, wT, xT, gate, token_major_out=False):
    if token_major_out:
        out_spec = pl.BlockSpec((1, TS, D_MODEL), lambda b, s: (b, s, 0))
        out_shape = jax.ShapeDtypeStruct((BATCH, SEQ, D_MODEL), F32)
    else:
        out_spec = _fm_spec(D_MODEL)
        out_shape = jax.ShapeDtypeStruct((BATCH, D_MODEL, SEQ), F32)
    return pl.pallas_call(
        functools.partial(_out_kernel, token_major_out=token_major_out),
        grid=(BATCH, SEQ // TS),
        in_specs=[
            _tile_spec(GQA * HEAD_DIM),
            pl.BlockSpec((D_MODEL, D_MODEL), lambda b, s: (0, 0)),
            _fm_spec(D_MODEL), _bcol_spec(D_MODEL),
        ],
        out_specs=out_spec,
        out_shape=out_shape,
        compiler_params=_cparams(("parallel", "parallel")),
        name="out_proj",
    )(oz, wT, xT, gate)


def _rope_tables(pos):
    inv_freq = ROPE_THETA ** (-jnp.arange(0, 2 * ROT_HALF, 2, dtype=F32) / (2 * ROT_HALF))
    ang = pos.astype(F32)[..., None] * inv_freq
    return jnp.cos(ang), jnp.sin(ang)


def _gate_perm():
    perm = np.full((N_KV * GATE_ROWS,), 3 * N_HEADS, dtype=np.int32)
    for g in range(N_KV):
        for br in range(3):
            for r in range(GQA):
                perm[g * GATE_ROWS + br * GQA + r] = (g * GQA + r) * 3 + br
    return perm


def _sel_weights_T():
    cs = np.arange(N_CMP)[:, None] * CMP_STRIDE
    ss = np.arange(N_SEL)[None, :] * SEL_BLOCK
    shared = np.clip(np.minimum(cs + CMP_LEN, ss + SEL_BLOCK) - np.maximum(cs, ss), 0, None)
    w = np.zeros((N_CMP_PAD, N_SEL), np.float32)
    w[:N_CMP] = shared / CMP_LEN
    return jnp.asarray(w.T, dtype=BF16)


def _col(v):
    return v.reshape(-1, 1)


def _nsa_layer(xT, ng, sc, sh, gate, cosT, sinT, cosc, sinc, w_in, w_out, q_g, k_g,
               cmp_pe, cmp_w1, cmp_b1, cmp_w2, gate_b, x_token_major=False):
    widths = [1024] + [256] * 6 + [3 * N_HEADS, 1024]
    q, kc, vc, ks, vs, kw, vw, gl, z = jnp.split(w_in, np.cumsum(widths)[:-1].tolist(), axis=1)
    perm = _gate_perm()
    gl_p = jnp.concatenate([gl, jnp.zeros((D_MODEL, 1), F32)], axis=1)[:, perm]
    gb_p = jnp.concatenate([gate_b, jnp.zeros((1,), F32)])[perm]
    wT = jnp.concatenate([q, ks, kw, kc, vc, vs, vw, gl_p, z], axis=1).T.astype(BF16)

    outs = _nsa_in_call(xT, ng, sc, sh, wT, _col(q_g), _col(k_g[1]), _col(k_g[2]), _col(gb_p),
                        cosT, sinT, x_token_major=x_token_major)
    qT, ks_c, kw_c, kc_c, vc_c, vs_c, vw_c, gates, sz = outs[:9]
    if x_token_major:
        xT = outs[9]

    half_blocks = lambda t: t.reshape(BATCH * N_KV, N_CMP_PAD, CMP_FEAT)
    k_cmp = _compress_call(half_blocks(kc_c), cmp_pe[0], cmp_w1[0], cmp_b1[0], cmp_w2[0],
                           key_extras=(_col(k_g[0]), cosc, sinc))
    v_cmpT = _compress_call(half_blocks(vc_c), cmp_pe[1], cmp_w1[1], cmp_b1[1], cmp_w2[1])
    k_cmp = k_cmp.reshape(BATCH, N_KV, N_CMP_PAD, HEAD_DIM)
    v_cmpT = v_cmpT.reshape(BATCH, N_KV, HEAD_DIM, N_CMP_PAD)

    oz = _nsa_attn_call(qT, k_cmp, v_cmpT, ks_c, vs_c, kw_c, vw_c, gates, sz, _sel_weights_T())
    return _out_call(oz, w_out.T.astype(BF16), xT, gate)


def _moba_layer(xT, ng, sc, sh, gate, cosT, sinT, w_in, w_out, q_g, k_g, token_major_out=False):
    wT = w_in.T.astype(BF16)
    qT, k_c, km, v_c, sz = _moba_in_call(xT, ng, sc, sh, wT, _col(q_g), _col(k_g), cosT, sinT)
    nb = TS // TQ
    km = km[..., :nb].reshape(BATCH, SEQ // TS, N_KV, HEAD_DIM, nb)
    km = km.transpose(0, 2, 1, 4, 3).reshape(BATCH, N_KV, N_CHUNK, HEAD_DIM).astype(BF16)
    oz = _moba_attn_call(qT, km, k_c, v_c, sz)
    return _out_call(oz, w_out.T.astype(BF16), xT, gate, token_major_out)


@jax.jit
def _forward(x, c, positions, norm_g, ada_w, ada_b, nsa_w_in, nsa_w_out, nsa_q_norm, nsa_k_norm,
             nsa_cmp_pe, nsa_cmp_w1, nsa_cmp_b1, nsa_cmp_w2, nsa_gate_b,
             moba_w_in, moba_w_out, moba_q_norm, moba_k_norm):
    cos, sin = _rope_tables(positions)
    cosT = cos.transpose(0, 2, 1)
    sinT = sin.transpose(0, 2, 1)
    cmp_end = np.minimum(np.arange(N_CMP_PAD) * CMP_STRIDE + CMP_LEN - 1, SEQ - 1)
    cos_c, sin_c = _rope_tables(positions[:, cmp_end])

    def cmp_table(t):
        t = jnp.broadcast_to(t.transpose(2, 0, 1)[:, :, None, :],
                             (ROT_HALF, BATCH, N_KV, N_CMP_PAD))
        return t.reshape(ROT_HALF, CMP_COLS)

    cosc = cmp_table(cos_c)
    sinc = cmp_table(sin_c)

    mod = _ada_call(c, ada_w, ada_b)
    shift, scale, gate = jnp.split(mod[..., None], 3, axis=2)

    xT = x
    for i in range(DEPTH):
        j = i // 2
        ng = _col(norm_g[i])
        if i % 2 == 0:
            xT = _nsa_layer(xT, ng, scale[i], shift[i], gate[i], cosT, sinT, cosc, sinc,
                            nsa_w_in[j], nsa_w_out[j], nsa_q_norm[j], nsa_k_norm[j],
                            nsa_cmp_pe[j], nsa_cmp_w1[j], nsa_cmp_b1[j], nsa_cmp_w2[j],
                            nsa_gate_b[j], x_token_major=(i == 0))
        else:
            xT = _moba_layer(xT, ng, scale[i], shift[i], gate[i], cosT, sinT,
                             moba_w_in[j], moba_w_out[j], moba_q_norm[j], moba_k_norm[j],
                             token_major_out=(i == DEPTH - 1))
    return xT


def kernel(x, c, positions, norm_g, ada_w, ada_b, nsa_w_in, nsa_w_out, nsa_q_norm, nsa_k_norm, nsa_cmp_pe, nsa_cmp_w1, nsa_cmp_b1, nsa_cmp_w2, nsa_gate_b, moba_w_in, moba_w_out, moba_q_norm, moba_k_norm):
    return _forward(x, c, positions, norm_g, ada_w, ada_b, nsa_w_in, nsa_w_out, nsa_q_norm,
                    nsa_k_norm, nsa_cmp_pe, nsa_cmp_w1, nsa_cmp_b1, nsa_cmp_w2, nsa_gate_b,
                    moba_w_in, moba_w_out, moba_q_norm, moba_k_norm)
```

```python
import functools

import numpy as np
import jax
import jax.numpy as jnp
from jax import lax
from jax.experimental import pallas as pl
from jax.experimental.pallas import tpu as pltpu

D_MODEL = 1024
BATCH = 16
SEQ = 2048
DEPTH = 4
HEAD_DIM = 64
N_HEADS = 16
N_KV = 4
GQA = 4
ROT_HALF = 8
ROPE_THETA = 500000.0
NORM_EPS = 1e-6
CMP_LEN = 32
CMP_STRIDE = 16
CMP_HID = 256
SEL_BLOCK = 64
SEL_COUNT = 8
N_SEL = SEQ // SEL_BLOCK
N_CMP = (SEQ - CMP_LEN) // CMP_STRIDE + 1
N_CMP_PAD = 128
MOBA_TOPK = 3

TQ = 256
N_CHUNK = SEQ // TQ
LANES_Q = GQA * TQ
TS = 512
NEG = -1e30
BIAS_ROWS = 16
K_AUG = HEAD_DIM + BIAS_ROWS
V_AUG = HEAD_DIM + 16
PV_COL = 256
Q_SCALE = HEAD_DIM ** -0.5 * float(np.log2(np.e))
GATE_ROWS = 16

F32 = jnp.float32
BF16 = jnp.bfloat16

VMEM_LIMIT = 52 * 1024 * 1024


def _cparams(sem):
    return pltpu.CompilerParams(dimension_semantics=sem, vmem_limit_bytes=VMEM_LIMIT)


def _dot(a, b):
    return jnp.dot(a, b, preferred_element_type=F32)


def _ada_kernel(c_ref, w_ref, b_ref, o_ref):
    cond = c_ref[...]
    cond = cond * jax.nn.sigmoid(cond)
    o_ref[0] = jnp.dot(cond, w_ref[0], precision=lax.Precision.HIGHEST,
                       preferred_element_type=F32) + b_ref[0]


def _ada_call(c, ada_w, ada_b):
    nt = 1024
    return pl.pallas_call(
        _ada_kernel,
        grid=(DEPTH, 3 * D_MODEL // nt),
        in_specs=[
            pl.BlockSpec((BATCH, D_MODEL), lambda i, n: (0, 0)),
            pl.BlockSpec((1, D_MODEL, nt), lambda i, n: (i, 0, n)),
            pl.BlockSpec((1, 1, nt), lambda i, n: (i, 0, n)),
        ],
        out_specs=pl.BlockSpec((1, BATCH, nt), lambda i, n: (i, 0, n)),
        out_shape=jax.ShapeDtypeStruct((DEPTH, BATCH, 3 * D_MODEL), F32),
        compiler_params=_cparams(("parallel", "parallel")),
        name="ada_mod",
    )(c, ada_w, ada_b.reshape(DEPTH, 1, 3 * D_MODEL))


def _norm_mod(x_ref, ng_ref, sc_ref, sh_ref, xT_out=()):
    x = x_ref[0]
    if xT_out:
        x = x.T
        xT_out[0][0] = x
    ms = jnp.mean(x * x, axis=0, keepdims=True)
    y = x * lax.rsqrt(ms + NORM_EPS)
    h = (y * ng_ref[...]) * (1.0 + sc_ref[0]) + sh_ref[0]
    return h.astype(BF16)


def _head_norm_rope(y, g, cos, sin):
    ms = jnp.mean(y * y, axis=0, keepdims=True)
    yn = (y * lax.rsqrt(ms + NORM_EPS)) * g
    x1 = yn[0:ROT_HALF]
    x2 = yn[ROT_HALF:2 * ROT_HALF]
    return jnp.concatenate([x1 * cos - x2 * sin, x2 * cos + x1 * sin, yn[2 * ROT_HALF:]], axis=0)


def _store_k_chunks(o_ref, gi, yh, ind=None):
    for it in range(TS // TQ):
        chunk = yh[:, it * TQ:(it + 1) * TQ]
        if ind is None:
            o_ref[0, gi, it] = chunk.T.astype(BF16)
        else:
            aug = jnp.concatenate([chunk, ind], axis=0).T
            o_ref[0, gi, it] = aug[:, :K_AUG].astype(BF16)


def _store_tiles(o_ref, gi, rows, y):
    for it in range(TS // TQ):
        o_ref[0, gi, it, rows, :] = y[:, it * TQ:(it + 1) * TQ]


def _project_queries(proj, off, qg_ref, cos, sin, q_ref):
    for gi in range(N_KV):
        y = proj(off + gi * 256, 256)
        for r in range(GQA):
            rows = slice(r * HEAD_DIM, (r + 1) * HEAD_DIM)
            yh = _head_norm_rope(y[rows], qg_ref[...], cos, sin)
            _store_tiles(q_ref, gi, rows, (yh * Q_SCALE).astype(BF16))


def _project_silu(proj, off, sz_ref):
    for gi in range(N_KV):
        z = proj(off + gi * 256, 256)
        _store_tiles(sz_ref, gi, slice(None), z * jax.nn.sigmoid(z))


def _block_indicator(blocks):
    row = lax.broadcasted_iota(jnp.int32, (HEAD_DIM, TQ), 0)
    lane = lax.broadcasted_iota(jnp.int32, (HEAD_DIM, TQ), 1)
    return (lane // (TQ // blocks) == row).astype(F32)


def _store_v_chunks(o_ref, y):
    yb = y.astype(BF16)
    row = lax.broadcasted_iota(jnp.int32, (V_AUG - HEAD_DIM, TQ), 0)
    ones_row = (row == 0).astype(BF16)
    for gi in range(N_KV):
        for it in range(TS // TQ):
            v = yb[gi * HEAD_DIM:(gi + 1) * HEAD_DIM, it * TQ:(it + 1) * TQ]
            o_ref[0, gi, it] = jnp.concatenate([v, ones_row], axis=0)


NSA_OFF_Q = 0
NSA_OFF_KS = 1024
NSA_OFF_KW = 1280
NSA_OFF_KC = 1536
NSA_OFF_VC = 1792
NSA_OFF_VS = 2048
NSA_OFF_VW = 2304
NSA_OFF_GT = 2560
NSA_OFF_Z = NSA_OFF_GT + N_KV * GATE_ROWS
NSA_ROWS = NSA_OFF_Z + D_MODEL


def _nsa_in_kernel(x_ref, ng_ref, sc_ref, sh_ref, w_ref, qg_ref, ksg_ref, kwg_ref, gb_ref,
                   cos_ref, sin_ref,
                   q_ref, ks_ref, kw_ref, kc_ref, vc_ref, vs_ref, vw_ref, gt_ref, sz_ref, *xT_out):
    hb = _norm_mod(x_ref, ng_ref, sc_ref, sh_ref, xT_out)
    cos = cos_ref[0]
    sin = sin_ref[0]

    def proj(r0, n):
        return _dot(w_ref[r0:r0 + n, :], hb)

    _project_queries(proj, NSA_OFF_Q, qg_ref, cos, sin, q_ref)
    sel_ind = _block_indicator(TQ // SEL_BLOCK)
    for off, g_ref, o_ref, ind in ((NSA_OFF_KS, ksg_ref, ks_ref, sel_ind),
                                   (NSA_OFF_KW, kwg_ref, kw_ref, None)):
        y = proj(off, 256)
        for gi in range(N_KV):
            yh = _head_norm_rope(y[gi * HEAD_DIM:(gi + 1) * HEAD_DIM], g_ref[...], cos, sin)
            _store_k_chunks(o_ref, gi, yh, ind)
    for off, o_ref in ((NSA_OFF_KC, kc_ref), (NSA_OFF_VC, vc_ref)):
        y = proj(off, 256)
        for gi in range(N_KV):
            for it in range(TS // TQ):
                o_ref[0, gi, it] = y[gi * HEAD_DIM:(gi + 1) * HEAD_DIM, it * TQ:(it + 1) * TQ].T
    _store_v_chunks(vs_ref, proj(NSA_OFF_VS, 256))
    _store_v_chunks(vw_ref, proj(NSA_OFF_VW, 256))
    gates = jax.nn.sigmoid(proj(NSA_OFF_GT, N_KV * GATE_ROWS) + gb_ref[...])
    for gi in range(N_KV):
        _store_tiles(gt_ref, gi, slice(None), gates[gi * GATE_ROWS:(gi + 1) * GATE_ROWS])
    _project_silu(proj, NSA_OFF_Z, sz_ref)


def _col_spec(rows):
    return pl.BlockSpec((rows, 1), lambda b, s: (0, 0))


def _bcol_spec(rows):
    return pl.BlockSpec((1, rows, 1), lambda b, s: (b, 0, 0))


def _fm_spec(rows):
    return pl.BlockSpec((1, rows, TS), lambda b, s: (b, 0, s))


def _k_chunk_spec(width):
    return pl.BlockSpec((1, N_KV, TS // TQ, TQ, width), lambda b, s: (b, 0, s, 0, 0))


def _k_chunk_shape(width, dtype=BF16):
    return jax.ShapeDtypeStruct((BATCH, N_KV, N_CHUNK, TQ, width), dtype)


def _tile_spec(rows):
    return pl.BlockSpec((1, N_KV, TS // TQ, rows, TQ), lambda b, s: (b, 0, s, 0, 0))


def _tile_shape(rows, dtype):
    return jax.ShapeDtypeStruct((BATCH, N_KV, N_CHUNK, rows, TQ), dtype)


_V_CHUNK_SPEC = pl.BlockSpec((1, N_KV, TS // TQ, V_AUG, TQ), lambda b, s: (b, 0, s, 0, 0))
_V_CHUNK_SHAPE = jax.ShapeDtypeStruct((BATCH, N_KV, N_CHUNK, V_AUG, TQ), BF16)


def _nsa_in_call(x, ng, sc, sh, wT, qg, ksg, kwg, gb, cosT, sinT, x_token_major=False):
    x_spec = _fm_spec(D_MODEL)
    if x_token_major:
        x_spec = pl.BlockSpec((1, TS, D_MODEL), lambda b, s: (b, s, 0))
    extra_specs = [_fm_spec(D_MODEL)] if x_token_major else []
    extra_shapes = [jax.ShapeDtypeStruct((BATCH, D_MODEL, SEQ), F32)] if x_token_major else []
    return pl.pallas_call(
        _nsa_in_kernel,
        grid=(BATCH, SEQ // TS),
        in_specs=[
            x_spec, _col_spec(D_MODEL), _bcol_spec(D_MODEL), _bcol_spec(D_MODEL),
            pl.BlockSpec((NSA_ROWS, D_MODEL), lambda b, s: (0, 0)),
            _col_spec(HEAD_DIM), _col_spec(HEAD_DIM), _col_spec(HEAD_DIM),
            _col_spec(N_KV * GATE_ROWS),
            _fm_spec(ROT_HALF), _fm_spec(ROT_HALF),
        ],
        out_specs=[
            _tile_spec(GQA * HEAD_DIM), _k_chunk_spec(K_AUG), _k_chunk_spec(HEAD_DIM),
            _k_chunk_spec(HEAD_DIM), _k_chunk_spec(HEAD_DIM),
            _V_CHUNK_SPEC, _V_CHUNK_SPEC, _tile_spec(GATE_ROWS), _tile_spec(GQA * HEAD_DIM),
        ] + extra_specs,
        out_shape=[
            _tile_shape(GQA * HEAD_DIM, BF16), _k_chunk_shape(K_AUG), _k_chunk_shape(HEAD_DIM),
            _k_chunk_shape(HEAD_DIM, F32), _k_chunk_shape(HEAD_DIM, F32),
            _V_CHUNK_SHAPE, _V_CHUNK_SHAPE, _tile_shape(GATE_ROWS, F32),
            _tile_shape(GQA * HEAD_DIM, F32),
        ] + extra_shapes,
        compiler_params=_cparams(("parallel", "parallel")),
        name="nsa_in_proj",
    )(x, ng, sc, sh, wT, qg, ksg, kwg, gb, cosT, sinT)


MOBA_OFF_Q = 0
MOBA_OFF_K = 1024
MOBA_OFF_V = 1280
MOBA_OFF_Z = 1536
MOBA_ROWS = 2560
KM_LANES = 128


def _moba_in_kernel(x_ref, ng_ref, sc_ref, sh_ref, w_ref, qg_ref, kg_ref, cos_ref, sin_ref,
                    q_ref, k_ref, km_ref, v_ref, sz_ref):
    hb = _norm_mod(x_ref, ng_ref, sc_ref, sh_ref)
    cos = cos_ref[0]
    sin = sin_ref[0]

    def proj(r0, n):
        return _dot(w_ref[r0:r0 + n, :], hb)

    _project_queries(proj, MOBA_OFF_Q, qg_ref, cos, sin, q_ref)
    y = proj(MOBA_OFF_K, 256)
    lane = lax.broadcasted_iota(jnp.int32, (HEAD_DIM, KM_LANES), 1)
    for gi in range(N_KV):
        yh = _head_norm_rope(y[gi * HEAD_DIM:(gi + 1) * HEAD_DIM], kg_ref[...], cos, sin)
        _store_k_chunks(k_ref, gi, yh, _block_indicator(1))
        km = jnp.zeros((HEAD_DIM, KM_LANES), F32)
        for it in range(TS // TQ):
            mean = jnp.mean(yh[:, it * TQ:(it + 1) * TQ], axis=1, keepdims=True)
            km = jnp.where(lane == it, mean, km)
        km_ref[0, 0, gi * HEAD_DIM:(gi + 1) * HEAD_DIM, :] = km
    _store_v_chunks(v_ref, proj(MOBA_OFF_V, 256))
    _project_silu(proj, MOBA_OFF_Z, sz_ref)


def _moba_in_call(xT, ng, sc, sh, wT, qg, kg, cosT, sinT):
    fm = lambda rows, dt: jax.ShapeDtypeStruct((BATCH, rows, SEQ), dt)
    return pl.pallas_call(
        _moba_in_kernel,
        grid=(BATCH, SEQ // TS),
        in_specs=[
            _fm_spec(D_MODEL), _col_spec(D_MODEL), _bcol_spec(D_MODEL), _bcol_spec(D_MODEL),
            pl.BlockSpec((MOBA_ROWS, D_MODEL), lambda b, s: (0, 0)),
            _col_spec(HEAD_DIM), _col_spec(HEAD_DIM),
            _fm_spec(ROT_HALF), _fm_spec(ROT_HALF),
        ],
        out_specs=[
            _tile_spec(GQA * HEAD_DIM), _k_chunk_spec(K_AUG),
            pl.BlockSpec((1, 1, N_KV * HEAD_DIM, KM_LANES), lambda b, s: (b, s, 0, 0)),
            _V_CHUNK_SPEC, _tile_spec(GQA * HEAD_DIM),
        ],
        out_shape=[
            _tile_shape(GQA * HEAD_DIM, BF16), _k_chunk_shape(K_AUG),
            jax.ShapeDtypeStruct((BATCH, SEQ // TS, N_KV * HEAD_DIM, KM_LANES), F32),
            _V_CHUNK_SHAPE, _tile_shape(GQA * HEAD_DIM, F32),
        ],
        compiler_params=_cparams(("parallel", "parallel")),
        name="moba_in_proj",
    )(xT, ng, sc, sh, wT, qg, kg, cosT, sinT)


CMP_NB = 8
CMP_ROWS = CMP_NB * N_CMP_PAD
CMP_COLS = BATCH * N_KV * N_CMP_PAD
CMP_FEAT = CMP_STRIDE * HEAD_DIM


def _compress_mlp(x_ref, pea_ref, peb_ref, w1a_ref, w1b_ref, b1_ref, w2t_ref):
    x = x_ref[...].reshape(CMP_ROWS, CMP_FEAT)
    u = _dot((x + pea_ref[...]).astype(BF16), w1a_ref[...])
    v = _dot((x + peb_ref[...]).astype(BF16), w1b_ref[...])
    v = pltpu.roll(v, CMP_ROWS - 1, 0)
    h = u + v + b1_ref[...]
    h = 0.5 * h * (1.0 + jnp.tanh(np.sqrt(2.0 / np.pi) * (h + 0.044715 * (h * h * h))))
    return lax.dot_general(w2t_ref[...], h.astype(BF16), (((1,), (1,)), ((), ())),
                           preferred_element_type=F32)


def _compress_key_kernel(x_ref, pea_ref, peb_ref, w1a_ref, w1b_ref, b1_ref, w2t_ref, kg_ref,
                         cos_ref, sin_ref, o_ref):
    out = _compress_mlp(x_ref, pea_ref, peb_ref, w1a_ref, w1b_ref, b1_ref, w2t_ref)
    out = _head_norm_rope(out, kg_ref[...], cos_ref[...], sin_ref[...])
    for i in range(CMP_NB):
        o_ref[i] = out[:, i * N_CMP_PAD:(i + 1) * N_CMP_PAD].T.astype(BF16)


def _compress_value_kernel(x_ref, pea_ref, peb_ref, w1a_ref, w1b_ref, b1_ref, w2t_ref, o_ref):
    out = _compress_mlp(x_ref, pea_ref, peb_ref, w1a_ref, w1b_ref, b1_ref, w2t_ref)
    for i in range(CMP_NB):
        o_ref[i] = out[:, i * N_CMP_PAD:(i + 1) * N_CMP_PAD].astype(BF16)


def _compress_call(x, pe, w1, b1, w2, key_extras=None):
    full = lambda shape: pl.BlockSpec(shape, lambda t: (0,) * len(shape))
    in_specs = [
        pl.BlockSpec((CMP_NB, N_CMP_PAD, CMP_FEAT), lambda t: (t, 0, 0)),
        full((1, CMP_FEAT)), full((1, CMP_FEAT)),
        full((CMP_FEAT, CMP_HID)), full((CMP_FEAT, CMP_HID)),
        full((1, CMP_HID)), full((HEAD_DIM, CMP_HID)),
    ]
    pe = pe.reshape(2, 1, CMP_FEAT)
    w1 = w1.astype(BF16)
    args = [x, pe[0], pe[1], w1[:CMP_FEAT], w1[CMP_FEAT:], b1.reshape(1, CMP_HID), w2.T.astype(BF16)]
    if key_extras is not None:
        in_specs += [full((HEAD_DIM, 1)),
                     pl.BlockSpec((ROT_HALF, CMP_ROWS), lambda t: (0, t)),
                     pl.BlockSpec((ROT_HALF, CMP_ROWS), lambda t: (0, t))]
        args += list(key_extras)
        kernel, out_tail = _compress_key_kernel, (N_CMP_PAD, HEAD_DIM)
    else:
        kernel, out_tail = _compress_value_kernel, (HEAD_DIM, N_CMP_PAD)
    return pl.pallas_call(
        kernel,
        grid=(BATCH * N_KV // CMP_NB,),
        in_specs=in_specs,
        out_specs=pl.BlockSpec((CMP_NB,) + out_tail, lambda t: (t, 0, 0)),
        out_shape=jax.ShapeDtypeStruct((BATCH * N_KV,) + out_tail, BF16),
        compiler_params=_cparams(("parallel",)),
        name="nsa_compress_key" if key_extras is not None else "nsa_compress_value",
    )(*args)


def _group_queries(q_ref):
    return jnp.concatenate(
        [q_ref[0, r * HEAD_DIM:(r + 1) * HEAD_DIM, :] for r in range(GQA)], axis=1)


COL = 128


def _col_tiles():
    per_head = TQ // COL
    for ct in range(LANES_Q // COL):
        r, h = divmod(ct, per_head)
        yield r, slice(h * COL, (h + 1) * COL), slice(ct * COL, (ct + 1) * COL)


def _flash_scratch():
    return [
        pltpu.VMEM((2, TQ, LANES_Q), BF16),
        pltpu.VMEM((2, 1, LANES_Q), F32),
        pltpu.VMEM((1, LANES_Q), F32),
        pltpu.VMEM((V_AUG, LANES_Q), F32),
    ]


class _FlashBranch:
    def __init__(self, q_ref, k_ref, v_ref, bufs, own, past_chunk, own_fix=None, past_fix=None,
                 own_rows=None, past_rows=None):
        self.q_ref, self.k_ref, self.v_ref, self.bufs = q_ref, k_ref, v_ref, bufs
        self.own_chunk, self.past_chunk = own, past_chunk
        self.own_fix, self.past_fix, self.own_rows, self.past_rows = own_fix, past_fix, own_rows, past_rows

    def _softmax_tile(self, s, krows, cs, slot, init):
        p_buf, a_buf, m_ref, _ = self.bufs
        m_loc = jnp.max(s, axis=0, keepdims=True)
        if init:
            m_new = m_loc
            a_buf[slot, :, cs] = jnp.ones((1, COL), F32)
        else:
            m_old = m_ref[:, cs]
            m_new = jnp.maximum(m_old, m_loc)
            a_buf[slot, :, cs] = jnp.exp2(m_old - m_new)
        m_ref[:, cs] = m_new
        p_buf[slot, krows, cs] = jnp.exp2(s - m_new).astype(BF16)
        if krows.start > 0:
            p_buf[slot, 0:krows.start, cs] = jnp.zeros((krows.start, COL), BF16)
        if krows.stop < TQ:
            p_buf[slot, krows.stop:TQ, cs] = jnp.zeros((TQ - krows.stop, COL), BF16)

    def _scores(self, k, r, qs, cs, rows, fix):
        q = self.q_ref[0, r * HEAD_DIM:(r + 1) * HEAD_DIM, qs]
        if rows is not None:
            q = jnp.concatenate([q, rows(cs)], axis=0)
        krows = slice(0, TQ) if fix is None else fix.key_rows(qs)
        s = _dot(k[krows, :], q)
        if fix is not None:
            s = fix(s, qs, krows)
        return s, krows

    def _pv_slab(self, vT, slot, ct):
        if (ct * COL) % PV_COL:
            return
        p_buf, a_buf, _, acc_ref = self.bufs
        cs = slice(ct * COL, ct * COL + PV_COL)
        acc_ref[:, cs] = a_buf[slot, :, cs] * acc_ref[:, cs] + _dot(vT, p_buf[slot, :, cs])

    def own(self):
        acc_ref = self.bufs[3]
        acc_ref[...] = jnp.zeros(acc_ref.shape, F32)
        k_own = self.k_ref[0, 0, self.own_chunk]
        for r, qs, cs in _col_tiles():
            s, krows = self._scores(k_own, r, qs, cs, self.own_rows, self.own_fix)
            self._softmax_tile(s, krows, cs, 0, True)

    def trip(self, j, cur, prv):
        k = self.k_ref[0, 0, self.past_chunk(j)]
        vT = self.v_ref[0, 0, self._chunk_before(j)]
        rows = None if self.past_rows is None else self.past_rows(j)
        for ct, (r, qs, cs) in enumerate(_col_tiles()):
            s, krows = self._scores(k, r, qs, cs, rows, self.past_fix)
            self._pv_slab(vT, prv, ct)
            self._softmax_tile(s, krows, cs, cur, False)

    def run_past(self, n_past):
        if isinstance(n_past, int):
            for j in range(n_past):
                self.trip(j, 1 - j % 2, j % 2)
            return

        def pair(pp, carry):
            self.trip(2 * pp, 1, 0)
            self.trip(2 * pp + 1, 0, 1)
            return carry

        lax.fori_loop(0, n_past // 2, pair, 0)

        @pl.when(n_past % 2 == 1)
        def _():
            self.trip(n_past - 1, 1, 0)

    def _chunk_before(self, j):
        if isinstance(j, int):
            return self.own_chunk if j == 0 else self.past_chunk(j - 1)
        return jnp.where(j == 0, self.own_chunk, self.past_chunk(jnp.maximum(j - 1, 0)))

    def finish(self, n_past):
        vT = self.v_ref[0, 0, self._chunk_before(n_past)]
        for ct in range(LANES_Q // COL):
            self._pv_slab(vT, n_past % 2, ct)
        acc = self.bufs[3][...]
        return acc[:HEAD_DIM] / acc[HEAD_DIM:HEAD_DIM + 1]


class _CausalFix:
    def __init__(self, keep_lower):
        self.keep_lower = keep_lower

    def key_rows(self, qs):
        half = TQ // 2
        if self.keep_lower and qs.stop <= half:
            return slice(0, half)
        if not self.keep_lower and qs.start >= half:
            return slice(half, TQ)
        return slice(0, TQ)

    def __call__(self, s, qs, rows):
        n = rows.stop - rows.start
        a_idx = lax.broadcasted_iota(jnp.int32, (n, COL), 0) + rows.start
        lane = lax.broadcasted_iota(jnp.int32, (n, COL), 1) + qs.start
        return jnp.where(a_idx <= lane if self.keep_lower else a_idx > lane, s, NEG)


def _causal_fix(keep_lower):
    return _CausalFix(keep_lower)


def _rank_select(score_ref, n_rows, j_idx, count):
    score = score_ref[...]
    cnt = jnp.zeros(score.shape, jnp.int32)
    for jp in range(n_rows):
        row = score_ref[jp:jp + 1, :]
        beats = (row > score) | ((row == score) & (jp < j_idx))
        cnt = cnt + beats.astype(jnp.int32)
    return cnt < count


class _TileView:
    def __init__(self, ref, tile):
        self.ref, self.tile = ref, tile

    def __getitem__(self, idx):
        return self.ref[(0, 0, self.tile) + tuple(idx[1:])]

    def __setitem__(self, idx, value):
        self.ref[(0, 0, self.tile) + tuple(idx[1:])] = value


SCRATCH_SETS = 2


def _for_each_query_tile(tile_fn, tiled_refs, other_refs, scratch_refs):
    per_set = len(scratch_refs) // SCRATCH_SETS

    def start(qi):
        first = (qi % SCRATCH_SETS) * per_set
        gen = tile_fn(qi, *[_TileView(ref, qi) for ref in tiled_refs], *other_refs,
                      *scratch_refs[first:first + per_set])
        next(gen)
        return gen

    gen = start(0)
    for qi in range(N_CHUNK):
        next(gen)
        following = start(qi + 1) if qi + 1 < N_CHUNK else None
        for _ in gen:
            pass
        gen = following


def _store_out(o_ref, sz_ref, o):
    for r in range(GQA):
        rows = slice(r * HEAD_DIM, (r + 1) * HEAD_DIM)
        o_ref[0, rows, :] = (o[:, r * TQ:(r + 1) * TQ] * sz_ref[0, rows, :]).astype(BF16)


def _nsa_attn_kernel(q_ref, kc_ref, vc_ref, ks_ref, vs_ref, kw_ref, vw_ref, gt_ref, sz_ref,
                     selw_ref, o_ref, *scratch):
    _for_each_query_tile(_nsa_attn_tile, (q_ref, gt_ref, sz_ref, o_ref),
                         (kc_ref, vc_ref, ks_ref, vs_ref, kw_ref, vw_ref, selw_ref), scratch)


def _nsa_attn_tile(qi, q_ref, gt_ref, sz_ref, o_ref, kc_ref, vc_ref, ks_ref, vs_ref, kw_ref, vw_ref,
                   selw_ref, score_ref, bias_ref, *bufs):
    q4 = _group_queries(q_ref)

    sel_bufs, win_bufs = bufs[:len(bufs) // 2], bufs[len(bufs) // 2:]
    keep_lower = _causal_fix(True)

    n_win = min(qi, 1)
    win = _FlashBranch(q_ref, kw_ref, vw_ref, win_bufs, qi, lambda j: qi - 1,
                       own_fix=keep_lower, past_fix=_causal_fix(False))

    t = qi * TQ + (lax.broadcasted_iota(jnp.int32, (1, LANES_Q), 1) & (TQ - 1))
    n_idx = lax.broadcasted_iota(jnp.int32, (N_CMP_PAD, LANES_Q), 0)
    s_cmp = _dot(kc_ref[0, 0], q4)
    win.own()
    s = jnp.where(n_idx * CMP_STRIDE + (CMP_LEN - 1) <= t, s_cmp, NEG)
    m = jnp.max(s, axis=0, keepdims=True)
    p = jnp.exp2(s - m) * (m > 0.5 * NEG).astype(F32)
    p = p / jnp.maximum(jnp.sum(p, axis=0, keepdims=True), 1e-30)
    o_cmp = _dot(vc_ref[0, 0], p.astype(BF16))

    psum = p[:, 0:TQ]
    for r in range(1, GQA):
        psum = psum + p[:, r * TQ:(r + 1) * TQ]
    p_hi = psum.astype(BF16)
    p_lo = (psum - p_hi.astype(F32)).astype(BF16)
    imp = _dot(selw_ref[...], p_hi) + _dot(selw_ref[...], p_lo)
    j_idx = lax.broadcasted_iota(jnp.int32, (N_SEL, TQ), 0)
    cur = qi * (TQ // SEL_BLOCK) + (lax.broadcasted_iota(jnp.int32, (N_SEL, TQ), 1) >> 6)
    valid = j_idx <= cur
    forced = (j_idx == 0) | (j_idx == cur) | (j_idx == cur - 1)
    score_ref[...] = jnp.where(forced, jnp.inf, jnp.where(valid, imp, -jnp.inf))
    win.run_past(n_win)
    sel = valid & _rank_select(score_ref, N_SEL, j_idx, SEL_COUNT)
    bias = jnp.where(sel, 0.0, NEG)
    bias = jnp.concatenate([bias] * GQA, axis=1)
    per_chunk = TQ // SEL_BLOCK
    bias_ref[...] = jnp.zeros(bias_ref.shape, F32)
    for jp in range(N_SEL):
        bias_ref[jp // per_chunk, jp % per_chunk:jp % per_chunk + 1, :] = bias[jp:jp + 1, :]

    o_win = win.finish(n_win)
    yield

    def sel_rows(j):
        return lambda cs: bias_ref[j, :, cs].astype(BF16)

    slc = _FlashBranch(q_ref, ks_ref, vs_ref, sel_bufs, qi, lambda j: j, own_fix=keep_lower,
                       own_rows=sel_rows(qi), past_rows=sel_rows)
    slc.own()
    yield
    slc.run_past(qi)
    o_slc = slc.finish(qi)

    def gate(br):
        return jnp.concatenate(
            [gt_ref[0, br * GQA + r:br * GQA + r + 1, :] for r in range(GQA)], axis=1)

    o = gate(0) * o_cmp + gate(1) * o_slc + gate(2) * o_win
    _store_out(o_ref, sz_ref, o)


def _group_spec(*tail):
    return pl.BlockSpec((1, 1) + tail, lambda b, g: (b, g) + (0,) * len(tail))


def _attn_specs():
    q_spec = _group_spec(N_CHUNK, GQA * HEAD_DIM, TQ)
    k_spec = lambda width: _group_spec(N_CHUNK, TQ, width)
    v_spec = _group_spec(N_CHUNK, V_AUG, TQ)
    return q_spec, k_spec, v_spec


def _nsa_attn_call(q, kc, vc, ks, vs, kw, vw, gt, sz, selw):
    q_spec, k_spec, v_spec = _attn_specs()
    return pl.pallas_call(
        _nsa_attn_kernel,
        grid=(BATCH, N_KV),
        in_specs=[
            q_spec,
            _group_spec(N_CMP_PAD, HEAD_DIM), _group_spec(HEAD_DIM, N_CMP_PAD),
            k_spec(K_AUG), v_spec, k_spec(HEAD_DIM), v_spec,
            _group_spec(N_CHUNK, GATE_ROWS, TQ),
            q_spec,
            pl.BlockSpec((N_SEL, N_CMP_PAD), lambda b, g: (0, 0)),
        ],
        out_specs=q_spec,
        out_shape=_tile_shape(GQA * HEAD_DIM, BF16),
        scratch_shapes=SCRATCH_SETS * ([
            pltpu.VMEM((N_SEL, TQ), F32),
            pltpu.VMEM((N_CHUNK, BIAS_ROWS, LANES_Q), F32),
        ] + _flash_scratch() + _flash_scratch()),
        compiler_params=_cparams(("parallel", "parallel")),
        name="nsa_attention",
    )(q, kc, vc, ks, vs, kw, vw, gt, sz, selw)


def _moba_attn_kernel(q_ref, km_ref, k_ref, v_ref, sz_ref, o_ref,
                      *scratch):
    _for_each_query_tile(_moba_attn_tile, (q_ref, sz_ref, o_ref), (km_ref, k_ref, v_ref), scratch)


def _moba_attn_tile(qi, q_ref, sz_ref, o_ref, km_ref, k_ref, v_ref, score_ref, bias_ref, *bufs):
    q4 = _group_queries(q_ref)

    j_idx = lax.broadcasted_iota(jnp.int32, (N_CHUNK, LANES_Q), 0)
    past = j_idx < qi
    score_ref[...] = jnp.where(past, _dot(km_ref[0, 0], q4), -jnp.inf)
    sel = past & _rank_select(score_ref, N_CHUNK, j_idx, MOBA_TOPK)
    bias = jnp.where(sel, 0.0, NEG)
    bias_ref[...] = jnp.zeros(bias_ref.shape, F32)
    for jp in range(N_CHUNK):
        bias_ref[jp, 0:1, :] = bias[jp:jp + 1, :]

    yield

    branch = _FlashBranch(q_ref, k_ref, v_ref, bufs, qi, lambda j: j, own_fix=_causal_fix(True),
                          own_rows=lambda cs: jnp.zeros((BIAS_ROWS, COL), BF16),
                          past_rows=lambda j: (lambda cs: bias_ref[j, :, cs].astype(BF16)))
    branch.own()
    yield
    branch.run_past(qi)
    _store_out(o_ref, sz_ref, branch.finish(qi))


def _moba_attn_call(q, km, k, v, sz):
    q_spec, k_spec, v_spec = _attn_specs()
    return pl.pallas_call(
        _moba_attn_kernel,
        grid=(BATCH, N_KV),
        in_specs=[
            q_spec,
            _group_spec(N_CHUNK, HEAD_DIM),
            k_spec(K_AUG), v_spec, q_spec,
        ],
        out_specs=q_spec,
        out_shape=_tile_shape(GQA * HEAD_DIM, BF16),
        scratch_shapes=SCRATCH_SETS * ([
            pltpu.VMEM((N_CHUNK, LANES_Q), F32),
            pltpu.VMEM((N_CHUNK, BIAS_ROWS, LANES_Q), F32),
        ] + _flash_scratch()),
        compiler_params=_cparams(("parallel", "parallel")),
        name="moba_attention",
    )(q, km, k, v, sz)


def _out_kernel(oz_ref, w_ref, x_ref, gt_ref, o_ref, *, token_major_out):
    group_rows = GQA * HEAD_DIM
    for it in range(TS // TQ):
        lanes = slice(it * TQ, (it + 1) * TQ)
        for cblk in range(D_MODEL // 256):
            rows = slice(cblk * 256, (cblk + 1) * 256)
            y = _dot(w_ref[rows, 0:group_rows], oz_ref[0, 0, it])
            for gi in range(1, N_KV):
                y = y + _dot(w_ref[rows, gi * group_rows:(gi + 1) * group_rows], oz_ref[0, gi, it])
            new_x = x_ref[0, rows, lanes] + gt_ref[0, rows, :] * y
            if token_major_out:
                o_ref[0, lanes, rows] = new_x.T
            else:
                o_ref[0, rows, lanes] = new_x


def _out_call(oz, wT, xT, gate, token_major_out=False):
    if token_major_out:
        out_spec = pl.BlockSpec((1, TS, D_MODEL), lambda b, s: (b, s, 0))
        out_shape = jax.ShapeDtypeStruct((BATCH, SEQ, D_MODEL), F32)
    else:
        out_spec = _fm_spec(D_MODEL)
        out_shape = jax.ShapeDtypeStruct((BATCH, D_MODEL, SEQ), F32)
    return pl.pallas_call(
        functools.partial(_out_kernel, token_major_out=token_major_out),
        grid=(BATCH, SEQ // TS),
        in_specs=[
            _tile_spec(GQA * HEAD_DIM),
            pl.BlockSpec((D_MODEL, D_MODEL), lambda b, s: (0, 0)),
            _fm_spec(D_MODEL), _bcol_spec(D_MODEL),
        ],
        out_specs=out_spec,
        out_shape=out_shape,
        compiler_params=_cparams(("parallel", "parallel")),
        name="out_proj",
    )(oz, wT, xT, gate)


def _rope_tables(pos):
    inv_freq = ROPE_THETA ** (-jnp.arange(0, 2 * ROT_HALF, 2, dtype=F32) / (2 * ROT_HALF))
    ang = pos.astype(F32)[..., None] * inv_freq
    return jnp.cos(ang), jnp.sin(ang)


def _gate_perm():
    perm = np.full((N_KV * GATE_ROWS,), 3 * N_HEADS, dtype=np.int32)
    for g in range(N_KV):
        for br in range(3):
            for r in range(GQA):
                perm[g * GATE_ROWS + br * GQA + r] = (g * GQA + r) * 3 + br
    return perm


def _sel_weights_T():
    cs = np.arange(N_CMP)[:, None] * CMP_STRIDE
    ss = np.arange(N_SEL)[None, :] * SEL_BLOCK
    shared = np.clip(np.minimum(cs + CMP_LEN, ss + SEL_BLOCK) - np.maximum(cs, ss), 0, None)
    w = np.zeros((N_CMP_PAD, N_SEL), np.float32)
    w[:N_CMP] = shared / CMP_LEN
    return jnp.asarray(w.T, dtype=BF16)


def _col(v):
    return v.reshape(-1, 1)


def _nsa_layer(xT, ng, sc, sh, gate, cosT, sinT, cosc, sinc, w_in, w_out, q_g, k_g,
               cmp_pe, cmp_w1, cmp_b1, cmp_w2, gate_b, x_token_major=False):
    widths = [1024] + [256] * 6 + [3 * N_HEADS, 1024]
    q, kc, vc, ks, vs, kw, vw, gl, z = jnp.split(w_in, np.cumsum(widths)[:-1].tolist(), axis=1)
    perm = _gate_perm()
    gl_p = jnp.concatenate([gl, jnp.zeros((D_MODEL, 1), F32)], axis=1)[:, perm]
    gb_p = jnp.concatenate([gate_b, jnp.zeros((1,), F32)])[perm]
    wT = jnp.concatenate([q, ks, kw, kc, vc, vs, vw, gl_p, z], axis=1).T.astype(BF16)

    outs = _nsa_in_call(xT, ng, sc, sh, wT, _col(q_g), _col(k_g[1]), _col(k_g[2]), _col(gb_p),
                        cosT, sinT, x_token_major=x_token_major)
    qT, ks_c, kw_c, kc_c, vc_c, vs_c, vw_c, gates, sz = outs[:9]
    if x_token_major:
        xT = outs[9]

    half_blocks = lambda t: t.reshape(BATCH * N_KV, N_CMP_PAD, CMP_FEAT)
    k_cmp = _compress_call(half_blocks(kc_c), cmp_pe[0], cmp_w1[0], cmp_b1[0], cmp_w2[0],
                           key_extras=(_col(k_g[0]), cosc, sinc))
    v_cmpT = _compress_call(half_blocks(vc_c), cmp_pe[1], cmp_w1[1], cmp_b1[1], cmp_w2[1])
    k_cmp = k_cmp.reshape(BATCH, N_KV, N_CMP_PAD, HEAD_DIM)
    v_cmpT = v_cmpT.reshape(BATCH, N_KV, HEAD_DIM, N_CMP_PAD)

    oz = _nsa_attn_call(qT, k_cmp, v_cmpT, ks_c, vs_c, kw_c, vw_c, gates, sz, _sel_weights_T())
    return _out_call(oz, w_out.T.astype(BF16), xT, gate)


def _moba_layer(xT, ng, sc, sh, gate, cosT, sinT, w_in, w_out, q_g, k_g, token_major_out=False):
    wT = w_in.T.astype(BF16)
    qT, k_c, km, v_c, sz = _moba_in_call(xT, ng, sc, sh, wT, _col(q_g), _col(k_g), cosT, sinT)
    nb = TS // TQ
    km = km[..., :nb].reshape(BATCH, SEQ // TS, N_KV, HEAD_DIM, nb)
    km = km.transpose(0, 2, 1, 4, 3).reshape(BATCH, N_KV, N_CHUNK, HEAD_DIM).astype(BF16)
    oz = _moba_attn_call(qT, km, k_c, v_c, sz)
    return _out_call(oz, w_out.T.astype(BF16), xT, gate, token_major_out)


@jax.jit
def _forward(x, c, positions, norm_g, ada_w, ada_b, nsa_w_in, nsa_w_out, nsa_q_norm, nsa_k_norm,
             nsa_cmp_pe, nsa_cmp_w1, nsa_cmp_b1, nsa_cmp_w2, nsa_gate_b,
             moba_w_in, moba_w_out, moba_q_norm, moba_k_norm):
    cos, sin = _rope_tables(positions)
    cosT = cos.transpose(0, 2, 1)
    sinT = sin.transpose(0, 2, 1)
    cmp_end = np.minimum(np.arange(N_CMP_PAD) * CMP_STRIDE + CMP_LEN - 1, SEQ - 1)
    cos_c, sin_c = _rope_tables(positions[:, cmp_end])

    def cmp_table(t):
        t = jnp.broadcast_to(t.transpose(2, 0, 1)[:, :, None, :],
                             (ROT_HALF, BATCH, N_KV, N_CMP_PAD))
        return t.reshape(ROT_HALF, CMP_COLS)

    cosc = cmp_table(cos_c)
    sinc = cmp_table(sin_c)

    mod = _ada_call(c, ada_w, ada_b)
    shift, scale, gate = jnp.split(mod[..., None], 3, axis=2)

    xT = x
    for i in range(DEPTH):
        j = i // 2
        ng = _col(norm_g[i])
        if i % 2 == 0:
            xT = _nsa_layer(xT, ng, scale[i], shift[i], gate[i], cosT, sinT, cosc, sinc,
                            nsa_w_in[j], nsa_w_out[j], nsa_q_norm[j], nsa_k_norm[j],
                            nsa_cmp_pe[j], nsa_cmp_w1[j], nsa_cmp_b1[j], nsa_cmp_w2[j],
                            nsa_gate_b[j], x_token_major=(i == 0))
        else:
            xT = _moba_layer(xT, ng, scale[i], shift[i], gate[i], cosT, sinT,
                             moba_w_in[j], moba_w_out[j], moba_q_norm[j], moba_k_norm[j],
                             token_major_out=(i == DEPTH - 1))
    return xT


def kernel(x, c, positions, norm_g, ada_w, ada_b, nsa_w_in, nsa_w_out, nsa_q_norm, nsa_k_norm, nsa_cmp_pe, nsa_cmp_w1, nsa_cmp_b1, nsa_cmp_w2, nsa_gate_b, moba_w_in, moba_w_out, moba_q_norm, moba_k_norm):
    return _forward(x, c, positions, norm_g, ada_w, ada_b, nsa_w_in, nsa_w_out, nsa_q_norm,
                    nsa_k_norm, nsa_cmp_pe, nsa_cmp_w1, nsa_cmp_b1, nsa_cmp_w2, nsa_gate_b,
                    moba_w_in, moba_w_out, moba_q_norm, moba_k_norm)
```

```python
import functools

import numpy as np
import jax
import jax.numpy as jnp
from jax import lax
from jax.experimental import pallas as pl
from jax.experimental.pallas import tpu as pltpu

D_MODEL = 1024
BATCH = 16
SEQ = 2048
DEPTH = 4
HEAD_DIM = 64
N_HEADS = 16
N_KV = 4
GQA = 4
ROT_HALF = 8
ROPE_THETA = 500000.0
NORM_EPS = 1e-6
CMP_LEN = 32
CMP_STRIDE = 16
CMP_HID = 256
SEL_BLOCK = 64
SEL_COUNT = 8
N_SEL = SEQ // SEL_BLOCK
N_CMP = (SEQ - CMP_LEN) // CMP_STRIDE + 1
N_CMP_PAD = 128
MOBA_TOPK = 3

TQ = 256
N_CHUNK = SEQ // TQ
LANES_Q = GQA * TQ
TS = 512
NEG = -1e30
BIAS_ROWS = 16
K_AUG = HEAD_DIM + BIAS_ROWS
V_AUG = HEAD_DIM + 16
PV_COL = 256
Q_SCALE = HEAD_DIM ** -0.5 * float(np.log2(np.e))
GATE_ROWS = 16

F32 = jnp.float32
BF16 = jnp.bfloat16

VMEM_LIMIT = 52 * 1024 * 1024


def _cparams(sem):
    return pltpu.CompilerParams(dimension_semantics=sem, vmem_limit_bytes=VMEM_LIMIT)


def _dot(a, b):
    return jnp.dot(a, b, preferred_element_type=F32)


def _ada_kernel(c_ref, w_ref, b_ref, o_ref):
    cond = c_ref[...]
    cond = cond * jax.nn.sigmoid(cond)
    o_ref[0] = jnp.dot(cond, w_ref[0], precision=lax.Precision.HIGHEST,
                       preferred_element_type=F32) + b_ref[0]


def _ada_call(c, ada_w, ada_b):
    nt = 1024
    return pl.pallas_call(
        _ada_kernel,
        grid=(DEPTH, 3 * D_MODEL // nt),
        in_specs=[
            pl.BlockSpec((BATCH, D_MODEL), lambda i, n: (0, 0)),
            pl.BlockSpec((1, D_MODEL, nt), lambda i, n: (i, 0, n)),
            pl.BlockSpec((1, 1, nt), lambda i, n: (i, 0, n)),
        ],
        out_specs=pl.BlockSpec((1, BATCH, nt), lambda i, n: (i, 0, n)),
        out_shape=jax.ShapeDtypeStruct((DEPTH, BATCH, 3 * D_MODEL), F32),
        compiler_params=_cparams(("parallel", "parallel")),
        name="ada_mod",
    )(c, ada_w, ada_b.reshape(DEPTH, 1, 3 * D_MODEL))


def _input_tile(refs, source):
    if source == "pending":
        oz_ref, wo_ref, gate_ref, x_ref, *rest = refs
        *rest, new_x_ref = rest

        def store(rows, lanes, value):
            new_x_ref[0, rows, lanes] = value

        _residual_update(oz_ref, wo_ref, x_ref, gate_ref, store)
        return new_x_ref[0], rest
    assert source == "tokens"
    x_ref, *rest, xT_ref = refs
    x = x_ref[0].T
    xT_ref[0] = x
    return x, rest


def _input_specs(source):
    if source == "pending":
        return _pending_specs() + [_fm_spec(D_MODEL)]
    return [pl.BlockSpec((1, TS, D_MODEL), lambda b, s: (b, s, 0))]


def _norm_mod(x, ng_ref, sc_ref, sh_ref):
    ms = jnp.mean(x * x, axis=0, keepdims=True)
    y = x * lax.rsqrt(ms + NORM_EPS)
    h = (y * ng_ref[...]) * (1.0 + sc_ref[0]) + sh_ref[0]
    return h.astype(BF16)


def _head_norm_rope(y, g, cos, sin):
    ms = jnp.mean(y * y, axis=0, keepdims=True)
    yn = (y * lax.rsqrt(ms + NORM_EPS)) * g
    x1 = yn[0:ROT_HALF]
    x2 = yn[ROT_HALF:2 * ROT_HALF]
    return jnp.concatenate([x1 * cos - x2 * sin, x2 * cos + x1 * sin, yn[2 * ROT_HALF:]], axis=0)


def _store_k_chunks(o_ref, gi, yh, ind=None):
    for it in range(TS // TQ):
        chunk = yh[:, it * TQ:(it + 1) * TQ]
        if ind is None:
            o_ref[0, gi, it] = chunk.T.astype(BF16)
        else:
            aug = jnp.concatenate([chunk, ind], axis=0).T
            o_ref[0, gi, it] = aug[:, :K_AUG].astype(BF16)


def _store_tiles(o_ref, gi, rows, y):
    for it in range(TS // TQ):
        o_ref[0, gi, it, rows, :] = y[:, it * TQ:(it + 1) * TQ]


def _project_queries(proj, off, qg_ref, cos, sin, q_ref):
    for gi in range(N_KV):
        y = proj(off + gi * 256, 256)
        for r in range(GQA):
            rows = slice(r * HEAD_DIM, (r + 1) * HEAD_DIM)
            yh = _head_norm_rope(y[rows], qg_ref[...], cos, sin)
            _store_tiles(q_ref, gi, rows, (yh * Q_SCALE).astype(BF16))


def _project_silu(proj, off, sz_ref):
    for gi in range(N_KV):
        z = proj(off + gi * 256, 256)
        _store_tiles(sz_ref, gi, slice(None), z * jax.nn.sigmoid(z))


def _block_indicator(blocks):
    row = lax.broadcasted_iota(jnp.int32, (HEAD_DIM, TQ), 0)
    lane = lax.broadcasted_iota(jnp.int32, (HEAD_DIM, TQ), 1)
    return (lane // (TQ // blocks) == row).astype(F32)


def _store_v_chunks(o_ref, y):
    yb = y.astype(BF16)
    row = lax.broadcasted_iota(jnp.int32, (V_AUG - HEAD_DIM, TQ), 0)
    ones_row = (row == 0).astype(BF16)
    for gi in range(N_KV):
        for it in range(TS // TQ):
            v = yb[gi * HEAD_DIM:(gi + 1) * HEAD_DIM, it * TQ:(it + 1) * TQ]
            o_ref[0, gi, it] = jnp.concatenate([v, ones_row], axis=0)


NSA_OFF_Q = 0
NSA_OFF_KS = 1024
NSA_OFF_KW = 1280
NSA_OFF_KC = 1536
NSA_OFF_VC = 1792
NSA_OFF_VS = 2048
NSA_OFF_VW = 2304
NSA_OFF_GT = 2560
NSA_OFF_Z = NSA_OFF_GT + N_KV * GATE_ROWS
NSA_ROWS = NSA_OFF_Z + D_MODEL


def _nsa_in_kernel(*refs, source):
    x, refs = _input_tile(refs, source)
    (ng_ref, sc_ref, sh_ref, w_ref, qg_ref, ksg_ref, kwg_ref, gb_ref, cos_ref, sin_ref,
     q_ref, ks_ref, kw_ref, kc_ref, vc_ref, vs_ref, vw_ref, gt_ref, sz_ref) = refs
    hb = _norm_mod(x, ng_ref, sc_ref, sh_ref)
    cos = cos_ref[0]
    sin = sin_ref[0]

    def proj(r0, n):
        return _dot(w_ref[r0:r0 + n, :], hb)

    _project_queries(proj, NSA_OFF_Q, qg_ref, cos, sin, q_ref)
    sel_ind = _block_indicator(TQ // SEL_BLOCK)
    for off, g_ref, o_ref, ind in ((NSA_OFF_KS, ksg_ref, ks_ref, sel_ind),
                                   (NSA_OFF_KW, kwg_ref, kw_ref, None)):
        y = proj(off, 256)
        for gi in range(N_KV):
            yh = _head_norm_rope(y[gi * HEAD_DIM:(gi + 1) * HEAD_DIM], g_ref[...], cos, sin)
            _store_k_chunks(o_ref, gi, yh, ind)
    for off, o_ref in ((NSA_OFF_KC, kc_ref), (NSA_OFF_VC, vc_ref)):
        y = proj(off, 256)
        for gi in range(N_KV):
            for it in range(TS // TQ):
                o_ref[0, gi, it] = y[gi * HEAD_DIM:(gi + 1) * HEAD_DIM, it * TQ:(it + 1) * TQ].T
    _store_v_chunks(vs_ref, proj(NSA_OFF_VS, 256))
    _store_v_chunks(vw_ref, proj(NSA_OFF_VW, 256))
    gates = jax.nn.sigmoid(proj(NSA_OFF_GT, N_KV * GATE_ROWS) + gb_ref[...])
    for gi in range(N_KV):
        _store_tiles(gt_ref, gi, slice(None), gates[gi * GATE_ROWS:(gi + 1) * GATE_ROWS])
    _project_silu(proj, NSA_OFF_Z, sz_ref)


def _col_spec(rows):
    return pl.BlockSpec((rows, 1), lambda b, s: (0, 0))


def _bcol_spec(rows):
    return pl.BlockSpec((1, rows, 1), lambda b, s: (b, 0, 0))


def _fm_spec(rows):
    return pl.BlockSpec((1, rows, TS), lambda b, s: (b, 0, s))


def _k_chunk_spec(width):
    return pl.BlockSpec((1, N_KV, TS // TQ, TQ, width), lambda b, s: (b, 0, s, 0, 0))


def _k_chunk_shape(width, dtype=BF16):
    return jax.ShapeDtypeStruct((BATCH, N_KV, N_CHUNK, TQ, width), dtype)


def _tile_spec(rows):
    return pl.BlockSpec((1, N_KV, TS // TQ, rows, TQ), lambda b, s: (b, 0, s, 0, 0))


def _tile_shape(rows, dtype):
    return jax.ShapeDtypeStruct((BATCH, N_KV, N_CHUNK, rows, TQ), dtype)


_V_CHUNK_SPEC = pl.BlockSpec((1, N_KV, TS // TQ, V_AUG, TQ), lambda b, s: (b, 0, s, 0, 0))
_V_CHUNK_SHAPE = jax.ShapeDtypeStruct((BATCH, N_KV, N_CHUNK, V_AUG, TQ), BF16)


def _resident_spec(shape):
    return pl.BlockSpec(shape, lambda b, s: (0,) * len(shape), pipeline_mode=pl.Buffered(1))


def _residual_out():
    return [_fm_spec(D_MODEL)], [jax.ShapeDtypeStruct((BATCH, D_MODEL, SEQ), F32)]


def _nsa_in_call(x_args, ng, sc, sh, wT, qg, ksg, kwg, gb, cosT, sinT, source):
    x_specs = _input_specs(source)
    extra_specs, extra_shapes = _residual_out()
    return pl.pallas_call(
        functools.partial(_nsa_in_kernel, source=source),
        grid=(BATCH, SEQ // TS),
        in_specs=x_specs + [
            _col_spec(D_MODEL), _bcol_spec(D_MODEL), _bcol_spec(D_MODEL),
            _resident_spec((NSA_ROWS, D_MODEL)),
            _col_spec(HEAD_DIM), _col_spec(HEAD_DIM), _col_spec(HEAD_DIM),
            _col_spec(N_KV * GATE_ROWS),
            _fm_spec(ROT_HALF), _fm_spec(ROT_HALF),
        ],
        out_specs=[
            _tile_spec(GQA * HEAD_DIM), _k_chunk_spec(K_AUG), _k_chunk_spec(HEAD_DIM),
            _k_chunk_spec(HEAD_DIM), _k_chunk_spec(HEAD_DIM),
            _V_CHUNK_SPEC, _V_CHUNK_SPEC, _tile_spec(GATE_ROWS), _tile_spec(GQA * HEAD_DIM),
        ] + extra_specs,
        out_shape=[
            _tile_shape(GQA * HEAD_DIM, BF16), _k_chunk_shape(K_AUG), _k_chunk_shape(HEAD_DIM),
            _k_chunk_shape(HEAD_DIM, F32), _k_chunk_shape(HEAD_DIM, F32),
            _V_CHUNK_SHAPE, _V_CHUNK_SHAPE, _tile_shape(GATE_ROWS, F32),
            _tile_shape(GQA * HEAD_DIM, F32),
        ] + extra_shapes,
        compiler_params=_cparams(("parallel", "parallel")),
        name="nsa_in_proj",
    )(*x_args, ng, sc, sh, wT, qg, ksg, kwg, gb, cosT, sinT)


MOBA_OFF_Q = 0
MOBA_OFF_K = 1024
MOBA_OFF_V = 1280
MOBA_OFF_Z = 1536
MOBA_ROWS = 2560
KM_LANES = 128


def _moba_in_kernel(*refs, source):
    x, refs = _input_tile(refs, source)
    (ng_ref, sc_ref, sh_ref, w_ref, qg_ref, kg_ref, cos_ref, sin_ref,
     q_ref, k_ref, km_ref, v_ref, sz_ref) = refs
    hb = _norm_mod(x, ng_ref, sc_ref, sh_ref)
    cos = cos_ref[0]
    sin = sin_ref[0]

    def proj(r0, n):
        return _dot(w_ref[r0:r0 + n, :], hb)

    _project_queries(proj, MOBA_OFF_Q, qg_ref, cos, sin, q_ref)
    y = proj(MOBA_OFF_K, 256)
    lane = lax.broadcasted_iota(jnp.int32, (HEAD_DIM, KM_LANES), 1)
    for gi in range(N_KV):
        yh = _head_norm_rope(y[gi * HEAD_DIM:(gi + 1) * HEAD_DIM], kg_ref[...], cos, sin)
        _store_k_chunks(k_ref, gi, yh, _block_indicator(1))
        km = jnp.zeros((HEAD_DIM, KM_LANES), F32)
        for it in range(TS // TQ):
            mean = jnp.mean(yh[:, it * TQ:(it + 1) * TQ], axis=1, keepdims=True)
            km = jnp.where(lane == it, mean, km)
        km_ref[0, 0, gi * HEAD_DIM:(gi + 1) * HEAD_DIM, :] = km
    _store_v_chunks(v_ref, proj(MOBA_OFF_V, 256))
    _project_silu(proj, MOBA_OFF_Z, sz_ref)


def _moba_in_call(x_args, ng, sc, sh, wT, qg, kg, cosT, sinT, source):
    x_specs = _input_specs(source)
    extra_specs, extra_shapes = _residual_out()
    return pl.pallas_call(
        functools.partial(_moba_in_kernel, source=source),
        grid=(BATCH, SEQ // TS),
        in_specs=x_specs + [
            _col_spec(D_MODEL), _bcol_spec(D_MODEL), _bcol_spec(D_MODEL),
            _resident_spec((MOBA_ROWS, D_MODEL)),
            _col_spec(HEAD_DIM), _col_spec(HEAD_DIM),
            _fm_spec(ROT_HALF), _fm_spec(ROT_HALF),
        ],
        out_specs=[
            _tile_spec(GQA * HEAD_DIM), _k_chunk_spec(K_AUG),
            pl.BlockSpec((1, 1, N_KV * HEAD_DIM, KM_LANES), lambda b, s: (b, s, 0, 0)),
            _V_CHUNK_SPEC, _tile_spec(GQA * HEAD_DIM),
        ] + extra_specs,
        out_shape=[
            _tile_shape(GQA * HEAD_DIM, BF16), _k_chunk_shape(K_AUG),
            jax.ShapeDtypeStruct((BATCH, SEQ // TS, N_KV * HEAD_DIM, KM_LANES), F32),
            _V_CHUNK_SHAPE, _tile_shape(GQA * HEAD_DIM, F32),
        ] + extra_shapes,
        compiler_params=_cparams(("parallel", "parallel")),
        name="moba_in_proj",
    )(*x_args, ng, sc, sh, wT, qg, kg, cosT, sinT)


CMP_NB = 8
CMP_ROWS = CMP_NB * N_CMP_PAD
CMP_COLS = BATCH * N_KV * N_CMP_PAD
CMP_FEAT = CMP_STRIDE * HEAD_DIM


def _compress_mlp(x_ref, pea_ref, peb_ref, w1a_ref, w1b_ref, b1_ref, w2t_ref):
    x = x_ref[...].reshape(CMP_ROWS, CMP_FEAT)
    u = _dot((x + pea_ref[...]).astype(BF16), w1a_ref[...])
    v = _dot((x + peb_ref[...]).astype(BF16), w1b_ref[...])
    v = pltpu.roll(v, CMP_ROWS - 1, 0)
    h = u + v + b1_ref[...]
    h = 0.5 * h * (1.0 + jnp.tanh(np.sqrt(2.0 / np.pi) * (h + 0.044715 * (h * h * h))))
    return lax.dot_general(w2t_ref[...], h.astype(BF16), (((1,), (1,)), ((), ())),
                           preferred_element_type=F32)


def _compress_key_kernel(x_ref, pea_ref, peb_ref, w1a_ref, w1b_ref, b1_ref, w2t_ref, kg_ref,
                         cos_ref, sin_ref, o_ref):
    out = _compress_mlp(x_ref, pea_ref, peb_ref, w1a_ref, w1b_ref, b1_ref, w2t_ref)
    out = _head_norm_rope(out, kg_ref[...], cos_ref[...], sin_ref[...])
    for i in range(CMP_NB):
        o_ref[i] = out[:, i * N_CMP_PAD:(i + 1) * N_CMP_PAD].T.astype(BF16)


def _compress_value_kernel(x_ref, pea_ref, peb_ref, w1a_ref, w1b_ref, b1_ref, w2t_ref, o_ref):
    out = _compress_mlp(x_ref, pea_ref, peb_ref, w1a_ref, w1b_ref, b1_ref, w2t_ref)
    for i in range(CMP_NB):
        o_ref[i] = out[:, i * N_CMP_PAD:(i + 1) * N_CMP_PAD].astype(BF16)


def _compress_call(x, pe, w1, b1, w2, key_extras=None):
    full = lambda shape: pl.BlockSpec(shape, lambda t: (0,) * len(shape))
    in_specs = [
        pl.BlockSpec((CMP_NB, N_CMP_PAD, CMP_FEAT), lambda t: (t, 0, 0)),
        full((1, CMP_FEAT)), full((1, CMP_FEAT)),
        full((CMP_FEAT, CMP_HID)), full((CMP_FEAT, CMP_HID)),
        full((1, CMP_HID)), full((HEAD_DIM, CMP_HID)),
    ]
    pe = pe.reshape(2, 1, CMP_FEAT)
    w1 = w1.astype(BF16)
    args = [x, pe[0], pe[1], w1[:CMP_FEAT], w1[CMP_FEAT:], b1.reshape(1, CMP_HID), w2.T.astype(BF16)]
    if key_extras is not None:
        in_specs += [full((HEAD_DIM, 1)),
                     pl.BlockSpec((ROT_HALF, CMP_ROWS), lambda t: (0, t)),
                     pl.BlockSpec((ROT_HALF, CMP_ROWS), lambda t: (0, t))]
        args += list(key_extras)
        kernel, out_tail = _compress_key_kernel, (N_CMP_PAD, HEAD_DIM)
    else:
        kernel, out_tail = _compress_value_kernel, (HEAD_DIM, N_CMP_PAD)
    return pl.pallas_call(
        kernel,
        grid=(BATCH * N_KV // CMP_NB,),
        in_specs=in_specs,
        out_specs=pl.BlockSpec((CMP_NB,) + out_tail, lambda t: (t, 0, 0)),
        out_shape=jax.ShapeDtypeStruct((BATCH * N_KV,) + out_tail, BF16),
        compiler_params=_cparams(("parallel",)),
        name="nsa_compress_key" if key_extras is not None else "nsa_compress_value",
    )(*args)


def _group_queries(q_ref):
    return jnp.concatenate(
        [q_ref[0, r * HEAD_DIM:(r + 1) * HEAD_DIM, :] for r in range(GQA)], axis=1)


COL = 128


def _col_tiles():
    per_head = TQ // COL
    for ct in range(LANES_Q // COL):
        r, h = divmod(ct, per_head)
        yield r, slice(h * COL, (h + 1) * COL), slice(ct * COL, (ct + 1) * COL)


def _flash_scratch():
    return [
        pltpu.VMEM((2, TQ, LANES_Q), BF16),
        pltpu.VMEM((2, 1, LANES_Q), F32),
        pltpu.VMEM((1, LANES_Q), F32),
        pltpu.VMEM((V_AUG, LANES_Q), F32),
    ]


class _FlashBranch:
    def __init__(self, q_ref, k_ref, v_ref, bufs, own, past_chunk, own_fix=None, past_fix=None,
                 own_rows=None, past_rows=None):
        self.q_ref, self.k_ref, self.v_ref, self.bufs = q_ref, k_ref, v_ref, bufs
        self.own_chunk, self.past_chunk = own, past_chunk
        self.own_fix, self.past_fix, self.own_rows, self.past_rows = own_fix, past_fix, own_rows, past_rows

    def _softmax_tile(self, s, krows, cs, slot, init):
        p_buf, a_buf, m_ref, _ = self.bufs
        m_loc = jnp.max(s, axis=0, keepdims=True)
        if init:
            m_new = m_loc
            a_buf[slot, :, cs] = jnp.ones((1, COL), F32)
        else:
            m_old = m_ref[:, cs]
            m_new = jnp.maximum(m_old, m_loc)
            a_buf[slot, :, cs] = jnp.exp2(m_old - m_new)
        m_ref[:, cs] = m_new
        p_buf[slot, krows, cs] = jnp.exp2(s - m_new).astype(BF16)
        if krows.start > 0:
            p_buf[slot, 0:krows.start, cs] = jnp.zeros((krows.start, COL), BF16)
        if krows.stop < TQ:
            p_buf[slot, krows.stop:TQ, cs] = jnp.zeros((TQ - krows.stop, COL), BF16)

    def _scores(self, k, r, qs, cs, rows, fix):
        q = self.q_ref[0, r * HEAD_DIM:(r + 1) * HEAD_DIM, qs]
        if rows is not None:
            q = jnp.concatenate([q, rows(cs)], axis=0)
        krows = slice(0, TQ) if fix is None else fix.key_rows(qs)
        s = _dot(k[krows, :], q)
        if fix is not None:
            s = fix(s, qs, krows)
        return s, krows

    def _pv_slab(self, vT, slot, ct):
        if (ct * COL) % PV_COL:
            return
        p_buf, a_buf, _, acc_ref = self.bufs
        cs = slice(ct * COL, ct * COL + PV_COL)
        acc_ref[:, cs] = a_buf[slot, :, cs] * acc_ref[:, cs] + _dot(vT, p_buf[slot, :, cs])

    def own(self):
        acc_ref = self.bufs[3]
        acc_ref[...] = jnp.zeros(acc_ref.shape, F32)
        k_own = self.k_ref[0, 0, self.own_chunk]
        for r, qs, cs in _col_tiles():
            s, krows = self._scores(k_own, r, qs, cs, self.own_rows, self.own_fix)
            self._softmax_tile(s, krows, cs, 0, True)

    def trip(self, j, cur, prv):
        k = self.k_ref[0, 0, self.past_chunk(j)]
        vT = self.v_ref[0, 0, self._chunk_before(j)]
        rows = None if self.past_rows is None else self.past_rows(j)
        for ct, (r, qs, cs) in enumerate(_col_tiles()):
            s, krows = self._scores(k, r, qs, cs, rows, self.past_fix)
            self._pv_slab(vT, prv, ct)
            self._softmax_tile(s, krows, cs, cur, False)

    def run_past(self, n_past):
        if isinstance(n_past, int):
            for j in range(n_past):
                self.trip(j, 1 - j % 2, j % 2)
            return

        def pair(pp, carry):
            self.trip(2 * pp, 1, 0)
            self.trip(2 * pp + 1, 0, 1)
            return carry

        lax.fori_loop(0, n_past // 2, pair, 0)

        @pl.when(n_past % 2 == 1)
        def _():
            self.trip(n_past - 1, 1, 0)

    def _chunk_before(self, j):
        if isinstance(j, int):
            return self.own_chunk if j == 0 else self.past_chunk(j - 1)
        return jnp.where(j == 0, self.own_chunk, self.past_chunk(jnp.maximum(j - 1, 0)))

    def finish(self, n_past):
        vT = self.v_ref[0, 0, self._chunk_before(n_past)]
        for ct in range(LANES_Q // COL):
            self._pv_slab(vT, n_past % 2, ct)
        acc = self.bufs[3][...]
        return acc[:HEAD_DIM] / acc[HEAD_DIM:HEAD_DIM + 1]


class _CausalFix:
    def __init__(self, keep_lower):
        self.keep_lower = keep_lower

    def key_rows(self, qs):
        half = TQ // 2
        if self.keep_lower and qs.stop <= half:
            return slice(0, half)
        if not self.keep_lower and qs.start >= half:
            return slice(half, TQ)
        return slice(0, TQ)

    def __call__(self, s, qs, rows):
        n = rows.stop - rows.start
        a_idx = lax.broadcasted_iota(jnp.int32, (n, COL), 0) + rows.start
        lane = lax.broadcasted_iota(jnp.int32, (n, COL), 1) + qs.start
        return jnp.where(a_idx <= lane if self.keep_lower else a_idx > lane, s, NEG)


def _causal_fix(keep_lower):
    return _CausalFix(keep_lower)


def _rank_select(score_ref, n_rows, j_idx, count):
    score = score_ref[...]
    cnt = jnp.zeros(score.shape, jnp.int32)
    for jp in range(n_rows):
        row = score_ref[jp:jp + 1, :]
        beats = (row > score) | ((row == score) & (jp < j_idx))
        cnt = cnt + beats.astype(jnp.int32)
    return cnt < count


class _TileView:
    def __init__(self, ref, tile):
        self.ref, self.tile = ref, tile

    def __getitem__(self, idx):
        return self.ref[(0, 0, self.tile) + tuple(idx[1:])]

    def __setitem__(self, idx, value):
        self.ref[(0, 0, self.tile) + tuple(idx[1:])] = value


SCRATCH_SETS = 2


def _for_each_query_tile(tile_fn, tiled_refs, other_refs, scratch_refs):
    per_set = len(scratch_refs) // SCRATCH_SETS

    def start(qi):
        first = (qi % SCRATCH_SETS) * per_set
        gen = tile_fn(qi, *[_TileView(ref, qi) for ref in tiled_refs], *other_refs,
                      *scratch_refs[first:first + per_set])
        next(gen)
        return gen

    gen = start(0)
    for qi in range(N_CHUNK):
        next(gen)
        following = start(qi + 1) if qi + 1 < N_CHUNK else None
        for _ in gen:
            pass
        gen = following


def _store_out(o_ref, sz_ref, o):
    for r in range(GQA):
        rows = slice(r * HEAD_DIM, (r + 1) * HEAD_DIM)
        o_ref[0, rows, :] = (o[:, r * TQ:(r + 1) * TQ] * sz_ref[0, rows, :]).astype(BF16)


def _nsa_attn_kernel(q_ref, kc_ref, vc_ref, ks_ref, vs_ref, kw_ref, vw_ref, gt_ref, sz_ref,
                     selw_ref, o_ref, *scratch):
    _for_each_query_tile(_nsa_attn_tile, (q_ref, gt_ref, sz_ref, o_ref),
                         (kc_ref, vc_ref, ks_ref, vs_ref, kw_ref, vw_ref, selw_ref), scratch)


def _nsa_attn_tile(qi, q_ref, gt_ref, sz_ref, o_ref, kc_ref, vc_ref, ks_ref, vs_ref, kw_ref, vw_ref,
                   selw_ref, score_ref, bias_ref, *bufs):
    q4 = _group_queries(q_ref)

    sel_bufs, win_bufs = bufs[:len(bufs) // 2], bufs[len(bufs) // 2:]
    keep_lower = _causal_fix(True)

    n_win = min(qi, 1)
    win = _FlashBranch(q_ref, kw_ref, vw_ref, win_bufs, qi, lambda j: qi - 1,
                       own_fix=keep_lower, past_fix=_causal_fix(False))

    t = qi * TQ + (lax.broadcasted_iota(jnp.int32, (1, LANES_Q), 1) & (TQ - 1))
    n_idx = lax.broadcasted_iota(jnp.int32, (N_CMP_PAD, LANES_Q), 0)
    s_cmp = _dot(kc_ref[0, 0], q4)
    win.own()
    s = jnp.where(n_idx * CMP_STRIDE + (CMP_LEN - 1) <= t, s_cmp, NEG)
    m = jnp.max(s, axis=0, keepdims=True)
    p = jnp.exp2(s - m) * (m > 0.5 * NEG).astype(F32)
    p = p / jnp.maximum(jnp.sum(p, axis=0, keepdims=True), 1e-30)
    o_cmp = _dot(vc_ref[0, 0], p.astype(BF16))

    psum = p[:, 0:TQ]
    for r in range(1, GQA):
        psum = psum + p[:, r * TQ:(r + 1) * TQ]
    p_hi = psum.astype(BF16)
    p_lo = (psum - p_hi.astype(F32)).astype(BF16)
    imp = _dot(selw_ref[...], p_hi) + _dot(selw_ref[...], p_lo)
    j_idx = lax.broadcasted_iota(jnp.int32, (N_SEL, TQ), 0)
    cur = qi * (TQ // SEL_BLOCK) + (lax.broadcasted_iota(jnp.int32, (N_SEL, TQ), 1) >> 6)
    valid = j_idx <= cur
    forced = (j_idx == 0) | (j_idx == cur) | (j_idx == cur - 1)
    score_ref[...] = jnp.where(forced, jnp.inf, jnp.where(valid, imp, -jnp.inf))
    win.run_past(n_win)
    sel = valid & _rank_select(score_ref, N_SEL, j_idx, SEL_COUNT)
    bias = jnp.where(sel, 0.0, NEG)
    bias = jnp.concatenate([bias] * GQA, axis=1)
    per_chunk = TQ // SEL_BLOCK
    bias_ref[...] = jnp.zeros(bias_ref.shape, F32)
    for jp in range(N_SEL):
        bias_ref[jp // per_chunk, jp % per_chunk:jp % per_chunk + 1, :] = bias[jp:jp + 1, :]

    o_win = win.finish(n_win)
    yield

    def sel_rows(j):
        return lambda cs: bias_ref[j, :, cs].astype(BF16)

    slc = _FlashBranch(q_ref, ks_ref, vs_ref, sel_bufs, qi, lambda j: j, own_fix=keep_lower,
                       own_rows=sel_rows(qi), past_rows=sel_rows)
    slc.own()
    yield
    slc.run_past(qi)
    o_slc = slc.finish(qi)

    def gate(br):
        return jnp.concatenate(
            [gt_ref[0, br * GQA + r:br * GQA + r + 1, :] for r in range(GQA)], axis=1)

    o = gate(0) * o_cmp + gate(1) * o_slc + gate(2) * o_win
    _store_out(o_ref, sz_ref, o)


def _group_spec(*tail):
    return pl.BlockSpec((1, 1) + tail, lambda b, g: (b, g) + (0,) * len(tail))


def _attn_specs():
    q_spec = _group_spec(N_CHUNK, GQA * HEAD_DIM, TQ)
    k_spec = lambda width: _group_spec(N_CHUNK, TQ, width)
    v_spec = _group_spec(N_CHUNK, V_AUG, TQ)
    return q_spec, k_spec, v_spec


def _nsa_attn_call(q, kc, vc, ks, vs, kw, vw, gt, sz, selw):
    q_spec, k_spec, v_spec = _attn_specs()
    return pl.pallas_call(
        _nsa_attn_kernel,
        grid=(BATCH, N_KV),
        in_specs=[
            q_spec,
            _group_spec(N_CMP_PAD, HEAD_DIM), _group_spec(HEAD_DIM, N_CMP_PAD),
            k_spec(K_AUG), v_spec, k_spec(HEAD_DIM), v_spec,
            _group_spec(N_CHUNK, GATE_ROWS, TQ),
            q_spec,
            pl.BlockSpec((N_SEL, N_CMP_PAD), lambda b, g: (0, 0)),
        ],
        out_specs=q_spec,
        out_shape=_tile_shape(GQA * HEAD_DIM, BF16),
        scratch_shapes=SCRATCH_SETS * ([
            pltpu.VMEM((N_SEL, TQ), F32),
            pltpu.VMEM((N_CHUNK, BIAS_ROWS, LANES_Q), F32),
        ] + _flash_scratch() + _flash_scratch()),
        compiler_params=_cparams(("parallel", "parallel")),
        name="nsa_attention",
    )(q, kc, vc, ks, vs, kw, vw, gt, sz, selw)


def _moba_attn_kernel(q_ref, km_ref, k_ref, v_ref, sz_ref, o_ref,
                      *scratch):
    _for_each_query_tile(_moba_attn_tile, (q_ref, sz_ref, o_ref), (km_ref, k_ref, v_ref), scratch)


def _moba_attn_tile(qi, q_ref, sz_ref, o_ref, km_ref, k_ref, v_ref, score_ref, bias_ref, *bufs):
    q4 = _group_queries(q_ref)

    j_idx = lax.broadcasted_iota(jnp.int32, (N_CHUNK, LANES_Q), 0)
    past = j_idx < qi
    score_ref[...] = jnp.where(past, _dot(km_ref[0, 0], q4), -jnp.inf)
    sel = past & _rank_select(score_ref, N_CHUNK, j_idx, MOBA_TOPK)
    bias = jnp.where(sel, 0.0, NEG)
    bias_ref[...] = jnp.zeros(bias_ref.shape, F32)
    for jp in range(N_CHUNK):
        bias_ref[jp, 0:1, :] = bias[jp:jp + 1, :]

    yield

    branch = _FlashBranch(q_ref, k_ref, v_ref, bufs, qi, lambda j: j, own_fix=_causal_fix(True),
                          own_rows=lambda cs: jnp.zeros((BIAS_ROWS, COL), BF16),
                          past_rows=lambda j: (lambda cs: bias_ref[j, :, cs].astype(BF16)))
    branch.own()
    yield
    branch.run_past(qi)
    _store_out(o_ref, sz_ref, branch.finish(qi))


def _moba_attn_call(q, km, k, v, sz):
    q_spec, k_spec, v_spec = _attn_specs()
    return pl.pallas_call(
        _moba_attn_kernel,
        grid=(BATCH, N_KV),
        in_specs=[
            q_spec,
            _group_spec(N_CHUNK, HEAD_DIM),
            k_spec(K_AUG), v_spec, q_spec,
        ],
        out_specs=q_spec,
        out_shape=_tile_shape(GQA * HEAD_DIM, BF16),
        scratch_shapes=SCRATCH_SETS * ([
            pltpu.VMEM((N_CHUNK, LANES_Q), F32),
            pltpu.VMEM((N_CHUNK, BIAS_ROWS, LANES_Q), F32),
        ] + _flash_scratch()),
        compiler_params=_cparams(("parallel", "parallel")),
        name="moba_attention",
    )(q, km, k, v, sz)


def _residual_update(oz_ref, w_ref, x_ref, gt_ref, store):
    group_rows = GQA * HEAD_DIM
    for it in range(TS // TQ):
        lanes = slice(it * TQ, (it + 1) * TQ)
        for cblk in range(D_MODEL // 256):
            rows = slice(cblk * 256, (cblk + 1) * 256)
            y = _dot(w_ref[rows, 0:group_rows], oz_ref[0, 0, it])
            for gi in range(1, N_KV):
                y = y + _dot(w_ref[rows, gi * group_rows:(gi + 1) * group_rows], oz_ref[0, gi, it])
            store(rows, lanes, x_ref[0, rows, lanes] + gt_ref[0, rows, :] * y)


def _out_kernel(oz_ref, w_ref, x_ref, gt_ref, o_ref):
    def store(rows, lanes, new_x):
        o_ref[0, lanes, rows] = new_x.T

    _residual_update(oz_ref, w_ref, x_ref, gt_ref, store)


def _pending_specs():
    return [_tile_spec(GQA * HEAD_DIM),
            pl.BlockSpec((D_MODEL, D_MODEL), lambda b, s: (0, 0), pipeline_mode=pl.Buffered(1)),
            _bcol_spec(D_MODEL)]


def _out_call(pending, xT):
    oz, wT, gate = pending
    oz_spec, w_spec, gate_spec = _pending_specs()
    return pl.pallas_call(
        _out_kernel,
        grid=(BATCH, SEQ // TS),
        in_specs=[oz_spec, w_spec, _fm_spec(D_MODEL), gate_spec],
        out_specs=pl.BlockSpec((1, TS, D_MODEL), lambda b, s: (b, s, 0)),
        out_shape=jax.ShapeDtypeStruct((BATCH, SEQ, D_MODEL), F32),
        compiler_params=_cparams(("parallel", "parallel")),
        name="out_proj",
    )(oz, wT, xT, gate)


def _rope_tables(pos):
    inv_freq = ROPE_THETA ** (-jnp.arange(0, 2 * ROT_HALF, 2, dtype=F32) / (2 * ROT_HALF))
    ang = pos.astype(F32)[..., None] * inv_freq
    return jnp.cos(ang), jnp.sin(ang)


def _gate_perm():
    perm = np.full((N_KV * GATE_ROWS,), 3 * N_HEADS, dtype=np.int32)
    for g in range(N_KV):
        for br in range(3):
            for r in range(GQA):
                perm[g * GATE_ROWS + br * GQA + r] = (g * GQA + r) * 3 + br
    return perm


def _sel_weights_T():
    cs = np.arange(N_CMP)[:, None] * CMP_STRIDE
    ss = np.arange(N_SEL)[None, :] * SEL_BLOCK
    shared = np.clip(np.minimum(cs + CMP_LEN, ss + SEL_BLOCK) - np.maximum(cs, ss), 0, None)
    w = np.zeros((N_CMP_PAD, N_SEL), np.float32)
    w[:N_CMP] = shared / CMP_LEN
    return jnp.asarray(w.T, dtype=BF16)


def _col(v):
    return v.reshape(-1, 1)


def _input_args(x, pending):
    if pending is None:
        return (x,), "tokens"
    return tuple(pending) + (x,), "pending"


def _nsa_layer(x, pending, ng, sc, sh, cosT, sinT, cosc, sinc, w_in, q_g, k_g,
               cmp_pe, cmp_w1, cmp_b1, cmp_w2, gate_b):
    widths = [1024] + [256] * 6 + [3 * N_HEADS, 1024]
    q, kc, vc, ks, vs, kw, vw, gl, z = jnp.split(w_in, np.cumsum(widths)[:-1].tolist(), axis=1)
    perm = _gate_perm()
    gl_p = jnp.concatenate([gl, jnp.zeros((D_MODEL, 1), F32)], axis=1)[:, perm]
    gb_p = jnp.concatenate([gate_b, jnp.zeros((1,), F32)])[perm]
    wT = jnp.concatenate([q, ks, kw, kc, vc, vs, vw, gl_p, z], axis=1).T.astype(BF16)

    x_args, source = _input_args(x, pending)
    qT, ks_c, kw_c, kc_c, vc_c, vs_c, vw_c, gates, sz, xT = _nsa_in_call(
        x_args, ng, sc, sh, wT, _col(q_g), _col(k_g[1]), _col(k_g[2]), _col(gb_p), cosT, sinT, source)

    half_blocks = lambda t: t.reshape(BATCH * N_KV, N_CMP_PAD, CMP_FEAT)
    k_cmp = _compress_call(half_blocks(kc_c), cmp_pe[0], cmp_w1[0], cmp_b1[0], cmp_w2[0],
                           key_extras=(_col(k_g[0]), cosc, sinc))
    v_cmpT = _compress_call(half_blocks(vc_c), cmp_pe[1], cmp_w1[1], cmp_b1[1], cmp_w2[1])
    k_cmp = k_cmp.reshape(BATCH, N_KV, N_CMP_PAD, HEAD_DIM)
    v_cmpT = v_cmpT.reshape(BATCH, N_KV, HEAD_DIM, N_CMP_PAD)

    oz = _nsa_attn_call(qT, k_cmp, v_cmpT, ks_c, vs_c, kw_c, vw_c, gates, sz, _sel_weights_T())
    return oz, xT


def _moba_layer(x, pending, ng, sc, sh, cosT, sinT, w_in, q_g, k_g):
    x_args, source = _input_args(x, pending)
    qT, k_c, km, v_c, sz, xT = _moba_in_call(
        x_args, ng, sc, sh, w_in.T.astype(BF16), _col(q_g), _col(k_g), cosT, sinT, source)
    nb = TS // TQ
    km = km[..., :nb].reshape(BATCH, SEQ // TS, N_KV, HEAD_DIM, nb)
    km = km.transpose(0, 2, 1, 4, 3).reshape(BATCH, N_KV, N_CHUNK, HEAD_DIM).astype(BF16)
    return _moba_attn_call(qT, km, k_c, v_c, sz), xT


@jax.jit
def _forward(x, c, positions, norm_g, ada_w, ada_b, nsa_w_in, nsa_w_out, nsa_q_norm, nsa_k_norm,
             nsa_cmp_pe, nsa_cmp_w1, nsa_cmp_b1, nsa_cmp_w2, nsa_gate_b,
             moba_w_in, moba_w_out, moba_q_norm, moba_k_norm):
    cos, sin = _rope_tables(positions)
    cosT = cos.transpose(0, 2, 1)
    sinT = sin.transpose(0, 2, 1)
    cmp_end = np.minimum(np.arange(N_CMP_PAD) * CMP_STRIDE + CMP_LEN - 1, SEQ - 1)
    cos_c, sin_c = _rope_tables(positions[:, cmp_end])

    def cmp_table(t):
        t = jnp.broadcast_to(t.transpose(2, 0, 1)[:, :, None, :],
                             (ROT_HALF, BATCH, N_KV, N_CMP_PAD))
        return t.reshape(ROT_HALF, CMP_COLS)

    cosc = cmp_table(cos_c)
    sinc = cmp_table(sin_c)

    mod = _ada_call(c, ada_w, ada_b)
    shift, scale, gate = jnp.split(mod[..., None], 3, axis=2)

    pending = None
    for i in range(DEPTH):
        j = i // 2
        ng = _col(norm_g[i])
        if i % 2 == 0:
            oz, x = _nsa_layer(x, pending, ng, scale[i], shift[i], cosT, sinT, cosc, sinc,
                               nsa_w_in[j], nsa_q_norm[j], nsa_k_norm[j], nsa_cmp_pe[j],
                               nsa_cmp_w1[j], nsa_cmp_b1[j], nsa_cmp_w2[j], nsa_gate_b[j])
            w_out = nsa_w_out[j]
        else:
            oz, x = _moba_layer(x, pending, ng, scale[i], shift[i], cosT, sinT,
                                moba_w_in[j], moba_q_norm[j], moba_k_norm[j])
            w_out = moba_w_out[j]
        pending = (oz, w_out.T.astype(BF16), gate[i])
    return _out_call(pending, x)


def kernel(x, c, positions, norm_g, ada_w, ada_b, nsa_w_in, nsa_w_out, nsa_q_norm, nsa_k_norm, nsa_cmp_pe, nsa_cmp_w1, nsa_cmp_b1, nsa_cmp_w2, nsa_gate_b, moba_w_in, moba_w_out, moba_q_norm, moba_k_norm):
    return _forward(x, c, positions, norm_g, ada_w, ada_b, nsa_w_in, nsa_w_out, nsa_q_norm,
                    nsa_k_norm, nsa_cmp_pe, nsa_cmp_w1, nsa_cmp_b1, nsa_cmp_w2, nsa_gate_b,
                    moba_w_in, moba_w_out, moba_q_norm, moba_k_norm)
```

```python
import functools

import numpy as np
import jax
import jax.numpy as jnp
from jax import lax
from jax.experimental import pallas as pl
from jax.experimental.pallas import tpu as pltpu

D_MODEL = 1024
BATCH = 16
SEQ = 2048
DEPTH = 4
HEAD_DIM = 64
N_HEADS = 16
N_KV = 4
GQA = 4
ROT_HALF = 8
ROPE_THETA = 500000.0
NORM_EPS = 1e-6
CMP_LEN = 32
CMP_STRIDE = 16
CMP_HID = 256
SEL_BLOCK = 64
SEL_COUNT = 8
N_SEL = SEQ // SEL_BLOCK
N_CMP = (SEQ - CMP_LEN) // CMP_STRIDE + 1
N_CMP_PAD = 128
MOBA_TOPK = 3

TQ = 256
N_CHUNK = SEQ // TQ
LANES_Q = GQA * TQ
TS = 512
NEG = -1e30
BIAS_ROWS = 16
K_AUG = HEAD_DIM + BIAS_ROWS
V_AUG = HEAD_DIM + 16
PV_COL = 256
Q_SCALE = HEAD_DIM ** -0.5 * float(np.log2(np.e))
GATE_ROWS = 16

F32 = jnp.float32
BF16 = jnp.bfloat16

VMEM_LIMIT = 52 * 1024 * 1024


def _cparams(sem):
    return pltpu.CompilerParams(dimension_semantics=sem, vmem_limit_bytes=VMEM_LIMIT)


def _dot(a, b):
    return jnp.dot(a, b, preferred_element_type=F32)


def _ada_kernel(c_ref, w_ref, b_ref, o_ref):
    cond = c_ref[...]
    cond = cond * jax.nn.sigmoid(cond)
    o_ref[0] = jnp.dot(cond, w_ref[0], precision=lax.Precision.HIGHEST,
                       preferred_element_type=F32) + b_ref[0]


def _ada_call(c, ada_w, ada_b):
    nt = 1024
    return pl.pallas_call(
        _ada_kernel,
        grid=(DEPTH, 3 * D_MODEL // nt),
        in_specs=[
            pl.BlockSpec((BATCH, D_MODEL), lambda i, n: (0, 0)),
            pl.BlockSpec((1, D_MODEL, nt), lambda i, n: (i, 0, n)),
            pl.BlockSpec((1, 1, nt), lambda i, n: (i, 0, n)),
        ],
        out_specs=pl.BlockSpec((1, BATCH, nt), lambda i, n: (i, 0, n)),
        out_shape=jax.ShapeDtypeStruct((DEPTH, BATCH, 3 * D_MODEL), F32),
        compiler_params=_cparams(("parallel", "parallel")),
        name="ada_mod",
    )(c, ada_w, ada_b.reshape(DEPTH, 1, 3 * D_MODEL))


def _input_tile(refs, source):
    if source == "pending":
        oz_ref, wo_ref, gate_ref, x_ref, *rest = refs
        *rest, new_x_ref = rest

        def store(rows, lanes, value):
            new_x_ref[0, rows, lanes] = value

        _residual_update(oz_ref, wo_ref, x_ref, gate_ref, store)
        return new_x_ref[0], rest
    assert source == "tokens"
    x_ref, *rest, xT_ref = refs
    x = x_ref[0].T
    xT_ref[0] = x
    return x, rest


def _input_specs(source):
    if source == "pending":
        return _pending_specs() + [_fm_spec(D_MODEL)]
    return [pl.BlockSpec((1, TS, D_MODEL), lambda b, s: (b, s, 0))]


def _norm_mod(x, ng_ref, sc_ref, sh_ref):
    ms = jnp.mean(x * x, axis=0, keepdims=True)
    y = x * lax.rsqrt(ms + NORM_EPS)
    h = (y * ng_ref[...]) * (1.0 + sc_ref[0]) + sh_ref[0]
    return h.astype(BF16)


def _head_norm_rope(y, g, cos, sin):
    ms = jnp.mean(y * y, axis=0, keepdims=True)
    yn = (y * lax.rsqrt(ms + NORM_EPS)) * g
    x1 = yn[0:ROT_HALF]
    x2 = yn[ROT_HALF:2 * ROT_HALF]
    return jnp.concatenate([x1 * cos - x2 * sin, x2 * cos + x1 * sin, yn[2 * ROT_HALF:]], axis=0)


def _store_k_chunks(o_ref, gi, yh, ind=None):
    for it in range(TS // TQ):
        chunk = yh[:, it * TQ:(it + 1) * TQ]
        if ind is None:
            o_ref[0, gi, it] = chunk.T.astype(BF16)
        else:
            aug = jnp.concatenate([chunk, ind], axis=0).T
            o_ref[0, gi, it] = aug[:, :K_AUG].astype(BF16)


def _store_tiles(o_ref, gi, rows, y):
    for it in range(TS // TQ):
        o_ref[0, gi, it, rows, :] = y[:, it * TQ:(it + 1) * TQ]


def _project_queries(proj, off, qg_ref, cos, sin, q_ref):
    for gi in range(N_KV):
        y = proj(off + gi * 256, 256)
        for r in range(GQA):
            rows = slice(r * HEAD_DIM, (r + 1) * HEAD_DIM)
            yh = _head_norm_rope(y[rows], qg_ref[...], cos, sin)
            _store_tiles(q_ref, gi, rows, (yh * Q_SCALE).astype(BF16))


def _project_silu(proj, off, sz_ref):
    for gi in range(N_KV):
        z = proj(off + gi * 256, 256)
        _store_tiles(sz_ref, gi, slice(None), z * jax.nn.sigmoid(z))


def _block_indicator(blocks):
    row = lax.broadcasted_iota(jnp.int32, (HEAD_DIM, TQ), 0)
    lane = lax.broadcasted_iota(jnp.int32, (HEAD_DIM, TQ), 1)
    return (lane // (TQ // blocks) == row).astype(F32)


def _store_v_chunks(o_ref, y):
    yb = y.astype(BF16)
    row = lax.broadcasted_iota(jnp.int32, (V_AUG - HEAD_DIM, TQ), 0)
    ones_row = (row == 0).astype(BF16)
    for gi in range(N_KV):
        for it in range(TS // TQ):
            v = yb[gi * HEAD_DIM:(gi + 1) * HEAD_DIM, it * TQ:(it + 1) * TQ]
            o_ref[0, gi, it] = jnp.concatenate([v, ones_row], axis=0)


NSA_OFF_Q = 0
NSA_OFF_KS = 1024
NSA_OFF_KW = 1280
NSA_OFF_KC = 1536
NSA_OFF_VC = 1792
NSA_OFF_VS = 2048
NSA_OFF_VW = 2304
NSA_OFF_GT = 2560
NSA_OFF_Z = NSA_OFF_GT + N_KV * GATE_ROWS
NSA_ROWS = NSA_OFF_Z + D_MODEL


def _nsa_in_kernel(*refs, source):
    x, refs = _input_tile(refs, source)
    (ng_ref, sc_ref, sh_ref, w_ref, qg_ref, ksg_ref, kwg_ref, gb_ref, cos_ref, sin_ref,
     q_ref, ks_ref, kw_ref, kc_ref, vc_ref, vs_ref, vw_ref, gt_ref, sz_ref) = refs
    hb = _norm_mod(x, ng_ref, sc_ref, sh_ref)
    cos = cos_ref[0]
    sin = sin_ref[0]

    def proj(r0, n):
        return _dot(w_ref[r0:r0 + n, :], hb)

    _project_queries(proj, NSA_OFF_Q, qg_ref, cos, sin, q_ref)
    sel_ind = _block_indicator(TQ // SEL_BLOCK)
    for off, g_ref, o_ref, ind in ((NSA_OFF_KS, ksg_ref, ks_ref, sel_ind),
                                   (NSA_OFF_KW, kwg_ref, kw_ref, None)):
        y = proj(off, 256)
        for gi in range(N_KV):
            yh = _head_norm_rope(y[gi * HEAD_DIM:(gi + 1) * HEAD_DIM], g_ref[...], cos, sin)
            _store_k_chunks(o_ref, gi, yh, ind)
    for off, o_ref in ((NSA_OFF_KC, kc_ref), (NSA_OFF_VC, vc_ref)):
        y = proj(off, 256)
        for gi in range(N_KV):
            for it in range(TS // TQ):
                o_ref[0, gi, it] = y[gi * HEAD_DIM:(gi + 1) * HEAD_DIM, it * TQ:(it + 1) * TQ].T
    _store_v_chunks(vs_ref, proj(NSA_OFF_VS, 256))
    _store_v_chunks(vw_ref, proj(NSA_OFF_VW, 256))
    gates = jax.nn.sigmoid(proj(NSA_OFF_GT, N_KV * GATE_ROWS) + gb_ref[...])
    for gi in range(N_KV):
        _store_tiles(gt_ref, gi, slice(None), gates[gi * GATE_ROWS:(gi + 1) * GATE_ROWS])
    _project_silu(proj, NSA_OFF_Z, sz_ref)


def _col_spec(rows):
    return pl.BlockSpec((rows, 1), lambda b, s: (0, 0))


def _bcol_spec(rows):
    return pl.BlockSpec((1, rows, 1), lambda b, s: (b, 0, 0))


def _fm_spec(rows):
    return pl.BlockSpec((1, rows, TS), lambda b, s: (b, 0, s))


def _k_chunk_spec(width):
    return pl.BlockSpec((1, N_KV, TS // TQ, TQ, width), lambda b, s: (b, 0, s, 0, 0))


def _k_chunk_shape(width, dtype=BF16):
    return jax.ShapeDtypeStruct((BATCH, N_KV, N_CHUNK, TQ, width), dtype)


def _tile_spec(rows):
    return pl.BlockSpec((1, N_KV, TS // TQ, rows, TQ), lambda b, s: (b, 0, s, 0, 0))


def _tile_shape(rows, dtype):
    return jax.ShapeDtypeStruct((BATCH, N_KV, N_CHUNK, rows, TQ), dtype)


_V_CHUNK_SPEC = pl.BlockSpec((1, N_KV, TS // TQ, V_AUG, TQ), lambda b, s: (b, 0, s, 0, 0))
_V_CHUNK_SHAPE = jax.ShapeDtypeStruct((BATCH, N_KV, N_CHUNK, V_AUG, TQ), BF16)


def _resident_spec(shape):
    return pl.BlockSpec(shape, lambda b, s: (0,) * len(shape), pipeline_mode=pl.Buffered(1))


def _residual_out():
    return [_fm_spec(D_MODEL)], [jax.ShapeDtypeStruct((BATCH, D_MODEL, SEQ), F32)]


def _nsa_in_call(x_args, ng, sc, sh, wT, qg, ksg, kwg, gb, cosT, sinT, source):
    x_specs = _input_specs(source)
    extra_specs, extra_shapes = _residual_out()
    return pl.pallas_call(
        functools.partial(_nsa_in_kernel, source=source),
        grid=(BATCH, SEQ // TS),
        in_specs=x_specs + [
            _col_spec(D_MODEL), _bcol_spec(D_MODEL), _bcol_spec(D_MODEL),
            _resident_spec((NSA_ROWS, D_MODEL)),
            _col_spec(HEAD_DIM), _col_spec(HEAD_DIM), _col_spec(HEAD_DIM),
            _col_spec(N_KV * GATE_ROWS),
            _fm_spec(ROT_HALF), _fm_spec(ROT_HALF),
        ],
        out_specs=[
            _tile_spec(GQA * HEAD_DIM), _k_chunk_spec(K_AUG), _k_chunk_spec(HEAD_DIM),
            _k_chunk_spec(HEAD_DIM), _k_chunk_spec(HEAD_DIM),
            _V_CHUNK_SPEC, _V_CHUNK_SPEC, _tile_spec(GATE_ROWS), _tile_spec(GQA * HEAD_DIM),
        ] + extra_specs,
        out_shape=[
            _tile_shape(GQA * HEAD_DIM, BF16), _k_chunk_shape(K_AUG), _k_chunk_shape(HEAD_DIM),
            _k_chunk_shape(HEAD_DIM, F32), _k_chunk_shape(HEAD_DIM, F32),
            _V_CHUNK_SHAPE, _V_CHUNK_SHAPE, _tile_shape(GATE_ROWS, F32),
            _tile_shape(GQA * HEAD_DIM, F32),
        ] + extra_shapes,
        compiler_params=_cparams(("parallel", "parallel")),
        name="nsa_in_proj",
    )(*x_args, ng, sc, sh, wT, qg, ksg, kwg, gb, cosT, sinT)


MOBA_OFF_Q = 0
MOBA_OFF_K = 1024
MOBA_OFF_V = 1280
MOBA_OFF_Z = 1536
MOBA_ROWS = 2560
KM_LANES = 128


def _moba_in_kernel(*refs, source):
    x, refs = _input_tile(refs, source)
    (ng_ref, sc_ref, sh_ref, w_ref, qg_ref, kg_ref, cos_ref, sin_ref,
     q_ref, k_ref, km_ref, v_ref, sz_ref) = refs
    hb = _norm_mod(x, ng_ref, sc_ref, sh_ref)
    cos = cos_ref[0]
    sin = sin_ref[0]

    def proj(r0, n):
        return _dot(w_ref[r0:r0 + n, :], hb)

    _project_queries(proj, MOBA_OFF_Q, qg_ref, cos, sin, q_ref)
    y = proj(MOBA_OFF_K, 256)
    lane = lax.broadcasted_iota(jnp.int32, (HEAD_DIM, KM_LANES), 1)
    for gi in range(N_KV):
        yh = _head_norm_rope(y[gi * HEAD_DIM:(gi + 1) * HEAD_DIM], kg_ref[...], cos, sin)
        _store_k_chunks(k_ref, gi, yh, _block_indicator(1))
        km = jnp.zeros((HEAD_DIM, KM_LANES), F32)
        for it in range(TS // TQ):
            mean = jnp.mean(yh[:, it * TQ:(it + 1) * TQ], axis=1, keepdims=True)
            km = jnp.where(lane == it, mean, km)
        km_ref[0, 0, gi * HEAD_DIM:(gi + 1) * HEAD_DIM, :] = km
    _store_v_chunks(v_ref, proj(MOBA_OFF_V, 256))
    _project_silu(proj, MOBA_OFF_Z, sz_ref)


def _moba_in_call(x_args, ng, sc, sh, wT, qg, kg, cosT, sinT, source):
    x_specs = _input_specs(source)
    extra_specs, extra_shapes = _residual_out()
    return pl.pallas_call(
        functools.partial(_moba_in_kernel, source=source),
        grid=(BATCH, SEQ // TS),
        in_specs=x_specs + [
            _col_spec(D_MODEL), _bcol_spec(D_MODEL), _bcol_spec(D_MODEL),
            _resident_spec((MOBA_ROWS, D_MODEL)),
            _col_spec(HEAD_DIM), _col_spec(HEAD_DIM),
            _fm_spec(ROT_HALF), _fm_spec(ROT_HALF),
        ],
        out_specs=[
            _tile_spec(GQA * HEAD_DIM), _k_chunk_spec(K_AUG),
            pl.BlockSpec((1, 1, N_KV * HEAD_DIM, KM_LANES), lambda b, s: (b, s, 0, 0)),
            _V_CHUNK_SPEC, _tile_spec(GQA * HEAD_DIM),
        ] + extra_specs,
        out_shape=[
            _tile_shape(GQA * HEAD_DIM, BF16), _k_chunk_shape(K_AUG),
            jax.ShapeDtypeStruct((BATCH, SEQ // TS, N_KV * HEAD_DIM, KM_LANES), F32),
            _V_CHUNK_SHAPE, _tile_shape(GQA * HEAD_DIM, F32),
        ] + extra_shapes,
        compiler_params=_cparams(("parallel", "parallel")),
        name="moba_in_proj",
    )(*x_args, ng, sc, sh, wT, qg, kg, cosT, sinT)


CMP_NB = 8
CMP_ROWS = CMP_NB * N_CMP_PAD
CMP_COLS = BATCH * N_KV * N_CMP_PAD
CMP_FEAT = CMP_STRIDE * HEAD_DIM


def _compress_mlp(x_ref, pea_ref, peb_ref, w1a_ref, w1b_ref, b1_ref, w2t_ref):
    x = x_ref[...].reshape(CMP_ROWS, CMP_FEAT)
    u = _dot((x + pea_ref[...]).astype(BF16), w1a_ref[...])
    v = _dot((x + peb_ref[...]).astype(BF16), w1b_ref[...])
    v = pltpu.roll(v, CMP_ROWS - 1, 0)
    h = u + v + b1_ref[...]
    h = 0.5 * h * (1.0 + jnp.tanh(np.sqrt(2.0 / np.pi) * (h + 0.044715 * (h * h * h))))
    return lax.dot_general(w2t_ref[...], h.astype(BF16), (((1,), (1,)), ((), ())),
                           preferred_element_type=F32)


def _compress_key_kernel(x_ref, pea_ref, peb_ref, w1a_ref, w1b_ref, b1_ref, w2t_ref, kg_ref,
                         cos_ref, sin_ref, o_ref):
    out = _compress_mlp(x_ref, pea_ref, peb_ref, w1a_ref, w1b_ref, b1_ref, w2t_ref)
    out = _head_norm_rope(out, kg_ref[...], cos_ref[...], sin_ref[...])
    for i in range(CMP_NB):
        o_ref[i] = out[:, i * N_CMP_PAD:(i + 1) * N_CMP_PAD].T.astype(BF16)


def _compress_value_kernel(x_ref, pea_ref, peb_ref, w1a_ref, w1b_ref, b1_ref, w2t_ref, o_ref):
    out = _compress_mlp(x_ref, pea_ref, peb_ref, w1a_ref, w1b_ref, b1_ref, w2t_ref)
    for i in range(CMP_NB):
        o_ref[i] = out[:, i * N_CMP_PAD:(i + 1) * N_CMP_PAD].astype(BF16)


def _compress_call(x, pe, w1, b1, w2, key_extras=None):
    full = lambda shape: pl.BlockSpec(shape, lambda t: (0,) * len(shape))
    in_specs = [
        pl.BlockSpec((CMP_NB, N_CMP_PAD, CMP_FEAT), lambda t: (t, 0, 0)),
        full((1, CMP_FEAT)), full((1, CMP_FEAT)),
        full((CMP_FEAT, CMP_HID)), full((CMP_FEAT, CMP_HID)),
        full((1, CMP_HID)), full((HEAD_DIM, CMP_HID)),
    ]
    pe = pe.reshape(2, 1, CMP_FEAT)
    w1 = w1.astype(BF16)
    args = [x, pe[0], pe[1], w1[:CMP_FEAT], w1[CMP_FEAT:], b1.reshape(1, CMP_HID), w2.T.astype(BF16)]
    if key_extras is not None:
        in_specs += [full((HEAD_DIM, 1)),
                     pl.BlockSpec((ROT_HALF, CMP_ROWS), lambda t: (0, t)),
                     pl.BlockSpec((ROT_HALF, CMP_ROWS), lambda t: (0, t))]
        args += list(key_extras)
        kernel, out_tail = _compress_key_kernel, (N_CMP_PAD, HEAD_DIM)
    else:
        kernel, out_tail = _compress_value_kernel, (HEAD_DIM, N_CMP_PAD)
    return pl.pallas_call(
        kernel,
        grid=(BATCH * N_KV // CMP_NB,),
        in_specs=in_specs,
        out_specs=pl.BlockSpec((CMP_NB,) + out_tail, lambda t: (t, 0, 0)),
        out_shape=jax.ShapeDtypeStruct((BATCH * N_KV,) + out_tail, BF16),
        compiler_params=_cparams(("parallel",)),
        name="nsa_compress_key" if key_extras is not None else "nsa_compress_value",
    )(*args)


def _group_queries(q_ref):
    return jnp.concatenate(
        [q_ref[0, r * HEAD_DIM:(r + 1) * HEAD_DIM, :] for r in range(GQA)], axis=1)


COL = 128


def _col_tiles():
    per_head = TQ // COL
    for ct in range(LANES_Q // COL):
        r, h = divmod(ct, per_head)
        yield r, slice(h * COL, (h + 1) * COL), slice(ct * COL, (ct + 1) * COL)


def _flash_scratch():
    return [
        pltpu.VMEM((2, TQ, LANES_Q), BF16),
        pltpu.VMEM((2, 1, LANES_Q), F32),
        pltpu.VMEM((1, LANES_Q), F32),
        pltpu.VMEM((V_AUG, LANES_Q), F32),
    ]


class _FlashBranch:
    def __init__(self, q_ref, k_ref, v_ref, bufs, own, past_chunk, own_fix=None, past_fix=None,
                 own_rows=None, past_rows=None):
        self.q_ref, self.k_ref, self.v_ref, self.bufs = q_ref, k_ref, v_ref, bufs
        self.own_chunk, self.past_chunk = own, past_chunk
        self.own_fix, self.past_fix, self.own_rows, self.past_rows = own_fix, past_fix, own_rows, past_rows

    def _softmax_tile(self, s, krows, cs, slot, init):
        p_buf, a_buf, m_ref, _ = self.bufs
        m_loc = jnp.max(s, axis=0, keepdims=True)
        if init:
            m_new = m_loc
            a_buf[slot, :, cs] = jnp.ones((1, COL), F32)
        else:
            m_old = m_ref[:, cs]
            m_new = jnp.maximum(m_old, m_loc)
            a_buf[slot, :, cs] = jnp.exp2(m_old - m_new)
        m_ref[:, cs] = m_new
        p_buf[slot, krows, cs] = jnp.exp2(s - m_new).astype(BF16)
        if krows.start > 0:
            p_buf[slot, 0:krows.start, cs] = jnp.zeros((krows.start, COL), BF16)
        if krows.stop < TQ:
            p_buf[slot, krows.stop:TQ, cs] = jnp.zeros((TQ - krows.stop, COL), BF16)

    def _scores(self, k, r, qs, cs, rows, fix):
        q = self.q_ref[0, r * HEAD_DIM:(r + 1) * HEAD_DIM, qs]
        if rows is not None:
            q = jnp.concatenate([q, rows(cs)], axis=0)
        krows = slice(0, TQ) if fix is None else fix.key_rows(qs)
        s = _dot(k[krows, :], q)
        if fix is not None:
            s = fix(s, qs, krows)
        return s, krows

    def _pv_slab(self, vT, slot, ct):
        if (ct * COL) % PV_COL:
            return
        p_buf, a_buf, _, acc_ref = self.bufs
        cs = slice(ct * COL, ct * COL + PV_COL)
        acc_ref[:, cs] = a_buf[slot, :, cs] * acc_ref[:, cs] + _dot(vT, p_buf[slot, :, cs])

    def own(self):
        acc_ref = self.bufs[3]
        acc_ref[...] = jnp.zeros(acc_ref.shape, F32)
        k_own = self.k_ref[0, 0, self.own_chunk]
        for r, qs, cs in _col_tiles():
            s, krows = self._scores(k_own, r, qs, cs, self.own_rows, self.own_fix)
            self._softmax_tile(s, krows, cs, 0, True)

    def trip(self, j, cur, prv):
        k = self.k_ref[0, 0, self.past_chunk(j)]
        vT = self.v_ref[0, 0, self._chunk_before(j)]
        rows = None if self.past_rows is None else self.past_rows(j)
        for ct, (r, qs, cs) in enumerate(_col_tiles()):
            s, krows = self._scores(k, r, qs, cs, rows, self.past_fix)
            self._pv_slab(vT, prv, ct)
            self._softmax_tile(s, krows, cs, cur, False)

    def run_past(self, n_past):
        if isinstance(n_past, int):
            for j in range(n_past):
                self.trip(j, 1 - j % 2, j % 2)
            return

        def pair(pp, carry):
            self.trip(2 * pp, 1, 0)
            self.trip(2 * pp + 1, 0, 1)
            return carry

        lax.fori_loop(0, n_past // 2, pair, 0)

        @pl.when(n_past % 2 == 1)
        def _():
            self.trip(n_past - 1, 1, 0)

    def _chunk_before(self, j):
        if isinstance(j, int):
            return self.own_chunk if j == 0 else self.past_chunk(j - 1)
        return jnp.where(j == 0, self.own_chunk, self.past_chunk(jnp.maximum(j - 1, 0)))

    def finish(self, n_past):
        vT = self.v_ref[0, 0, self._chunk_before(n_past)]
        for ct in range(LANES_Q // COL):
            self._pv_slab(vT, n_past % 2, ct)
        acc = self.bufs[3][...]
        return acc[:HEAD_DIM] / acc[HEAD_DIM:HEAD_DIM + 1]


class _CausalFix:
    def __init__(self, keep_lower):
        self.keep_lower = keep_lower

    def key_rows(self, qs):
        half = TQ // 2
        if self.keep_lower and qs.stop <= half:
            return slice(0, half)
        if not self.keep_lower and qs.start >= half:
            return slice(half, TQ)
        return slice(0, TQ)

    def __call__(self, s, qs, rows):
        n = rows.stop - rows.start
        a_idx = lax.broadcasted_iota(jnp.int32, (n, COL), 0) + rows.start
        lane = lax.broadcasted_iota(jnp.int32, (n, COL), 1) + qs.start
        return jnp.where(a_idx <= lane if self.keep_lower else a_idx > lane, s, NEG)


def _causal_fix(keep_lower):
    return _CausalFix(keep_lower)


SUBLANES = 8


def _rank_select(score_ref, n_live, count):
    n_rows = -(-n_live // SUBLANES) * SUBLANES
    lanes = score_ref.shape[1]
    groups = [score_ref[g:g + SUBLANES, :] for g in range(0, n_rows, SUBLANES)]
    counts = [jnp.zeros((SUBLANES, lanes), jnp.int32) for _ in groups]
    j_in_group = lax.broadcasted_iota(jnp.int32, (SUBLANES, lanes), 0)
    for jp in range(n_live):
        row = score_ref[jp:jp + 1, :]
        for gi, sg in enumerate(groups):
            first = gi * SUBLANES
            if first > jp:
                beats = row >= sg
            elif first + SUBLANES - 1 <= jp:
                beats = row > sg
            else:
                beats = (row > sg) | ((row == sg) & (j_in_group + first > jp))
            counts[gi] = counts[gi] + beats.astype(jnp.int32)
    return jnp.concatenate(counts, axis=0) < count


class _TileView:
    def __init__(self, ref, tile):
        self.ref, self.tile = ref, tile

    def __getitem__(self, idx):
        return self.ref[(0, 0, self.tile) + tuple(idx[1:])]

    def __setitem__(self, idx, value):
        self.ref[(0, 0, self.tile) + tuple(idx[1:])] = value


SCRATCH_SETS = 2


def _for_each_query_tile(tile_fn, tiled_refs, other_refs, scratch_refs):
    per_set = len(scratch_refs) // SCRATCH_SETS

    def start(qi):
        first = (qi % SCRATCH_SETS) * per_set
        gen = tile_fn(qi, *[_TileView(ref, qi) for ref in tiled_refs], *other_refs,
                      *scratch_refs[first:first + per_set])
        next(gen)
        return gen

    gen = start(0)
    for qi in range(N_CHUNK):
        next(gen)
        following = start(qi + 1) if qi + 1 < N_CHUNK else None
        for _ in gen:
            pass
        gen = following


def _store_out(o_ref, sz_ref, o):
    for r in range(GQA):
        rows = slice(r * HEAD_DIM, (r + 1) * HEAD_DIM)
        o_ref[0, rows, :] = (o[:, r * TQ:(r + 1) * TQ] * sz_ref[0, rows, :]).astype(BF16)


def _nsa_attn_kernel(q_ref, kc_ref, vc_ref, ks_ref, vs_ref, kw_ref, vw_ref, gt_ref, sz_ref,
                     selw_ref, o_ref, *scratch):
    _for_each_query_tile(_nsa_attn_tile, (q_ref, gt_ref, sz_ref, o_ref),
                         (kc_ref, vc_ref, ks_ref, vs_ref, kw_ref, vw_ref, selw_ref), scratch)


def _nsa_attn_tile(qi, q_ref, gt_ref, sz_ref, o_ref, kc_ref, vc_ref, ks_ref, vs_ref, kw_ref, vw_ref,
                   selw_ref, score_ref, bias_ref, *bufs):
    q4 = _group_queries(q_ref)

    sel_bufs, win_bufs = bufs[:len(bufs) // 2], bufs[len(bufs) // 2:]
    keep_lower = _causal_fix(True)

    n_win = min(qi, 1)
    win = _FlashBranch(q_ref, kw_ref, vw_ref, win_bufs, qi, lambda j: qi - 1,
                       own_fix=keep_lower, past_fix=_causal_fix(False))

    t = qi * TQ + (lax.broadcasted_iota(jnp.int32, (1, LANES_Q), 1) & (TQ - 1))
    n_need = (qi + 1) * (TQ // CMP_STRIDE)
    n_idx = lax.broadcasted_iota(jnp.int32, (n_need, LANES_Q), 0)
    s_cmp = _dot(kc_ref[0, 0, 0:n_need, :], q4)
    win.own()
    s = jnp.where(n_idx * CMP_STRIDE + (CMP_LEN - 1) <= t, s_cmp, NEG)
    m = jnp.max(s, axis=0, keepdims=True)
    p = jnp.exp2(s - m)
    if qi == 0:
        p = p * (m > 0.5 * NEG).astype(F32)
    p = p * (1.0 / jnp.maximum(jnp.sum(p, axis=0, keepdims=True), 1e-30))

    def pad_rows(a):
        if n_need == N_CMP_PAD:
            return a
        return jnp.concatenate([a, jnp.zeros((N_CMP_PAD - n_need, a.shape[1]), a.dtype)], axis=0)

    o_cmp = _dot(vc_ref[0, 0], pad_rows(p).astype(BF16))

    psum = p[:, 0:TQ]
    for r in range(1, GQA):
        psum = psum + p[:, r * TQ:(r + 1) * TQ]
    psum = pad_rows(psum)
    p_hi = psum.astype(BF16)
    p_lo = (psum - p_hi.astype(F32)).astype(BF16)
    imp = _dot(selw_ref[...], p_hi) + _dot(selw_ref[...], p_lo)
    j_idx = lax.broadcasted_iota(jnp.int32, (N_SEL, TQ), 0)
    cur = qi * (TQ // SEL_BLOCK) + (lax.broadcasted_iota(jnp.int32, (N_SEL, TQ), 1) >> 6)
    valid = j_idx <= cur
    forced = (j_idx == 0) | (j_idx == cur) | (j_idx == cur - 1)
    score_ref[...] = jnp.where(forced, jnp.inf, jnp.where(valid, imp, -jnp.inf))
    win.run_past(n_win)
    per_chunk = TQ // SEL_BLOCK
    n_live = (qi + 1) * per_chunk
    ranked = _rank_select(score_ref, n_live, SEL_COUNT)
    sel = valid[:ranked.shape[0]] & ranked
    bias = jnp.where(sel, 0.0, NEG)
    bias = jnp.concatenate([bias] * GQA, axis=1)
    bias_ref[0:qi + 1] = jnp.zeros((qi + 1,) + bias_ref.shape[1:], F32)
    for jp in range(n_live):
        bias_ref[jp // per_chunk, jp % per_chunk:jp % per_chunk + 1, :] = bias[jp:jp + 1, :]

    o_win = win.finish(n_win)
    yield

    def sel_rows(j):
        return lambda cs: bias_ref[j, :, cs].astype(BF16)

    slc = _FlashBranch(q_ref, ks_ref, vs_ref, sel_bufs, qi, lambda j: j, own_fix=keep_lower,
                       own_rows=sel_rows(qi), past_rows=sel_rows)
    slc.own()
    yield
    slc.run_past(qi)
    o_slc = slc.finish(qi)

    def gate(br):
        return jnp.concatenate(
            [gt_ref[0, br * GQA + r:br * GQA + r + 1, :] for r in range(GQA)], axis=1)

    o = gate(0) * o_cmp + gate(1) * o_slc + gate(2) * o_win
    _store_out(o_ref, sz_ref, o)


def _group_spec(*tail):
    return pl.BlockSpec((1, 1) + tail, lambda b, g: (b, g) + (0,) * len(tail))


def _attn_specs():
    q_spec = _group_spec(N_CHUNK, GQA * HEAD_DIM, TQ)
    k_spec = lambda width: _group_spec(N_CHUNK, TQ, width)
    v_spec = _group_spec(N_CHUNK, V_AUG, TQ)
    return q_spec, k_spec, v_spec


def _nsa_attn_call(q, kc, vc, ks, vs, kw, vw, gt, sz, selw):
    q_spec, k_spec, v_spec = _attn_specs()
    return pl.pallas_call(
        _nsa_attn_kernel,
        grid=(BATCH, N_KV),
        in_specs=[
            q_spec,
            _group_spec(N_CMP_PAD, HEAD_DIM), _group_spec(HEAD_DIM, N_CMP_PAD),
            k_spec(K_AUG), v_spec, k_spec(HEAD_DIM), v_spec,
            _group_spec(N_CHUNK, GATE_ROWS, TQ),
            q_spec,
            pl.BlockSpec((N_SEL, N_CMP_PAD), lambda b, g: (0, 0)),
        ],
        out_specs=q_spec,
        out_shape=_tile_shape(GQA * HEAD_DIM, BF16),
        scratch_shapes=SCRATCH_SETS * ([
            pltpu.VMEM((N_SEL, TQ), F32),
            pltpu.VMEM((N_CHUNK, BIAS_ROWS, LANES_Q), F32),
        ] + _flash_scratch() + _flash_scratch()),
        compiler_params=_cparams(("parallel", "parallel")),
        name="nsa_attention",
    )(q, kc, vc, ks, vs, kw, vw, gt, sz, selw)


def _moba_attn_kernel(q_ref, km_ref, k_ref, v_ref, sz_ref, o_ref,
                      *scratch):
    _for_each_query_tile(_moba_attn_tile, (q_ref, sz_ref, o_ref), (km_ref, k_ref, v_ref), scratch)


def _moba_attn_tile(qi, q_ref, sz_ref, o_ref, km_ref, k_ref, v_ref, score_ref, bias_ref, *bufs):
    q4 = _group_queries(q_ref)

    if qi > 0:
        j_idx = lax.broadcasted_iota(jnp.int32, (N_CHUNK, LANES_Q), 0)
        past = j_idx < qi
        score_ref[...] = jnp.where(past, _dot(km_ref[0, 0], q4), -jnp.inf)
        sel = past & _rank_select(score_ref, qi, MOBA_TOPK)
        bias = jnp.where(sel, 0.0, NEG)
        bias_ref[0:qi] = jnp.zeros((qi,) + bias_ref.shape[1:], F32)
        for jp in range(qi):
            bias_ref[jp, 0:1, :] = bias[jp:jp + 1, :]

    yield

    branch = _FlashBranch(q_ref, k_ref, v_ref, bufs, qi, lambda j: j, own_fix=_causal_fix(True),
                          own_rows=lambda cs: jnp.zeros((BIAS_ROWS, COL), BF16),
                          past_rows=lambda j: (lambda cs: bias_ref[j, :, cs].astype(BF16)))
    branch.own()
    yield
    branch.run_past(qi)
    _store_out(o_ref, sz_ref, branch.finish(qi))


def _moba_attn_call(q, km, k, v, sz):
    q_spec, k_spec, v_spec = _attn_specs()
    return pl.pallas_call(
        _moba_attn_kernel,
        grid=(BATCH, N_KV),
        in_specs=[
            q_spec,
            _group_spec(N_CHUNK, HEAD_DIM),
            k_spec(K_AUG), v_spec, q_spec,
        ],
        out_specs=q_spec,
        out_shape=_tile_shape(GQA * HEAD_DIM, BF16),
        scratch_shapes=SCRATCH_SETS * ([
            pltpu.VMEM((N_CHUNK, LANES_Q), F32),
            pltpu.VMEM((N_CHUNK, BIAS_ROWS, LANES_Q), F32),
        ] + _flash_scratch()),
        compiler_params=_cparams(("parallel", "parallel")),
        name="moba_attention",
    )(q, km, k, v, sz)


def _residual_update(oz_ref, w_ref, x_ref, gt_ref, store):
    group_rows = GQA * HEAD_DIM
    for it in range(TS // TQ):
        lanes = slice(it * TQ, (it + 1) * TQ)
        for cblk in range(D_MODEL // 256):
            rows = slice(cblk * 256, (cblk + 1) * 256)
            y = _dot(w_ref[rows, 0:group_rows], oz_ref[0, 0, it])
            for gi in range(1, N_KV):
                y = y + _dot(w_ref[rows, gi * group_rows:(gi + 1) * group_rows], oz_ref[0, gi, it])
            store(rows, lanes, x_ref[0, rows, lanes] + gt_ref[0, rows, :] * y)


def _out_kernel(oz_ref, w_ref, x_ref, gt_ref, o_ref):
    def store(rows, lanes, new_x):
        o_ref[0, lanes, rows] = new_x.T

    _residual_update(oz_ref, w_ref, x_ref, gt_ref, store)


def _pending_specs():
    return [_tile_spec(GQA * HEAD_DIM),
            pl.BlockSpec((D_MODEL, D_MODEL), lambda b, s: (0, 0), pipeline_mode=pl.Buffered(1)),
            _bcol_spec(D_MODEL)]


def _out_call(pending, xT):
    oz, wT, gate = pending
    oz_spec, w_spec, gate_spec = _pending_specs()
    return pl.pallas_call(
        _out_kernel,
        grid=(BATCH, SEQ // TS),
        in_specs=[oz_spec, w_spec, _fm_spec(D_MODEL), gate_spec],
        out_specs=pl.BlockSpec((1, TS, D_MODEL), lambda b, s: (b, s, 0)),
        out_shape=jax.ShapeDtypeStruct((BATCH, SEQ, D_MODEL), F32),
        compiler_params=_cparams(("parallel", "parallel")),
        name="out_proj",
    )(oz, wT, xT, gate)


def _rope_tables(pos):
    inv_freq = ROPE_THETA ** (-jnp.arange(0, 2 * ROT_HALF, 2, dtype=F32) / (2 * ROT_HALF))
    ang = pos.astype(F32)[..., None] * inv_freq
    return jnp.cos(ang), jnp.sin(ang)


def _gate_perm():
    perm = np.full((N_KV * GATE_ROWS,), 3 * N_HEADS, dtype=np.int32)
    for g in range(N_KV):
        for br in range(3):
            for r in range(GQA):
                perm[g * GATE_ROWS + br * GQA + r] = (g * GQA + r) * 3 + br
    return perm


def _sel_weights_T():
    cs = np.arange(N_CMP)[:, None] * CMP_STRIDE
    ss = np.arange(N_SEL)[None, :] * SEL_BLOCK
    shared = np.clip(np.minimum(cs + CMP_LEN, ss + SEL_BLOCK) - np.maximum(cs, ss), 0, None)
    w = np.zeros((N_CMP_PAD, N_SEL), np.float32)
    w[:N_CMP] = shared / CMP_LEN
    return jnp.asarray(w.T, dtype=BF16)


def _col(v):
    return v.reshape(-1, 1)


def _input_args(x, pending):
    if pending is None:
        return (x,), "tokens"
    return tuple(pending) + (x,), "pending"


def _nsa_layer(x, pending, ng, sc, sh, cosT, sinT, cosc, sinc, w_in, q_g, k_g,
               cmp_pe, cmp_w1, cmp_b1, cmp_w2, gate_b):
    widths = [1024] + [256] * 6 + [3 * N_HEADS, 1024]
    q, kc, vc, ks, vs, kw, vw, gl, z = jnp.split(w_in, np.cumsum(widths)[:-1].tolist(), axis=1)
    perm = _gate_perm()
    gl_p = jnp.concatenate([gl, jnp.zeros((D_MODEL, 1), F32)], axis=1)[:, perm]
    gb_p = jnp.concatenate([gate_b, jnp.zeros((1,), F32)])[perm]
    wT = jnp.concatenate([q, ks, kw, kc, vc, vs, vw, gl_p, z], axis=1).T.astype(BF16)

    x_args, source = _input_args(x, pending)
    qT, ks_c, kw_c, kc_c, vc_c, vs_c, vw_c, gates, sz, xT = _nsa_in_call(
        x_args, ng, sc, sh, wT, _col(q_g), _col(k_g[1]), _col(k_g[2]), _col(gb_p), cosT, sinT, source)

    half_blocks = lambda t: t.reshape(BATCH * N_KV, N_CMP_PAD, CMP_FEAT)
    k_cmp = _compress_call(half_blocks(kc_c), cmp_pe[0], cmp_w1[0], cmp_b1[0], cmp_w2[0],
                           key_extras=(_col(k_g[0]), cosc, sinc))
    v_cmpT = _compress_call(half_blocks(vc_c), cmp_pe[1], cmp_w1[1], cmp_b1[1], cmp_w2[1])
    k_cmp = k_cmp.reshape(BATCH, N_KV, N_CMP_PAD, HEAD_DIM)
    v_cmpT = v_cmpT.reshape(BATCH, N_KV, HEAD_DIM, N_CMP_PAD)

    oz = _nsa_attn_call(qT, k_cmp, v_cmpT, ks_c, vs_c, kw_c, vw_c, gates, sz, _sel_weights_T())
    return oz, xT


def _moba_layer(x, pending, ng, sc, sh, cosT, sinT, w_in, q_g, k_g):
    x_args, source = _input_args(x, pending)
    qT, k_c, km, v_c, sz, xT = _moba_in_call(
        x_args, ng, sc, sh, w_in.T.astype(BF16), _col(q_g), _col(k_g), cosT, sinT, source)
    nb = TS // TQ
    km = km[..., :nb].reshape(BATCH, SEQ // TS, N_KV, HEAD_DIM, nb)
    km = km.transpose(0, 2, 1, 4, 3).reshape(BATCH, N_KV, N_CHUNK, HEAD_DIM).astype(BF16)
    return _moba_attn_call(qT, km, k_c, v_c, sz), xT


@jax.jit
def _forward(x, c, positions, norm_g, ada_w, ada_b, nsa_w_in, nsa_w_out, nsa_q_norm, nsa_k_norm,
             nsa_cmp_pe, nsa_cmp_w1, nsa_cmp_b1, nsa_cmp_w2, nsa_gate_b,
             moba_w_in, moba_w_out, moba_q_norm, moba_k_norm):
    cos, sin = _rope_tables(positions)
    cosT = cos.transpose(0, 2, 1)
    sinT = sin.transpose(0, 2, 1)
    cmp_end = np.minimum(np.arange(N_CMP_PAD) * CMP_STRIDE + CMP_LEN - 1, SEQ - 1)
    cos_c, sin_c = _rope_tables(positions[:, cmp_end])

    def cmp_table(t):
        t = jnp.broadcast_to(t.transpose(2, 0, 1)[:, :, None, :],
                             (ROT_HALF, BATCH, N_KV, N_CMP_PAD))
        return t.reshape(ROT_HALF, CMP_COLS)

    cosc = cmp_table(cos_c)
    sinc = cmp_table(sin_c)

    mod = _ada_call(c, ada_w, ada_b)
    shift, scale, gate = jnp.split(mod[..., None], 3, axis=2)

    pending = None
    for i in range(DEPTH):
        j = i // 2
        ng = _col(norm_g[i])
        if i % 2 == 0:
            oz, x = _nsa_layer(x, pending, ng, scale[i], shift[i], cosT, sinT, cosc, sinc,
                               nsa_w_in[j], nsa_q_norm[j], nsa_k_norm[j], nsa_cmp_pe[j],
                               nsa_cmp_w1[j], nsa_cmp_b1[j], nsa_cmp_w2[j], nsa_gate_b[j])
            w_out = nsa_w_out[j]
        else:
            oz, x = _moba_layer(x, pending, ng, scale[i], shift[i], cosT, sinT,
                                moba_w_in[j], moba_q_norm[j], moba_k_norm[j])
            w_out = moba_w_out[j]
        pending = (oz, w_out.T.astype(BF16), gate[i])
    return _out_call(pending, x)


def kernel(x, c, positions, norm_g, ada_w, ada_b, nsa_w_in, nsa_w_out, nsa_q_norm, nsa_k_norm, nsa_cmp_pe, nsa_cmp_w1, nsa_cmp_b1, nsa_cmp_w2, nsa_gate_b, moba_w_in, moba_w_out, moba_q_norm, moba_k_norm):
    return _forward(x, c, positions, norm_g, ada_w, ada_b, nsa_w_in, nsa_w_out, nsa_q_norm,
                    nsa_k_norm, nsa_cmp_pe, nsa_cmp_w1, nsa_cmp_b1, nsa_cmp_w2, nsa_gate_b,
                    moba_w_in, moba_w_out, moba_q_norm, moba_k_norm)
```

```python
import functools

import numpy as np
import jax
import jax.numpy as jnp
from jax import lax
from jax.experimental import pallas as pl
from jax.experimental.pallas import tpu as pltpu

D_MODEL = 1024
BATCH = 16
SEQ = 2048
DEPTH = 4
HEAD_DIM = 64
N_HEADS = 16
N_KV = 4
GQA = 4
ROT_HALF = 8
ROPE_THETA = 500000.0
NORM_EPS = 1e-6
CMP_LEN = 32
CMP_STRIDE = 16
CMP_HID = 256
SEL_BLOCK = 64
SEL_COUNT = 8
N_SEL = SEQ // SEL_BLOCK
N_CMP = (SEQ - CMP_LEN) // CMP_STRIDE + 1
N_CMP_PAD = 128
MOBA_TOPK = 3

TQ = 256
N_CHUNK = SEQ // TQ
LANES_Q = GQA * TQ
TS = 512
NEG = -1e30
BIAS_ROWS = 16
K_AUG = HEAD_DIM + BIAS_ROWS
V_AUG = HEAD_DIM + 16
PV_COL = 256
Q_SCALE = HEAD_DIM ** -0.5 * float(np.log2(np.e))
GATE_ROWS = 16

F32 = jnp.float32
BF16 = jnp.bfloat16

VMEM_LIMIT = 52 * 1024 * 1024


def _cparams(sem):
    return pltpu.CompilerParams(dimension_semantics=sem, vmem_limit_bytes=VMEM_LIMIT)


def _dot(a, b):
    return jnp.dot(a, b, preferred_element_type=F32)


def _ada_kernel(c_ref, w_ref, b_ref, o_ref):
    cond = c_ref[...]
    cond = cond * jax.nn.sigmoid(cond)
    o_ref[0] = jnp.dot(cond, w_ref[0], precision=lax.Precision.HIGHEST,
                       preferred_element_type=F32) + b_ref[0]


def _ada_call(c, ada_w, ada_b):
    nt = 1024
    return pl.pallas_call(
        _ada_kernel,
        grid=(DEPTH, 3 * D_MODEL // nt),
        in_specs=[
            pl.BlockSpec((BATCH, D_MODEL), lambda i, n: (0, 0)),
            pl.BlockSpec((1, D_MODEL, nt), lambda i, n: (i, 0, n)),
            pl.BlockSpec((1, 1, nt), lambda i, n: (i, 0, n)),
        ],
        out_specs=pl.BlockSpec((1, BATCH, nt), lambda i, n: (i, 0, n)),
        out_shape=jax.ShapeDtypeStruct((DEPTH, BATCH, 3 * D_MODEL), F32),
        compiler_params=_cparams(("parallel", "parallel")),
        name="ada_mod",
    )(c, ada_w, ada_b.reshape(DEPTH, 1, 3 * D_MODEL))


def _input_tile(refs, source):
    if source == "pending":
        oz_ref, wo_ref, gate_ref, x_ref, *rest = refs
        *rest, new_x_ref = rest

        def store(rows, lanes, value):
            new_x_ref[0, rows, lanes] = value

        _residual_update(oz_ref, wo_ref, x_ref, gate_ref, store)
        return new_x_ref[0], rest
    assert source == "tokens"
    x_ref, *rest, xT_ref = refs
    x = x_ref[0].T
    xT_ref[0] = x
    return x, rest


def _input_specs(source):
    if source == "pending":
        return _pending_specs() + [_fm_spec(D_MODEL)]
    return [pl.BlockSpec((1, TS, D_MODEL), lambda b, s: (b, s, 0))]


def _norm_mod(x, ng_ref, sc_ref, sh_ref):
    ms = jnp.mean(x * x, axis=0, keepdims=True)
    y = x * lax.rsqrt(ms + NORM_EPS)
    h = (y * ng_ref[...]) * (1.0 + sc_ref[0]) + sh_ref[0]
    return h.astype(BF16)


def _head_norm_rope(y, g, cos, sin):
    ms = jnp.mean(y * y, axis=0, keepdims=True)
    yn = (y * lax.rsqrt(ms + NORM_EPS)) * g
    x1 = yn[0:ROT_HALF]
    x2 = yn[ROT_HALF:2 * ROT_HALF]
    return jnp.concatenate([x1 * cos - x2 * sin, x2 * cos + x1 * sin, yn[2 * ROT_HALF:]], axis=0)


def _store_k_chunks(o_ref, gi, yh, ind=None):
    for it in range(TS // TQ):
        chunk = yh[:, it * TQ:(it + 1) * TQ]
        if ind is None:
            o_ref[0, gi, it] = chunk.T.astype(BF16)
        else:
            aug = jnp.concatenate([chunk, ind], axis=0).T
            o_ref[0, gi, it] = aug[:, :K_AUG].astype(BF16)


def _store_tiles(o_ref, gi, rows, y):
    for it in range(TS // TQ):
        o_ref[0, gi, it, rows, :] = y[:, it * TQ:(it + 1) * TQ]


def _project_queries(proj, off, qg_ref, cos, sin, q_ref):
    for gi in range(N_KV):
        y = proj(off + gi * 256, 256)
        for r in range(GQA):
            rows = slice(r * HEAD_DIM, (r + 1) * HEAD_DIM)
            yh = _head_norm_rope(y[rows], qg_ref[...], cos, sin)
            _store_tiles(q_ref, gi, rows, (yh * Q_SCALE).astype(BF16))


def _project_silu(proj, off, sz_ref):
    for gi in range(N_KV):
        z = proj(off + gi * 256, 256)
        _store_tiles(sz_ref, gi, slice(None), z * jax.nn.sigmoid(z))


def _block_indicator(blocks):
    row = lax.broadcasted_iota(jnp.int32, (HEAD_DIM, TQ), 0)
    lane = lax.broadcasted_iota(jnp.int32, (HEAD_DIM, TQ), 1)
    return (lane // (TQ // blocks) == row).astype(F32)


def _store_v_chunks(o_ref, y):
    yb = y.astype(BF16)
    row = lax.broadcasted_iota(jnp.int32, (V_AUG - HEAD_DIM, TQ), 0)
    ones_row = (row == 0).astype(BF16)
    for gi in range(N_KV):
        for it in range(TS // TQ):
            v = yb[gi * HEAD_DIM:(gi + 1) * HEAD_DIM, it * TQ:(it + 1) * TQ]
            o_ref[0, gi, it] = jnp.concatenate([v, ones_row], axis=0)


NSA_OFF_Q = 0
NSA_OFF_KS = 1024
NSA_OFF_KW = 1280
NSA_OFF_KC = 1536
NSA_OFF_VC = 1792
NSA_OFF_VS = 2048
NSA_OFF_VW = 2304
NSA_OFF_GT = 2560
NSA_OFF_Z = NSA_OFF_GT + N_KV * GATE_ROWS
NSA_ROWS = NSA_OFF_Z + D_MODEL


def _nsa_in_kernel(*refs, source):
    x, refs = _input_tile(refs, source)
    (ng_ref, sc_ref, sh_ref, w_ref, qg_ref, ksg_ref, kwg_ref, gb_ref, cos_ref, sin_ref,
     q_ref, ks_ref, kw_ref, kc_ref, vc_ref, vs_ref, vw_ref, gt_ref, sz_ref) = refs
    hb = _norm_mod(x, ng_ref, sc_ref, sh_ref)
    cos = cos_ref[0]
    sin = sin_ref[0]

    def proj(r0, n):
        return _dot(w_ref[r0:r0 + n, :], hb)

    _project_queries(proj, NSA_OFF_Q, qg_ref, cos, sin, q_ref)
    sel_ind = _block_indicator(TQ // SEL_BLOCK)
    for off, g_ref, o_ref, ind in ((NSA_OFF_KS, ksg_ref, ks_ref, sel_ind),
                                   (NSA_OFF_KW, kwg_ref, kw_ref, None)):
        y = proj(off, 256)
        for gi in range(N_KV):
            yh = _head_norm_rope(y[gi * HEAD_DIM:(gi + 1) * HEAD_DIM], g_ref[...], cos, sin)
            _store_k_chunks(o_ref, gi, yh, ind)
    for off, o_ref in ((NSA_OFF_KC, kc_ref), (NSA_OFF_VC, vc_ref)):
        y = proj(off, 256)
        for gi in range(N_KV):
            for it in range(TS // TQ):
                o_ref[0, gi, it] = y[gi * HEAD_DIM:(gi + 1) * HEAD_DIM, it * TQ:(it + 1) * TQ].T
    _store_v_chunks(vs_ref, proj(NSA_OFF_VS, 256))
    _store_v_chunks(vw_ref, proj(NSA_OFF_VW, 256))
    gates = jax.nn.sigmoid(proj(NSA_OFF_GT, N_KV * GATE_ROWS) + gb_ref[...])
    for gi in range(N_KV):
        _store_tiles(gt_ref, gi, slice(None), gates[gi * GATE_ROWS:(gi + 1) * GATE_ROWS])
    _project_silu(proj, NSA_OFF_Z, sz_ref)


def _col_spec(rows):
    return pl.BlockSpec((rows, 1), lambda b, s: (0, 0))


def _bcol_spec(rows):
    return pl.BlockSpec((1, rows, 1), lambda b, s: (b, 0, 0))


def _fm_spec(rows):
    return pl.BlockSpec((1, rows, TS), lambda b, s: (b, 0, s))


def _k_chunk_spec(width):
    return pl.BlockSpec((1, N_KV, TS // TQ, TQ, width), lambda b, s: (b, 0, s, 0, 0))


def _k_chunk_shape(width, dtype=BF16):
    return jax.ShapeDtypeStruct((BATCH, N_KV, N_CHUNK, TQ, width), dtype)


def _tile_spec(rows):
    return pl.BlockSpec((1, N_KV, TS // TQ, rows, TQ), lambda b, s: (b, 0, s, 0, 0))


def _tile_shape(rows, dtype):
    return jax.ShapeDtypeStruct((BATCH, N_KV, N_CHUNK, rows, TQ), dtype)


_V_CHUNK_SPEC = pl.BlockSpec((1, N_KV, TS // TQ, V_AUG, TQ), lambda b, s: (b, 0, s, 0, 0))
_V_CHUNK_SHAPE = jax.ShapeDtypeStruct((BATCH, N_KV, N_CHUNK, V_AUG, TQ), BF16)


def _resident_spec(shape):
    return pl.BlockSpec(shape, lambda b, s: (0,) * len(shape), pipeline_mode=pl.Buffered(1))


def _residual_out():
    return [_fm_spec(D_MODEL)], [jax.ShapeDtypeStruct((BATCH, D_MODEL, SEQ), F32)]


def _nsa_in_call(x_args, ng, sc, sh, wT, qg, ksg, kwg, gb, cosT, sinT, source):
    x_specs = _input_specs(source)
    extra_specs, extra_shapes = _residual_out()
    return pl.pallas_call(
        functools.partial(_nsa_in_kernel, source=source),
        grid=(BATCH, SEQ // TS),
        in_specs=x_specs + [
            _col_spec(D_MODEL), _bcol_spec(D_MODEL), _bcol_spec(D_MODEL),
            _resident_spec((NSA_ROWS, D_MODEL)),
            _col_spec(HEAD_DIM), _col_spec(HEAD_DIM), _col_spec(HEAD_DIM),
            _col_spec(N_KV * GATE_ROWS),
            _fm_spec(ROT_HALF), _fm_spec(ROT_HALF),
        ],
        out_specs=[
            _tile_spec(GQA * HEAD_DIM), _k_chunk_spec(K_AUG), _k_chunk_spec(HEAD_DIM),
            _k_chunk_spec(HEAD_DIM), _k_chunk_spec(HEAD_DIM),
            _V_CHUNK_SPEC, _V_CHUNK_SPEC, _tile_spec(GATE_ROWS), _tile_spec(GQA * HEAD_DIM),
        ] + extra_specs,
        out_shape=[
            _tile_shape(GQA * HEAD_DIM, BF16), _k_chunk_shape(K_AUG), _k_chunk_shape(HEAD_DIM),
            _k_chunk_shape(HEAD_DIM, F32), _k_chunk_shape(HEAD_DIM, F32),
            _V_CHUNK_SHAPE, _V_CHUNK_SHAPE, _tile_shape(GATE_ROWS, F32),
            _tile_shape(GQA * HEAD_DIM, F32),
        ] + extra_shapes,
        compiler_params=_cparams(("parallel", "parallel")),
        name="nsa_in_proj",
    )(*x_args, ng, sc, sh, wT, qg, ksg, kwg, gb, cosT, sinT)


MOBA_OFF_Q = 0
MOBA_OFF_K = 1024
MOBA_OFF_V = 1280
MOBA_OFF_Z = 1536
MOBA_ROWS = 2560
KM_LANES = 128


def _moba_in_kernel(*refs, source):
    x, refs = _input_tile(refs, source)
    (ng_ref, sc_ref, sh_ref, w_ref, qg_ref, kg_ref, cos_ref, sin_ref,
     q_ref, k_ref, km_ref, v_ref, sz_ref) = refs
    hb = _norm_mod(x, ng_ref, sc_ref, sh_ref)
    cos = cos_ref[0]
    sin = sin_ref[0]

    def proj(r0, n):
        return _dot(w_ref[r0:r0 + n, :], hb)

    _project_queries(proj, MOBA_OFF_Q, qg_ref, cos, sin, q_ref)
    y = proj(MOBA_OFF_K, 256)
    lane = lax.broadcasted_iota(jnp.int32, (HEAD_DIM, KM_LANES), 1)
    for gi in range(N_KV):
        yh = _head_norm_rope(y[gi * HEAD_DIM:(gi + 1) * HEAD_DIM], kg_ref[...], cos, sin)
        _store_k_chunks(k_ref, gi, yh, _block_indicator(1))
        km = jnp.zeros((HEAD_DIM, KM_LANES), F32)
        for it in range(TS // TQ):
            mean = jnp.mean(yh[:, it * TQ:(it + 1) * TQ], axis=1, keepdims=True)
            km = jnp.where(lane == it, mean, km)
        km_ref[0, 0, gi * HEAD_DIM:(gi + 1) * HEAD_DIM, :] = km
    _store_v_chunks(v_ref, proj(MOBA_OFF_V, 256))
    _project_silu(proj, MOBA_OFF_Z, sz_ref)


def _moba_in_call(x_args, ng, sc, sh, wT, qg, kg, cosT, sinT, source):
    x_specs = _input_specs(source)
    extra_specs, extra_shapes = _residual_out()
    return pl.pallas_call(
        functools.partial(_moba_in_kernel, source=source),
        grid=(BATCH, SEQ // TS),
        in_specs=x_specs + [
            _col_spec(D_MODEL), _bcol_spec(D_MODEL), _bcol_spec(D_MODEL),
            _resident_spec((MOBA_ROWS, D_MODEL)),
            _col_spec(HEAD_DIM), _col_spec(HEAD_DIM),
            _fm_spec(ROT_HALF), _fm_spec(ROT_HALF),
        ],
        out_specs=[
            _tile_spec(GQA * HEAD_DIM), _k_chunk_spec(K_AUG),
            pl.BlockSpec((1, 1, N_KV * HEAD_DIM, KM_LANES), lambda b, s: (b, s, 0, 0)),
            _V_CHUNK_SPEC, _tile_spec(GQA * HEAD_DIM),
        ] + extra_specs,
        out_shape=[
            _tile_shape(GQA * HEAD_DIM, BF16), _k_chunk_shape(K_AUG),
            jax.ShapeDtypeStruct((BATCH, SEQ // TS, N_KV * HEAD_DIM, KM_LANES), F32),
            _V_CHUNK_SHAPE, _tile_shape(GQA * HEAD_DIM, F32),
        ] + extra_shapes,
        compiler_params=_cparams(("parallel", "parallel")),
        name="moba_in_proj",
    )(*x_args, ng, sc, sh, wT, qg, kg, cosT, sinT)


CMP_NB = 8
CMP_ROWS = CMP_NB * N_CMP_PAD
CMP_COLS = BATCH * N_KV * N_CMP_PAD
CMP_FEAT = CMP_STRIDE * HEAD_DIM


def _compress_mlp(x_ref, pe_ref, w1a_ref, w1b_ref, b1_ref, w2t_ref):
    u = v = None
    for l in range(CMP_STRIDE):
        xl = jnp.concatenate([x_ref[i, pl.ds(l, N_CMP_PAD, stride=CMP_STRIDE), :]
                              for i in range(CMP_NB)], axis=0)
        cols = slice(l * HEAD_DIM, (l + 1) * HEAD_DIM)
        ul = _dot((xl + pe_ref[l:l + 1, :]).astype(BF16), w1a_ref[cols, :])
        vl = _dot((xl + pe_ref[CMP_STRIDE + l:CMP_STRIDE + l + 1, :]).astype(BF16), w1b_ref[cols, :])
        u = ul if u is None else u + ul
        v = vl if v is None else v + vl
    v = pltpu.roll(v, CMP_ROWS - 1, 0)
    h = u + v + b1_ref[...]
    h = 0.5 * h * (1.0 + jnp.tanh(np.sqrt(2.0 / np.pi) * (h + 0.044715 * (h * h * h))))
    return lax.dot_general(w2t_ref[...], h.astype(BF16), (((1,), (1,)), ((), ())),
                           preferred_element_type=F32)


def _compress_key_kernel(x_ref, pe_ref, w1a_ref, w1b_ref, b1_ref, w2t_ref, kg_ref,
                         cos_ref, sin_ref, o_ref):
    out = _compress_mlp(x_ref, pe_ref, w1a_ref, w1b_ref, b1_ref, w2t_ref)
    out = _head_norm_rope(out, kg_ref[...], cos_ref[...], sin_ref[...])
    for i in range(CMP_NB):
        o_ref[i] = out[:, i * N_CMP_PAD:(i + 1) * N_CMP_PAD].T.astype(BF16)


def _compress_value_kernel(x_ref, pe_ref, w1a_ref, w1b_ref, b1_ref, w2t_ref, o_ref):
    out = _compress_mlp(x_ref, pe_ref, w1a_ref, w1b_ref, b1_ref, w2t_ref)
    for i in range(CMP_NB):
        o_ref[i] = out[:, i * N_CMP_PAD:(i + 1) * N_CMP_PAD].astype(BF16)


def _compress_call(x, pe, w1, b1, w2, key_extras=None):
    full = lambda shape: pl.BlockSpec(shape, lambda t: (0,) * len(shape))
    in_specs = [
        pl.BlockSpec((CMP_NB, SEQ, HEAD_DIM), lambda t: (t, 0, 0)),
        full((CMP_LEN, HEAD_DIM)),
        full((CMP_FEAT, CMP_HID)), full((CMP_FEAT, CMP_HID)),
        full((1, CMP_HID)), full((HEAD_DIM, CMP_HID)),
    ]
    w1 = w1.astype(BF16)
    args = [x, pe, w1[:CMP_FEAT], w1[CMP_FEAT:], b1.reshape(1, CMP_HID), w2.T.astype(BF16)]
    if key_extras is not None:
        in_specs += [full((HEAD_DIM, 1)),
                     pl.BlockSpec((ROT_HALF, CMP_ROWS), lambda t: (0, t)),
                     pl.BlockSpec((ROT_HALF, CMP_ROWS), lambda t: (0, t))]
        args += list(key_extras)
        kernel, out_tail = _compress_key_kernel, (N_CMP_PAD, HEAD_DIM)
    else:
        kernel, out_tail = _compress_value_kernel, (HEAD_DIM, N_CMP_PAD)
    return pl.pallas_call(
        kernel,
        grid=(BATCH * N_KV // CMP_NB,),
        in_specs=in_specs,
        out_specs=pl.BlockSpec((CMP_NB,) + out_tail, lambda t: (t, 0, 0)),
        out_shape=jax.ShapeDtypeStruct((BATCH * N_KV,) + out_tail, BF16),
        compiler_params=_cparams(("parallel",)),
        name="nsa_compress_key" if key_extras is not None else "nsa_compress_value",
    )(*args)


def _group_queries(q_ref):
    return jnp.concatenate(
        [q_ref[0, r * HEAD_DIM:(r + 1) * HEAD_DIM, :] for r in range(GQA)], axis=1)


COL = 128


def _col_tiles():
    per_head = TQ // COL
    for ct in range(LANES_Q // COL):
        r, h = divmod(ct, per_head)
        yield r, slice(h * COL, (h + 1) * COL), slice(ct * COL, (ct + 1) * COL)


def _flash_scratch():
    return [
        pltpu.VMEM((2, TQ, LANES_Q), BF16),
        pltpu.VMEM((2, 1, LANES_Q), F32),
        pltpu.VMEM((1, LANES_Q), F32),
        pltpu.VMEM((V_AUG, LANES_Q), F32),
    ]


class _FlashBranch:
    def __init__(self, q_ref, k_ref, v_ref, bufs, own, past_chunk, own_fix=None, past_fix=None,
                 own_rows=None, past_rows=None):
        self.q_ref, self.k_ref, self.v_ref, self.bufs = q_ref, k_ref, v_ref, bufs
        self.own_chunk, self.past_chunk = own, past_chunk
        self.own_fix, self.past_fix, self.own_rows, self.past_rows = own_fix, past_fix, own_rows, past_rows

    def _softmax_tile(self, s, krows, cs, slot, init):
        p_buf, a_buf, m_ref, _ = self.bufs
        m_loc = jnp.max(s, axis=0, keepdims=True)
        if init:
            m_new = m_loc
            a_buf[slot, :, cs] = jnp.ones((1, COL), F32)
        else:
            m_old = m_ref[:, cs]
            m_new = jnp.maximum(m_old, m_loc)
            a_buf[slot, :, cs] = jnp.exp2(m_old - m_new)
        m_ref[:, cs] = m_new
        p_buf[slot, krows, cs] = jnp.exp2(s - m_new).astype(BF16)
        if krows.start > 0:
            p_buf[slot, 0:krows.start, cs] = jnp.zeros((krows.start, COL), BF16)
        if krows.stop < TQ:
            p_buf[slot, krows.stop:TQ, cs] = jnp.zeros((TQ - krows.stop, COL), BF16)

    def _scores(self, k, r, qs, cs, rows, fix):
        q = self.q_ref[0, r * HEAD_DIM:(r + 1) * HEAD_DIM, qs]
        if rows is not None:
            q = jnp.concatenate([q, rows(cs)], axis=0)
        krows = slice(0, TQ) if fix is None else fix.key_rows(qs)
        s = _dot(k[krows, :], q)
        if fix is not None:
            s = fix(s, qs, krows)
        return s, krows

    def _pv_slab(self, vT, slot, ct):
        if (ct * COL) % PV_COL:
            return
        p_buf, a_buf, _, acc_ref = self.bufs
        cs = slice(ct * COL, ct * COL + PV_COL)
        acc_ref[:, cs] = a_buf[slot, :, cs] * acc_ref[:, cs] + _dot(vT, p_buf[slot, :, cs])

    def own(self):
        acc_ref = self.bufs[3]
        acc_ref[...] = jnp.zeros(acc_ref.shape, F32)
        k_own = self.k_ref[0, 0, self.own_chunk]
        for r, qs, cs in _col_tiles():
            s, krows = self._scores(k_own, r, qs, cs, self.own_rows, self.own_fix)
            self._softmax_tile(s, krows, cs, 0, True)

    def trip(self, j, cur, prv):
        k = self.k_ref[0, 0, self.past_chunk(j)]
        vT = self.v_ref[0, 0, self._chunk_before(j)]
        rows = None if self.past_rows is None else self.past_rows(j)
        for ct, (r, qs, cs) in enumerate(_col_tiles()):
            s, krows = self._scores(k, r, qs, cs, rows, self.past_fix)
            self._pv_slab(vT, prv, ct)
            self._softmax_tile(s, krows, cs, cur, False)

    def run_past(self, n_past):
        if isinstance(n_past, int):
            for j in range(n_past):
                self.trip(j, 1 - j % 2, j % 2)
            return

        def pair(pp, carry):
            self.trip(2 * pp, 1, 0)
            self.trip(2 * pp + 1, 0, 1)
            return carry

        lax.fori_loop(0, n_past // 2, pair, 0)

        @pl.when(n_past % 2 == 1)
        def _():
            self.trip(n_past - 1, 1, 0)

    def _chunk_before(self, j):
        if isinstance(j, int):
            return self.own_chunk if j == 0 else self.past_chunk(j - 1)
        return jnp.where(j == 0, self.own_chunk, self.past_chunk(jnp.maximum(j - 1, 0)))

    def finish(self, n_past):
        vT = self.v_ref[0, 0, self._chunk_before(n_past)]
        for ct in range(LANES_Q // COL):
            self._pv_slab(vT, n_past % 2, ct)
        acc = self.bufs[3][...]
        return acc[:HEAD_DIM] / acc[HEAD_DIM:HEAD_DIM + 1]


class _CausalFix:
    def __init__(self, keep_lower):
        self.keep_lower = keep_lower

    def key_rows(self, qs):
        half = TQ // 2
        if self.keep_lower and qs.stop <= half:
            return slice(0, half)
        if not self.keep_lower and qs.start >= half:
            return slice(half, TQ)
        return slice(0, TQ)

    def __call__(self, s, qs, rows):
        assert rows.start <= qs.start and qs.stop <= rows.stop
        a_idx = lax.broadcasted_iota(jnp.int32, (COL, COL), 0)
        lane = lax.broadcasted_iota(jnp.int32, (COL, COL), 1)
        lo, hi = qs.start - rows.start, qs.stop - rows.start
        diag = jnp.where(a_idx <= lane if self.keep_lower else a_idx > lane, s[lo:hi], NEG)
        parts = [part for part in (s[:lo], diag, s[hi:]) if part.shape[0]]
        return parts[0] if len(parts) == 1 else jnp.concatenate(parts, axis=0)


def _causal_fix(keep_lower):
    return _CausalFix(keep_lower)


SUBLANES = 8


def _rank_select(score_ref, n_live, count):
    n_rows = -(-n_live // SUBLANES) * SUBLANES
    lanes = score_ref.shape[1]
    groups = [score_ref[g:g + SUBLANES, :] for g in range(0, n_rows, SUBLANES)]
    counts = [jnp.zeros((SUBLANES, lanes), jnp.int32) for _ in groups]
    j_in_group = lax.broadcasted_iota(jnp.int32, (SUBLANES, lanes), 0)
    for jp in range(n_live):
        row = score_ref[jp:jp + 1, :]
        for gi, sg in enumerate(groups):
            first = gi * SUBLANES
            if first > jp:
                beats = row >= sg
            elif first + SUBLANES - 1 <= jp:
                beats = row > sg
            else:
                beats = (row > sg) | ((row == sg) & (j_in_group + first > jp))
            counts[gi] = counts[gi] + beats.astype(jnp.int32)
    return jnp.concatenate(counts, axis=0) < count


class _TileView:
    def __init__(self, ref, tile):
        self.ref, self.tile = ref, tile

    def __getitem__(self, idx):
        return self.ref[(0, 0, self.tile) + tuple(idx[1:])]

    def __setitem__(self, idx, value):
        self.ref[(0, 0, self.tile) + tuple(idx[1:])] = value


SCRATCH_SETS = 2


def _for_each_query_tile(tile_fn, tiled_refs, other_refs, scratch_refs):
    per_set = len(scratch_refs) // SCRATCH_SETS

    def start(qi):
        first = (qi % SCRATCH_SETS) * per_set
        gen = tile_fn(qi, *[_TileView(ref, qi) for ref in tiled_refs], *other_refs,
                      *scratch_refs[first:first + per_set])
        next(gen)
        return gen

    gen = start(0)
    for qi in range(N_CHUNK):
        next(gen)
        following = start(qi + 1) if qi + 1 < N_CHUNK else None
        for _ in gen:
            pass
        gen = following


def _store_out(o_ref, sz_ref, o):
    for r in range(GQA):
        rows = slice(r * HEAD_DIM, (r + 1) * HEAD_DIM)
        o_ref[0, rows, :] = (o[:, r * TQ:(r + 1) * TQ] * sz_ref[0, rows, :]).astype(BF16)


def _nsa_attn_kernel(q_ref, kc_ref, vc_ref, ks_ref, vs_ref, kw_ref, vw_ref, gt_ref, sz_ref,
                     selw_ref, o_ref, *scratch):
    _for_each_query_tile(_nsa_attn_tile, (q_ref, gt_ref, sz_ref, o_ref),
                         (kc_ref, vc_ref, ks_ref, vs_ref, kw_ref, vw_ref, selw_ref), scratch)


def _nsa_attn_tile(qi, q_ref, gt_ref, sz_ref, o_ref, kc_ref, vc_ref, ks_ref, vs_ref, kw_ref, vw_ref,
                   selw_ref, score_ref, bias_ref, *bufs):
    q4 = _group_queries(q_ref)

    sel_bufs, win_bufs = bufs[:len(bufs) // 2], bufs[len(bufs) // 2:]
    keep_lower = _causal_fix(True)

    n_win = min(qi, 1)
    win = _FlashBranch(q_ref, kw_ref, vw_ref, win_bufs, qi, lambda j: qi - 1,
                       own_fix=keep_lower, past_fix=_causal_fix(False))

    t = qi * TQ + (lax.broadcasted_iota(jnp.int32, (1, LANES_Q), 1) & (TQ - 1))
    n_need = (qi + 1) * (TQ // CMP_STRIDE)
    n_idx = lax.broadcasted_iota(jnp.int32, (n_need, LANES_Q), 0)
    s_cmp = _dot(kc_ref[0, 0, 0:n_need, :], q4)
    win.own()
    s = jnp.where(n_idx * CMP_STRIDE + (CMP_LEN - 1) <= t, s_cmp, NEG)
    m = jnp.max(s, axis=0, keepdims=True)
    p = jnp.exp2(s - m)
    if qi == 0:
        p = p * (m > 0.5 * NEG).astype(F32)
    p = p * (1.0 / jnp.maximum(jnp.sum(p, axis=0, keepdims=True), 1e-30))

    def pad_rows(a):
        if n_need == N_CMP_PAD:
            return a
        return jnp.concatenate([a, jnp.zeros((N_CMP_PAD - n_need, a.shape[1]), a.dtype)], axis=0)

    o_cmp = _dot(vc_ref[0, 0], pad_rows(p).astype(BF16))

    psum = p[:, 0:TQ]
    for r in range(1, GQA):
        psum = psum + p[:, r * TQ:(r + 1) * TQ]
    psum = pad_rows(psum)
    p_hi = psum.astype(BF16)
    p_lo = (psum - p_hi.astype(F32)).astype(BF16)
    imp = _dot(selw_ref[...], p_hi) + _dot(selw_ref[...], p_lo)
    j_idx = lax.broadcasted_iota(jnp.int32, (N_SEL, TQ), 0)
    cur = qi * (TQ // SEL_BLOCK) + (lax.broadcasted_iota(jnp.int32, (N_SEL, TQ), 1) >> 6)
    valid = j_idx <= cur
    forced = (j_idx == 0) | (j_idx == cur) | (j_idx == cur - 1)
    score_ref[...] = jnp.where(forced, jnp.inf, jnp.where(valid, imp, -jnp.inf))
    win.run_past(n_win)
    per_chunk = TQ // SEL_BLOCK
    n_live = (qi + 1) * per_chunk
    ranked = _rank_select(score_ref, n_live, SEL_COUNT)
    sel = valid[:ranked.shape[0]] & ranked
    bias = jnp.where(sel, 0.0, NEG)
    bias = jnp.concatenate([bias] * GQA, axis=1)
    bias_ref[0:qi + 1] = jnp.zeros((qi + 1,) + bias_ref.shape[1:], F32)
    for jp in range(n_live):
        bias_ref[jp // per_chunk, jp % per_chunk:jp % per_chunk + 1, :] = bias[jp:jp + 1, :]

    o_win = win.finish(n_win)
    yield

    def sel_rows(j):
        return lambda cs: bias_ref[j, :, cs].astype(BF16)

    slc = _FlashBranch(q_ref, ks_ref, vs_ref, sel_bufs, qi, lambda j: j, own_fix=keep_lower,
                       own_rows=sel_rows(qi), past_rows=sel_rows)
    slc.own()
    yield
    slc.run_past(qi)
    o_slc = slc.finish(qi)

    def gate(br):
        return jnp.concatenate(
            [gt_ref[0, br * GQA + r:br * GQA + r + 1, :] for r in range(GQA)], axis=1)

    o = gate(0) * o_cmp + gate(1) * o_slc + gate(2) * o_win
    _store_out(o_ref, sz_ref, o)


def _group_spec(*tail):
    return pl.BlockSpec((1, 1) + tail, lambda b, g: (b, g) + (0,) * len(tail))


def _attn_specs():
    q_spec = _group_spec(N_CHUNK, GQA * HEAD_DIM, TQ)
    k_spec = lambda width: _group_spec(N_CHUNK, TQ, width)
    v_spec = _group_spec(N_CHUNK, V_AUG, TQ)
    return q_spec, k_spec, v_spec


def _nsa_attn_call(q, kc, vc, ks, vs, kw, vw, gt, sz, selw):
    q_spec, k_spec, v_spec = _attn_specs()
    return pl.pallas_call(
        _nsa_attn_kernel,
        grid=(BATCH, N_KV),
        in_specs=[
            q_spec,
            _group_spec(N_CMP_PAD, HEAD_DIM), _group_spec(HEAD_DIM, N_CMP_PAD),
            k_spec(K_AUG), v_spec, k_spec(HEAD_DIM), v_spec,
            _group_spec(N_CHUNK, GATE_ROWS, TQ),
            q_spec,
            pl.BlockSpec((N_SEL, N_CMP_PAD), lambda b, g: (0, 0)),
        ],
        out_specs=q_spec,
        out_shape=_tile_shape(GQA * HEAD_DIM, BF16),
        scratch_shapes=SCRATCH_SETS * ([
            pltpu.VMEM((N_SEL, TQ), F32),
            pltpu.VMEM((N_CHUNK, BIAS_ROWS, LANES_Q), F32),
        ] + _flash_scratch() + _flash_scratch()),
        compiler_params=_cparams(("parallel", "parallel")),
        name="nsa_attention",
    )(q, kc, vc, ks, vs, kw, vw, gt, sz, selw)


def _moba_attn_kernel(q_ref, km_ref, k_ref, v_ref, sz_ref, o_ref,
                      *scratch):
    _for_each_query_tile(_moba_attn_tile, (q_ref, sz_ref, o_ref), (km_ref, k_ref, v_ref), scratch)


def _moba_attn_tile(qi, q_ref, sz_ref, o_ref, km_ref, k_ref, v_ref, score_ref, bias_ref, *bufs):
    q4 = _group_queries(q_ref)

    if qi > 0:
        j_idx = lax.broadcasted_iota(jnp.int32, (N_CHUNK, LANES_Q), 0)
        past = j_idx < qi
        score_ref[...] = jnp.where(past, _dot(km_ref[0, 0], q4), -jnp.inf)
        sel = past & _rank_select(score_ref, qi, MOBA_TOPK)
        bias = jnp.where(sel, 0.0, NEG)
        bias_ref[0:qi] = jnp.zeros((qi,) + bias_ref.shape[1:], F32)
        for jp in range(qi):
            bias_ref[jp, 0:1, :] = bias[jp:jp + 1, :]

    yield

    branch = _FlashBranch(q_ref, k_ref, v_ref, bufs, qi, lambda j: j, own_fix=_causal_fix(True),
                          own_rows=lambda cs: jnp.zeros((BIAS_ROWS, COL), BF16),
                          past_rows=lambda j: (lambda cs: bias_ref[j, :, cs].astype(BF16)))
    branch.own()
    yield
    branch.run_past(qi)
    _store_out(o_ref, sz_ref, branch.finish(qi))


def _moba_attn_call(q, km, k, v, sz):
    q_spec, k_spec, v_spec = _attn_specs()
    return pl.pallas_call(
        _moba_attn_kernel,
        grid=(BATCH, N_KV),
        in_specs=[
            q_spec,
            _group_spec(N_CHUNK, HEAD_DIM),
            k_spec(K_AUG), v_spec, q_spec,
        ],
        out_specs=q_spec,
        out_shape=_tile_shape(GQA * HEAD_DIM, BF16),
        scratch_shapes=SCRATCH_SETS * ([
            pltpu.VMEM((N_CHUNK, LANES_Q), F32),
            pltpu.VMEM((N_CHUNK, BIAS_ROWS, LANES_Q), F32),
        ] + _flash_scratch()),
        compiler_params=_cparams(("parallel", "parallel")),
        name="moba_attention",
    )(q, km, k, v, sz)


def _residual_update(oz_ref, w_ref, x_ref, gt_ref, store):
    group_rows = GQA * HEAD_DIM
    for it in range(TS // TQ):
        lanes = slice(it * TQ, (it + 1) * TQ)
        for cblk in range(D_MODEL // 256):
            rows = slice(cblk * 256, (cblk + 1) * 256)
            y = _dot(w_ref[rows, 0:group_rows], oz_ref[0, 0, it])
            for gi in range(1, N_KV):
                y = y + _dot(w_ref[rows, gi * group_rows:(gi + 1) * group_rows], oz_ref[0, gi, it])
            store(rows, lanes, x_ref[0, rows, lanes] + gt_ref[0, rows, :] * y)


def _out_kernel(oz_ref, w_ref, x_ref, gt_ref, o_ref):
    def store(rows, lanes, new_x):
        o_ref[0, lanes, rows] = new_x.T

    _residual_update(oz_ref, w_ref, x_ref, gt_ref, store)


def _pending_specs():
    return [_tile_spec(GQA * HEAD_DIM),
            pl.BlockSpec((D_MODEL, D_MODEL), lambda b, s: (0, 0), pipeline_mode=pl.Buffered(1)),
            _bcol_spec(D_MODEL)]


def _out_call(pending, xT):
    oz, wT, gate = pending
    oz_spec, w_spec, gate_spec = _pending_specs()
    return pl.pallas_call(
        _out_kernel,
        grid=(BATCH, SEQ // TS),
        in_specs=[oz_spec, w_spec, _fm_spec(D_MODEL), gate_spec],
        out_specs=pl.BlockSpec((1, TS, D_MODEL), lambda b, s: (b, s, 0)),
        out_shape=jax.ShapeDtypeStruct((BATCH, SEQ, D_MODEL), F32),
        compiler_params=_cparams(("parallel", "parallel")),
        name="out_proj",
    )(oz, wT, xT, gate)


def _rope_tables(pos):
    inv_freq = ROPE_THETA ** (-jnp.arange(0, 2 * ROT_HALF, 2, dtype=F32) / (2 * ROT_HALF))
    ang = pos.astype(F32)[..., None] * inv_freq
    return jnp.cos(ang), jnp.sin(ang)


def _gate_perm():
    perm = np.full((N_KV * GATE_ROWS,), 3 * N_HEADS, dtype=np.int32)
    for g in range(N_KV):
        for br in range(3):
            for r in range(GQA):
                perm[g * GATE_ROWS + br * GQA + r] = (g * GQA + r) * 3 + br
    return perm


def _sel_weights_T():
    cs = np.arange(N_CMP)[:, None] * CMP_STRIDE
    ss = np.arange(N_SEL)[None, :] * SEL_BLOCK
    shared = np.clip(np.minimum(cs + CMP_LEN, ss + SEL_BLOCK) - np.maximum(cs, ss), 0, None)
    w = np.zeros((N_CMP_PAD, N_SEL), np.float32)
    w[:N_CMP] = shared / CMP_LEN
    return jnp.asarray(w.T, dtype=BF16)


def _col(v):
    return v.reshape(-1, 1)


def _input_args(x, pending):
    if pending is None:
        return (x,), "tokens"
    return tuple(pending) + (x,), "pending"


def _nsa_layer(x, pending, ng, sc, sh, cosT, sinT, cosc, sinc, w_in, q_g, k_g,
               cmp_pe, cmp_w1, cmp_b1, cmp_w2, gate_b):
    widths = [1024] + [256] * 6 + [3 * N_HEADS, 1024]
    q, kc, vc, ks, vs, kw, vw, gl, z = jnp.split(w_in, np.cumsum(widths)[:-1].tolist(), axis=1)
    perm = _gate_perm()
    gl_p = jnp.concatenate([gl, jnp.zeros((D_MODEL, 1), F32)], axis=1)[:, perm]
    gb_p = jnp.concatenate([gate_b, jnp.zeros((1,), F32)])[perm]
    wT = jnp.concatenate([q, ks, kw, kc, vc, vs, vw, gl_p, z], axis=1).T.astype(BF16)

    x_args, source = _input_args(x, pending)
    qT, ks_c, kw_c, kc_c, vc_c, vs_c, vw_c, gates, sz, xT = _nsa_in_call(
        x_args, ng, sc, sh, wT, _col(q_g), _col(k_g[1]), _col(k_g[2]), _col(gb_p), cosT, sinT, source)

    half_blocks = lambda t: t.reshape(BATCH * N_KV, SEQ, HEAD_DIM)
    k_cmp = _compress_call(half_blocks(kc_c), cmp_pe[0], cmp_w1[0], cmp_b1[0], cmp_w2[0],
                           key_extras=(_col(k_g[0]), cosc, sinc))
    v_cmpT = _compress_call(half_blocks(vc_c), cmp_pe[1], cmp_w1[1], cmp_b1[1], cmp_w2[1])
    k_cmp = k_cmp.reshape(BATCH, N_KV, N_CMP_PAD, HEAD_DIM)
    v_cmpT = v_cmpT.reshape(BATCH, N_KV, HEAD_DIM, N_CMP_PAD)

    oz = _nsa_attn_call(qT, k_cmp, v_cmpT, ks_c, vs_c, kw_c, vw_c, gates, sz, _sel_weights_T())
    return oz, xT


def _moba_layer(x, pending, ng, sc, sh, cosT, sinT, w_in, q_g, k_g):
    x_args, source = _input_args(x, pending)
    qT, k_c, km, v_c, sz, xT = _moba_in_call(
        x_args, ng, sc, sh, w_in.T.astype(BF16), _col(q_g), _col(k_g), cosT, sinT, source)
    nb = TS // TQ
    km = km[..., :nb].reshape(BATCH, SEQ // TS, N_KV, HEAD_DIM, nb)
    km = km.transpose(0, 2, 1, 4, 3).reshape(BATCH, N_KV, N_CHUNK, HEAD_DIM).astype(BF16)
    return _moba_attn_call(qT, km, k_c, v_c, sz), xT


@jax.jit
def _forward(x, c, positions, norm_g, ada_w, ada_b, nsa_w_in, nsa_w_out, nsa_q_norm, nsa_k_norm,
             nsa_cmp_pe, nsa_cmp_w1, nsa_cmp_b1, nsa_cmp_w2, nsa_gate_b,
             moba_w_in, moba_w_out, moba_q_norm, moba_k_norm):
    cos, sin = _rope_tables(positions)
    cosT = cos.transpose(0, 2, 1)
    sinT = sin.transpose(0, 2, 1)
    cmp_end = np.minimum(np.arange(N_CMP_PAD) * CMP_STRIDE + CMP_LEN - 1, SEQ - 1)
    cos_c, sin_c = _rope_tables(positions[:, cmp_end])

    def cmp_table(t):
        t = jnp.broadcast_to(t.transpose(2, 0, 1)[:, :, None, :],
                             (ROT_HALF, BATCH, N_KV, N_CMP_PAD))
        return t.reshape(ROT_HALF, CMP_COLS)

    cosc = cmp_table(cos_c)
    sinc = cmp_table(sin_c)

    mod = _ada_call(c, ada_w, ada_b)
    shift, scale, gate = jnp.split(mod[..., None], 3, axis=2)

    pending = None
    for i in range(DEPTH):
        j = i // 2
        ng = _col(norm_g[i])
        if i % 2 == 0:
            oz, x = _nsa_layer(x, pending, ng, scale[i], shift[i], cosT, sinT, cosc, sinc,
                               nsa_w_in[j], nsa_q_norm[j], nsa_k_norm[j], nsa_cmp_pe[j],
                               nsa_cmp_w1[j], nsa_cmp_b1[j], nsa_cmp_w2[j], nsa_gate_b[j])
            w_out = nsa_w_out[j]
        else:
            oz, x = _moba_layer(x, pending, ng, scale[i], shift[i], cosT, sinT,
                                moba_w_in[j], moba_q_norm[j], moba_k_norm[j])
            w_out = moba_w_out[j]
        pending = (oz, w_out.T.astype(BF16), gate[i])
    return _out_call(pending, x)


def kernel(x, c, positions, norm_g, ada_w, ada_b, nsa_w_in, nsa_w_out, nsa_q_norm, nsa_k_norm, nsa_cmp_pe, nsa_cmp_w1, nsa_cmp_b1, nsa_cmp_w2, nsa_gate_b, moba_w_in, moba_w_out, moba_q_norm, moba_k_norm):
    return _forward(x, c, positions, norm_g, ada_w, ada_b, nsa_w_in, nsa_w_out, nsa_q_norm,
                    nsa_k_norm, nsa_cmp_pe, nsa_cmp_w1, nsa_cmp_b1, nsa_cmp_w2, nsa_gate_b,
                    moba_w_in, moba_w_out, moba_q_norm, moba_k_norm)
```

```python
import functools

import numpy as np
import jax
import jax.numpy as jnp
from jax import lax
from jax.experimental import pallas as pl
from jax.experimental.pallas import tpu as pltpu

D_MODEL = 1024
BATCH = 16
SEQ = 2048
DEPTH = 4
HEAD_DIM = 64
N_HEADS = 16
N_KV = 4
GQA = 4
ROT_HALF = 8
ROPE_THETA = 500000.0
NORM_EPS = 1e-6
CMP_LEN = 32
CMP_STRIDE = 16
CMP_HID = 256
SEL_BLOCK = 64
SEL_COUNT = 8
N_SEL = SEQ // SEL_BLOCK
N_CMP = (SEQ - CMP_LEN) // CMP_STRIDE + 1
N_CMP_PAD = 128
MOBA_TOPK = 3

TQ = 256
N_CHUNK = SEQ // TQ
LANES_Q = GQA * TQ
TS = 512
NEG = -1e30
BIAS_ROWS = 16
K_AUG = HEAD_DIM + BIAS_ROWS
V_AUG = HEAD_DIM + 16
PV_COL = 256
Q_SCALE = HEAD_DIM ** -0.5 * float(np.log2(np.e))
GATE_ROWS = 16

F32 = jnp.float32
BF16 = jnp.bfloat16

VMEM_LIMIT = 52 * 1024 * 1024


def _cparams(sem):
    return pltpu.CompilerParams(dimension_semantics=sem, vmem_limit_bytes=VMEM_LIMIT)


def _dot(a, b):
    return jnp.dot(a, b, preferred_element_type=F32)


def _ada_kernel(c_ref, w_ref, b_ref, o_ref):
    cond = c_ref[...]
    cond = cond * jax.nn.sigmoid(cond)
    o_ref[0] = jnp.dot(cond, w_ref[0], precision=lax.Precision.HIGHEST,
                       preferred_element_type=F32) + b_ref[0]


def _ada_call(c, ada_w, ada_b):
    nt = 1024
    return pl.pallas_call(
        _ada_kernel,
        grid=(DEPTH, 3 * D_MODEL // nt),
        in_specs=[
            pl.BlockSpec((BATCH, D_MODEL), lambda i, n: (0, 0)),
            pl.BlockSpec((1, D_MODEL, nt), lambda i, n: (i, 0, n)),
            pl.BlockSpec((1, 1, nt), lambda i, n: (i, 0, n)),
        ],
        out_specs=pl.BlockSpec((1, BATCH, nt), lambda i, n: (i, 0, n)),
        out_shape=jax.ShapeDtypeStruct((DEPTH, BATCH, 3 * D_MODEL), F32),
        compiler_params=_cparams(("parallel", "parallel")),
        name="ada_mod",
    )(c, ada_w, ada_b.reshape(DEPTH, 1, 3 * D_MODEL))


def _input_tile(refs, source):
    if source == "pending":
        oz_ref, wo_ref, gate_ref, x_ref, *rest = refs
        *rest, new_x_ref = rest

        def store(rows, lanes, value):
            new_x_ref[0, rows, lanes] = value

        _residual_update(oz_ref, wo_ref, x_ref, gate_ref, store)
        return new_x_ref[0], rest
    assert source == "tokens"
    x_ref, *rest, xT_ref = refs
    x = x_ref[0].T
    xT_ref[0] = x
    return x, rest


def _input_specs(source):
    if source == "pending":
        return _pending_specs() + [_fm_spec(D_MODEL)]
    return [pl.BlockSpec((1, TS, D_MODEL), lambda b, s: (b, s, 0))]


def _norm_mod(x, ng_ref, sc_ref, sh_ref):
    ms = jnp.mean(x * x, axis=0, keepdims=True)
    y = x * lax.rsqrt(ms + NORM_EPS)
    h = (y * ng_ref[...]) * (1.0 + sc_ref[0]) + sh_ref[0]
    return h.astype(BF16)


def _head_norm_rope(y, g, cos, sin):
    ms = jnp.mean(y * y, axis=0, keepdims=True)
    yn = (y * lax.rsqrt(ms + NORM_EPS)) * g
    x1 = yn[0:ROT_HALF]
    x2 = yn[ROT_HALF:2 * ROT_HALF]
    return jnp.concatenate([x1 * cos - x2 * sin, x2 * cos + x1 * sin, yn[2 * ROT_HALF:]], axis=0)


def _store_k_chunks(o_ref, gi, yh, ind=None):
    for it in range(TS // TQ):
        chunk = yh[:, it * TQ:(it + 1) * TQ]
        if ind is None:
            o_ref[0, gi, it] = chunk.T.astype(BF16)
        else:
            aug = jnp.concatenate([chunk, ind], axis=0).T
            o_ref[0, gi, it] = aug[:, :K_AUG].astype(BF16)


def _store_tiles(o_ref, gi, rows, y):
    for it in range(TS // TQ):
        o_ref[0, gi, it, rows, :] = y[:, it * TQ:(it + 1) * TQ]


def _project_queries(proj, off, qg_ref, cos, sin, q_ref):
    for gi in range(N_KV):
        y = proj(off + gi * 256, 256)
        for r in range(GQA):
            rows = slice(r * HEAD_DIM, (r + 1) * HEAD_DIM)
            yh = _head_norm_rope(y[rows], qg_ref[...], cos, sin)
            _store_tiles(q_ref, gi, rows, (yh * Q_SCALE).astype(BF16))


def _project_silu(proj, off, sz_ref):
    for gi in range(N_KV):
        z = proj(off + gi * 256, 256)
        _store_tiles(sz_ref, gi, slice(None), z * jax.nn.sigmoid(z))


def _block_indicator(blocks):
    row = lax.broadcasted_iota(jnp.int32, (HEAD_DIM, TQ), 0)
    lane = lax.broadcasted_iota(jnp.int32, (HEAD_DIM, TQ), 1)
    return (lane // (TQ // blocks) == row).astype(F32)


def _store_v_chunks(o_ref, y):
    yb = y.astype(BF16)
    row = lax.broadcasted_iota(jnp.int32, (V_AUG - HEAD_DIM, TQ), 0)
    ones_row = (row == 0).astype(BF16)
    for gi in range(N_KV):
        for it in range(TS // TQ):
            v = yb[gi * HEAD_DIM:(gi + 1) * HEAD_DIM, it * TQ:(it + 1) * TQ]
            o_ref[0, gi, it] = jnp.concatenate([v, ones_row], axis=0)


NSA_OFF_Q = 0
NSA_OFF_KS = 1024
NSA_OFF_KW = 1280
NSA_OFF_KC = 1536
NSA_OFF_VC = 1792
NSA_OFF_VS = 2048
NSA_OFF_VW = 2304
NSA_OFF_GT = 2560
NSA_OFF_Z = NSA_OFF_GT + N_KV * GATE_ROWS
NSA_ROWS = NSA_OFF_Z + D_MODEL


def _nsa_in_kernel(*refs, source):
    x, refs = _input_tile(refs, source)
    (ng_ref, sc_ref, sh_ref, w_ref, qg_ref, ksg_ref, kwg_ref, gb_ref, cos_ref, sin_ref,
     q_ref, ks_ref, kw_ref, kc_ref, vc_ref, vs_ref, vw_ref, gt_ref, sz_ref) = refs
    hb = _norm_mod(x, ng_ref, sc_ref, sh_ref)
    cos = cos_ref[0]
    sin = sin_ref[0]

    def proj(r0, n):
        return _dot(w_ref[r0:r0 + n, :], hb)

    _project_queries(proj, NSA_OFF_Q, qg_ref, cos, sin, q_ref)
    sel_ind = _block_indicator(TQ // SEL_BLOCK)
    for off, g_ref, o_ref, ind in ((NSA_OFF_KS, ksg_ref, ks_ref, sel_ind),
                                   (NSA_OFF_KW, kwg_ref, kw_ref, None)):
        y = proj(off, 256)
        for gi in range(N_KV):
            yh = _head_norm_rope(y[gi * HEAD_DIM:(gi + 1) * HEAD_DIM], g_ref[...], cos, sin)
            _store_k_chunks(o_ref, gi, yh, ind)
    for off, o_ref in ((NSA_OFF_KC, kc_ref), (NSA_OFF_VC, vc_ref)):
        y = proj(off, 256)
        for gi in range(N_KV):
            for it in range(TS // TQ):
                o_ref[0, gi, it] = y[gi * HEAD_DIM:(gi + 1) * HEAD_DIM, it * TQ:(it + 1) * TQ].T
    _store_v_chunks(vs_ref, proj(NSA_OFF_VS, 256))
    _store_v_chunks(vw_ref, proj(NSA_OFF_VW, 256))
    gates = jax.nn.sigmoid(proj(NSA_OFF_GT, N_KV * GATE_ROWS) + gb_ref[...])
    for gi in range(N_KV):
        _store_tiles(gt_ref, gi, slice(None), gates[gi * GATE_ROWS:(gi + 1) * GATE_ROWS])
    _project_silu(proj, NSA_OFF_Z, sz_ref)


def _col_spec(rows):
    return pl.BlockSpec((rows, 1), lambda b, s: (0, 0))


def _bcol_spec(rows):
    return pl.BlockSpec((1, rows, 1), lambda b, s: (b, 0, 0))


def _fm_spec(rows):
    return pl.BlockSpec((1, rows, TS), lambda b, s: (b, 0, s))


def _k_chunk_spec(width):
    return pl.BlockSpec((1, N_KV, TS // TQ, TQ, width), lambda b, s: (b, 0, s, 0, 0))


def _k_chunk_shape(width, dtype=BF16):
    return jax.ShapeDtypeStruct((BATCH, N_KV, N_CHUNK, TQ, width), dtype)


def _tile_spec(rows):
    return pl.BlockSpec((1, N_KV, TS // TQ, rows, TQ), lambda b, s: (b, 0, s, 0, 0))


def _tile_shape(rows, dtype):
    return jax.ShapeDtypeStruct((BATCH, N_KV, N_CHUNK, rows, TQ), dtype)


_V_CHUNK_SPEC = pl.BlockSpec((1, N_KV, TS // TQ, V_AUG, TQ), lambda b, s: (b, 0, s, 0, 0))
_V_CHUNK_SHAPE = jax.ShapeDtypeStruct((BATCH, N_KV, N_CHUNK, V_AUG, TQ), BF16)


def _resident_spec(shape):
    return pl.BlockSpec(shape, lambda b, s: (0,) * len(shape), pipeline_mode=pl.Buffered(1))


def _residual_out():
    return [_fm_spec(D_MODEL)], [jax.ShapeDtypeStruct((BATCH, D_MODEL, SEQ), F32)]


def _nsa_in_call(x_args, ng, sc, sh, wT, qg, ksg, kwg, gb, cosT, sinT, source):
    x_specs = _input_specs(source)
    extra_specs, extra_shapes = _residual_out()
    return pl.pallas_call(
        functools.partial(_nsa_in_kernel, source=source),
        grid=(BATCH, SEQ // TS),
        in_specs=x_specs + [
            _col_spec(D_MODEL), _bcol_spec(D_MODEL), _bcol_spec(D_MODEL),
            _resident_spec((NSA_ROWS, D_MODEL)),
            _col_spec(HEAD_DIM), _col_spec(HEAD_DIM), _col_spec(HEAD_DIM),
            _col_spec(N_KV * GATE_ROWS),
            _fm_spec(ROT_HALF), _fm_spec(ROT_HALF),
        ],
        out_specs=[
            _tile_spec(GQA * HEAD_DIM), _k_chunk_spec(K_AUG), _k_chunk_spec(HEAD_DIM),
            _k_chunk_spec(HEAD_DIM), _k_chunk_spec(HEAD_DIM),
            _V_CHUNK_SPEC, _V_CHUNK_SPEC, _tile_spec(GATE_ROWS), _tile_spec(GQA * HEAD_DIM),
        ] + extra_specs,
        out_shape=[
            _tile_shape(GQA * HEAD_DIM, BF16), _k_chunk_shape(K_AUG), _k_chunk_shape(HEAD_DIM),
            _k_chunk_shape(HEAD_DIM, F32), _k_chunk_shape(HEAD_DIM, F32),
            _V_CHUNK_SHAPE, _V_CHUNK_SHAPE, _tile_shape(GATE_ROWS, F32),
            _tile_shape(GQA * HEAD_DIM, F32),
        ] + extra_shapes,
        compiler_params=_cparams(("parallel", "parallel")),
        name="nsa_in_proj",
    )(*x_args, ng, sc, sh, wT, qg, ksg, kwg, gb, cosT, sinT)


MOBA_OFF_Q = 0
MOBA_OFF_K = 1024
MOBA_OFF_V = 1280
MOBA_OFF_Z = 1536
MOBA_ROWS = 2560
KM_LANES = 128


def _moba_in_kernel(*refs, source):
    x, refs = _input_tile(refs, source)
    (ng_ref, sc_ref, sh_ref, w_ref, qg_ref, kg_ref, cos_ref, sin_ref,
     q_ref, k_ref, km_ref, v_ref, sz_ref) = refs
    hb = _norm_mod(x, ng_ref, sc_ref, sh_ref)
    cos = cos_ref[0]
    sin = sin_ref[0]

    def proj(r0, n):
        return _dot(w_ref[r0:r0 + n, :], hb)

    _project_queries(proj, MOBA_OFF_Q, qg_ref, cos, sin, q_ref)
    y = proj(MOBA_OFF_K, 256)
    lane = lax.broadcasted_iota(jnp.int32, (HEAD_DIM, KM_LANES), 1)
    for gi in range(N_KV):
        yh = _head_norm_rope(y[gi * HEAD_DIM:(gi + 1) * HEAD_DIM], kg_ref[...], cos, sin)
        _store_k_chunks(k_ref, gi, yh, _block_indicator(1))
        km = jnp.zeros((HEAD_DIM, KM_LANES), F32)
        for it in range(TS // TQ):
            mean = jnp.mean(yh[:, it * TQ:(it + 1) * TQ], axis=1, keepdims=True)
            km = jnp.where(lane == it, mean, km)
        km_ref[0, 0, gi * HEAD_DIM:(gi + 1) * HEAD_DIM, :] = km
    _store_v_chunks(v_ref, proj(MOBA_OFF_V, 256))
    _project_silu(proj, MOBA_OFF_Z, sz_ref)


def _moba_in_call(x_args, ng, sc, sh, wT, qg, kg, cosT, sinT, source):
    x_specs = _input_specs(source)
    extra_specs, extra_shapes = _residual_out()
    return pl.pallas_call(
        functools.partial(_moba_in_kernel, source=source),
        grid=(BATCH, SEQ // TS),
        in_specs=x_specs + [
            _col_spec(D_MODEL), _bcol_spec(D_MODEL), _bcol_spec(D_MODEL),
            _resident_spec((MOBA_ROWS, D_MODEL)),
            _col_spec(HEAD_DIM), _col_spec(HEAD_DIM),
            _fm_spec(ROT_HALF), _fm_spec(ROT_HALF),
        ],
        out_specs=[
            _tile_spec(GQA * HEAD_DIM), _k_chunk_spec(K_AUG),
            pl.BlockSpec((1, 1, N_KV * HEAD_DIM, KM_LANES), lambda b, s: (b, s, 0, 0)),
            _V_CHUNK_SPEC, _tile_spec(GQA * HEAD_DIM),
        ] + extra_specs,
        out_shape=[
            _tile_shape(GQA * HEAD_DIM, BF16), _k_chunk_shape(K_AUG),
            jax.ShapeDtypeStruct((BATCH, SEQ // TS, N_KV * HEAD_DIM, KM_LANES), F32),
            _V_CHUNK_SHAPE, _tile_shape(GQA * HEAD_DIM, F32),
        ] + extra_shapes,
        compiler_params=_cparams(("parallel", "parallel")),
        name="moba_in_proj",
    )(*x_args, ng, sc, sh, wT, qg, kg, cosT, sinT)


CMP_NB = 8
CMP_ROWS = CMP_NB * N_CMP_PAD
CMP_COLS = BATCH * N_KV * N_CMP_PAD
CMP_FEAT = CMP_STRIDE * HEAD_DIM


def _compress_mlp(x_ref, pe_ref, w1a_ref, w1b_ref, b1_ref, w2t_ref):
    u = v = None
    for l in range(CMP_STRIDE):
        xl = jnp.concatenate([x_ref[i, pl.ds(l, N_CMP_PAD, stride=CMP_STRIDE), :]
                              for i in range(CMP_NB)], axis=0)
        cols = slice(l * HEAD_DIM, (l + 1) * HEAD_DIM)
        ul = _dot((xl + pe_ref[l:l + 1, :]).astype(BF16), w1a_ref[cols, :])
        vl = _dot((xl + pe_ref[CMP_STRIDE + l:CMP_STRIDE + l + 1, :]).astype(BF16), w1b_ref[cols, :])
        u = ul if u is None else u + ul
        v = vl if v is None else v + vl
    v = pltpu.roll(v, CMP_ROWS - 1, 0)
    h = u + v + b1_ref[...]
    h = 0.5 * h * (1.0 + jnp.tanh(np.sqrt(2.0 / np.pi) * (h + 0.044715 * (h * h * h))))
    return lax.dot_general(w2t_ref[...], h.astype(BF16), (((1,), (1,)), ((), ())),
                           preferred_element_type=F32)


def _compress_key_kernel(x_ref, pe_ref, w1a_ref, w1b_ref, b1_ref, w2t_ref, kg_ref,
                         cos_ref, sin_ref, o_ref):
    out = _compress_mlp(x_ref, pe_ref, w1a_ref, w1b_ref, b1_ref, w2t_ref)
    out = _head_norm_rope(out, kg_ref[...], cos_ref[...], sin_ref[...])
    for i in range(CMP_NB):
        o_ref[i] = out[:, i * N_CMP_PAD:(i + 1) * N_CMP_PAD].T.astype(BF16)


def _compress_value_kernel(x_ref, pe_ref, w1a_ref, w1b_ref, b1_ref, w2t_ref, o_ref):
    out = _compress_mlp(x_ref, pe_ref, w1a_ref, w1b_ref, b1_ref, w2t_ref)
    for i in range(CMP_NB):
        o_ref[i] = out[:, i * N_CMP_PAD:(i + 1) * N_CMP_PAD].astype(BF16)


def _compress_call(x, pe, w1, b1, w2, key_extras=None):
    full = lambda shape: pl.BlockSpec(shape, lambda t: (0,) * len(shape))
    in_specs = [
        pl.BlockSpec((CMP_NB, SEQ, HEAD_DIM), lambda t: (t, 0, 0)),
        full((CMP_LEN, HEAD_DIM)),
        full((CMP_FEAT, CMP_HID)), full((CMP_FEAT, CMP_HID)),
        full((1, CMP_HID)), full((HEAD_DIM, CMP_HID)),
    ]
    w1 = w1.astype(BF16)
    args = [x, pe, w1[:CMP_FEAT], w1[CMP_FEAT:], b1.reshape(1, CMP_HID), w2.T.astype(BF16)]
    if key_extras is not None:
        in_specs += [full((HEAD_DIM, 1)),
                     pl.BlockSpec((ROT_HALF, CMP_ROWS), lambda t: (0, t)),
                     pl.BlockSpec((ROT_HALF, CMP_ROWS), lambda t: (0, t))]
        args += list(key_extras)
        kernel, out_tail = _compress_key_kernel, (N_CMP_PAD, HEAD_DIM)
    else:
        kernel, out_tail = _compress_value_kernel, (HEAD_DIM, N_CMP_PAD)
    return pl.pallas_call(
        kernel,
        grid=(BATCH * N_KV // CMP_NB,),
        in_specs=in_specs,
        out_specs=pl.BlockSpec((CMP_NB,) + out_tail, lambda t: (t, 0, 0)),
        out_shape=jax.ShapeDtypeStruct((BATCH * N_KV,) + out_tail, BF16),
        compiler_params=_cparams(("parallel",)),
        name="nsa_compress_key" if key_extras is not None else "nsa_compress_value",
    )(*args)


def _group_queries(q_ref):
    return jnp.concatenate(
        [q_ref[0, r * HEAD_DIM:(r + 1) * HEAD_DIM, :] for r in range(GQA)], axis=1)


COL = 128


def _col_tiles():
    per_head = TQ // COL
    for ct in range(LANES_Q // COL):
        r, h = divmod(ct, per_head)
        yield r, slice(h * COL, (h + 1) * COL), slice(ct * COL, (ct + 1) * COL)


def _flash_scratch():
    return [
        pltpu.VMEM((2, TQ, LANES_Q), BF16),
        pltpu.VMEM((2, 1, LANES_Q), F32),
        pltpu.VMEM((1, LANES_Q), F32),
        pltpu.VMEM((V_AUG, LANES_Q), F32),
    ]


class _FlashBranch:
    def __init__(self, q_ref, k_ref, v_ref, bufs, own, past_chunk, own_fix=None, past_fix=None,
                 own_rows=None, past_rows=None):
        self.q_ref, self.k_ref, self.v_ref, self.bufs = q_ref, k_ref, v_ref, bufs
        self.own_chunk, self.past_chunk = own, past_chunk
        self.own_fix, self.past_fix, self.own_rows, self.past_rows = own_fix, past_fix, own_rows, past_rows

    def _softmax_tile(self, s, krows, cs, slot, init):
        p_buf, a_buf, m_ref, _ = self.bufs
        m_loc = jnp.max(s, axis=0, keepdims=True)
        if init:
            m_new = m_loc
            a_buf[slot, :, cs] = jnp.ones((1, COL), F32)
        else:
            m_old = m_ref[:, cs]
            m_new = jnp.maximum(m_old, m_loc)
            a_buf[slot, :, cs] = jnp.exp2(m_old - m_new)
        m_ref[:, cs] = m_new
        p_buf[slot, krows, cs] = jnp.exp2(s - m_new).astype(BF16)
        if krows.start > 0:
            p_buf[slot, 0:krows.start, cs] = jnp.zeros((krows.start, COL), BF16)
        if krows.stop < TQ:
            p_buf[slot, krows.stop:TQ, cs] = jnp.zeros((TQ - krows.stop, COL), BF16)

    def _scores(self, k, r, qs, cs, rows, fix):
        q = self.q_ref[0, r * HEAD_DIM:(r + 1) * HEAD_DIM, qs]
        if rows is not None:
            q = jnp.concatenate([q, rows(cs)], axis=0)
        krows = slice(0, TQ) if fix is None else fix.key_rows(qs)
        s = _dot(k[krows, :], q)
        if fix is not None:
            s = fix(s, qs, krows)
        return s, krows

    def _pv_slab(self, vT, slot, ct):
        if (ct * COL) % PV_COL:
            return
        p_buf, a_buf, _, acc_ref = self.bufs
        cs = slice(ct * COL, ct * COL + PV_COL)
        acc_ref[:, cs] = a_buf[slot, :, cs] * acc_ref[:, cs] + _dot(vT, p_buf[slot, :, cs])

    def own(self):
        acc_ref = self.bufs[3]
        acc_ref[...] = jnp.zeros(acc_ref.shape, F32)
        k_own = self.k_ref[0, 0, self.own_chunk]
        tiles = list(_col_tiles())
        ahead = self._scores(k_own, *tiles[0], self.own_rows, self.own_fix)
        for ct, (r, qs, cs) in enumerate(tiles):
            s, krows = ahead
            if ct + 1 < len(tiles):
                ahead = self._scores(k_own, *tiles[ct + 1], self.own_rows, self.own_fix)
            self._softmax_tile(s, krows, cs, 0, True)

    def trip(self, j, cur, prv):
        k = self.k_ref[0, 0, self.past_chunk(j)]
        vT = self.v_ref[0, 0, self._chunk_before(j)]
        rows = None if self.past_rows is None else self.past_rows(j)
        tiles = list(_col_tiles())
        ahead = self._scores(k, *tiles[0], rows, self.past_fix)
        for ct, (r, qs, cs) in enumerate(tiles):
            s, krows = ahead
            if ct + 1 < len(tiles):
                ahead = self._scores(k, *tiles[ct + 1], rows, self.past_fix)
            self._pv_slab(vT, prv, ct)
            self._softmax_tile(s, krows, cs, cur, False)

    def run_past(self, n_past):
        if isinstance(n_past, int):
            for j in range(n_past):
                self.trip(j, 1 - j % 2, j % 2)
            return

        def pair(pp, carry):
            self.trip(2 * pp, 1, 0)
            self.trip(2 * pp + 1, 0, 1)
            return carry

        lax.fori_loop(0, n_past // 2, pair, 0)

        @pl.when(n_past % 2 == 1)
        def _():
            self.trip(n_past - 1, 1, 0)

    def _chunk_before(self, j):
        if isinstance(j, int):
            return self.own_chunk if j == 0 else self.past_chunk(j - 1)
        return jnp.where(j == 0, self.own_chunk, self.past_chunk(jnp.maximum(j - 1, 0)))

    def finish(self, n_past):
        vT = self.v_ref[0, 0, self._chunk_before(n_past)]
        for ct in range(LANES_Q // COL):
            self._pv_slab(vT, n_past % 2, ct)
        acc = self.bufs[3][...]
        return acc[:HEAD_DIM] / acc[HEAD_DIM:HEAD_DIM + 1]


class _CausalFix:
    def __init__(self, keep_lower):
        self.keep_lower = keep_lower

    def key_rows(self, qs):
        half = TQ // 2
        if self.keep_lower and qs.stop <= half:
            return slice(0, half)
        if not self.keep_lower and qs.start >= half:
            return slice(half, TQ)
        return slice(0, TQ)

    def __call__(self, s, qs, rows):
        assert rows.start <= qs.start and qs.stop <= rows.stop
        a_idx = lax.broadcasted_iota(jnp.int32, (COL, COL), 0)
        lane = lax.broadcasted_iota(jnp.int32, (COL, COL), 1)
        lo, hi = qs.start - rows.start, qs.stop - rows.start
        diag = jnp.where(a_idx <= lane if self.keep_lower else a_idx > lane, s[lo:hi], NEG)
        parts = [part for part in (s[:lo], diag, s[hi:]) if part.shape[0]]
        return parts[0] if len(parts) == 1 else jnp.concatenate(parts, axis=0)


def _causal_fix(keep_lower):
    return _CausalFix(keep_lower)


SUBLANES = 8


def _rank_select(score_ref, n_live, count):
    n_rows = -(-n_live // SUBLANES) * SUBLANES
    lanes = score_ref.shape[1]
    groups = [score_ref[g:g + SUBLANES, :] for g in range(0, n_rows, SUBLANES)]
    counts = [jnp.zeros((SUBLANES, lanes), jnp.int32) for _ in groups]
    j_in_group = lax.broadcasted_iota(jnp.int32, (SUBLANES, lanes), 0)
    for jp in range(n_live):
        row = score_ref[jp:jp + 1, :]
        for gi, sg in enumerate(groups):
            first = gi * SUBLANES
            if first > jp:
                beats = row >= sg
            elif first + SUBLANES - 1 <= jp:
                beats = row > sg
            else:
                beats = (row > sg) | ((row == sg) & (j_in_group + first > jp))
            counts[gi] = counts[gi] + beats.astype(jnp.int32)
    return jnp.concatenate(counts, axis=0) < count


class _TileView:
    def __init__(self, ref, tile):
        self.ref, self.tile = ref, tile

    def __getitem__(self, idx):
        return self.ref[(0, 0, self.tile) + tuple(idx[1:])]

    def __setitem__(self, idx, value):
        self.ref[(0, 0, self.tile) + tuple(idx[1:])] = value


SCRATCH_SETS = 2


def _for_each_query_tile(tile_fn, tiled_refs, other_refs, scratch_refs):
    per_set = len(scratch_refs) // SCRATCH_SETS

    def start(qi):
        first = (qi % SCRATCH_SETS) * per_set
        gen = tile_fn(qi, *[_TileView(ref, qi) for ref in tiled_refs], *other_refs,
                      *scratch_refs[first:first + per_set])
        next(gen)
        return gen

    gen = start(0)
    for qi in range(N_CHUNK):
        next(gen)
        following = start(qi + 1) if qi + 1 < N_CHUNK else None
        for _ in gen:
            pass
        gen = following


def _store_out(o_ref, sz_ref, o):
    for r in range(GQA):
        rows = slice(r * HEAD_DIM, (r + 1) * HEAD_DIM)
        o_ref[0, rows, :] = (o[:, r * TQ:(r + 1) * TQ] * sz_ref[0, rows, :]).astype(BF16)


def _nsa_attn_kernel(q_ref, kc_ref, vc_ref, ks_ref, vs_ref, kw_ref, vw_ref, gt_ref, sz_ref,
                     selw_ref, o_ref, *scratch):
    _for_each_query_tile(_nsa_attn_tile, (q_ref, gt_ref, sz_ref, o_ref),
                         (kc_ref, vc_ref, ks_ref, vs_ref, kw_ref, vw_ref, selw_ref), scratch)


def _nsa_attn_tile(qi, q_ref, gt_ref, sz_ref, o_ref, kc_ref, vc_ref, ks_ref, vs_ref, kw_ref, vw_ref,
                   selw_ref, score_ref, bias_ref, *bufs):
    q4 = _group_queries(q_ref)

    sel_bufs, win_bufs = bufs[:len(bufs) // 2], bufs[len(bufs) // 2:]
    keep_lower = _causal_fix(True)

    n_win = min(qi, 1)
    win = _FlashBranch(q_ref, kw_ref, vw_ref, win_bufs, qi, lambda j: qi - 1,
                       own_fix=keep_lower, past_fix=_causal_fix(False))

    t = qi * TQ + (lax.broadcasted_iota(jnp.int32, (1, LANES_Q), 1) & (TQ - 1))
    n_need = (qi + 1) * (TQ // CMP_STRIDE)
    n_idx = lax.broadcasted_iota(jnp.int32, (n_need, LANES_Q), 0)
    s_cmp = _dot(kc_ref[0, 0, 0:n_need, :], q4)
    win.own()
    s = jnp.where(n_idx * CMP_STRIDE + (CMP_LEN - 1) <= t, s_cmp, NEG)
    m = jnp.max(s, axis=0, keepdims=True)
    p = jnp.exp2(s - m)
    if qi == 0:
        p = p * (m > 0.5 * NEG).astype(F32)
    p = p * (1.0 / jnp.maximum(jnp.sum(p, axis=0, keepdims=True), 1e-30))

    def pad_rows(a):
        if n_need == N_CMP_PAD:
            return a
        return jnp.concatenate([a, jnp.zeros((N_CMP_PAD - n_need, a.shape[1]), a.dtype)], axis=0)

    o_cmp = _dot(vc_ref[0, 0], pad_rows(p).astype(BF16))

    psum = p[:, 0:TQ]
    for r in range(1, GQA):
        psum = psum + p[:, r * TQ:(r + 1) * TQ]
    psum = pad_rows(psum)
    p_hi = psum.astype(BF16)
    p_lo = (psum - p_hi.astype(F32)).astype(BF16)
    imp = _dot(selw_ref[...], p_hi) + _dot(selw_ref[...], p_lo)
    j_idx = lax.broadcasted_iota(jnp.int32, (N_SEL, TQ), 0)
    cur = qi * (TQ // SEL_BLOCK) + (lax.broadcasted_iota(jnp.int32, (N_SEL, TQ), 1) >> 6)
    valid = j_idx <= cur
    forced = (j_idx == 0) | (j_idx == cur) | (j_idx == cur - 1)
    score_ref[...] = jnp.where(forced, jnp.inf, jnp.where(valid, imp, -jnp.inf))
    win.run_past(n_win)
    per_chunk = TQ // SEL_BLOCK
    n_live = (qi + 1) * per_chunk
    ranked = _rank_select(score_ref, n_live, SEL_COUNT)
    sel = valid[:ranked.shape[0]] & ranked
    bias = jnp.where(sel, 0.0, NEG)
    bias = jnp.concatenate([bias] * GQA, axis=1)
    bias_ref[0:qi + 1] = jnp.zeros((qi + 1,) + bias_ref.shape[1:], F32)
    for jp in range(n_live):
        bias_ref[jp // per_chunk, jp % per_chunk:jp % per_chunk + 1, :] = bias[jp:jp + 1, :]

    o_win = win.finish(n_win)
    yield

    def sel_rows(j):
        return lambda cs: bias_ref[j, :, cs].astype(BF16)

    slc = _FlashBranch(q_ref, ks_ref, vs_ref, sel_bufs, qi, lambda j: j, own_fix=keep_lower,
                       own_rows=sel_rows(qi), past_rows=sel_rows)
    slc.own()
    yield
    slc.run_past(qi)
    o_slc = slc.finish(qi)

    def gate(br):
        return jnp.concatenate(
            [gt_ref[0, br * GQA + r:br * GQA + r + 1, :] for r in range(GQA)], axis=1)

    o = gate(0) * o_cmp + gate(1) * o_slc + gate(2) * o_win
    _store_out(o_ref, sz_ref, o)


def _group_spec(*tail):
    return pl.BlockSpec((1, 1) + tail, lambda b, g: (b, g) + (0,) * len(tail))


def _attn_specs():
    q_spec = _group_spec(N_CHUNK, GQA * HEAD_DIM, TQ)
    k_spec = lambda width: _group_spec(N_CHUNK, TQ, width)
    v_spec = _group_spec(N_CHUNK, V_AUG, TQ)
    return q_spec, k_spec, v_spec


def _nsa_attn_call(q, kc, vc, ks, vs, kw, vw, gt, sz, selw):
    q_spec, k_spec, v_spec = _attn_specs()
    return pl.pallas_call(
        _nsa_attn_kernel,
        grid=(BATCH, N_KV),
        in_specs=[
            q_spec,
            _group_spec(N_CMP_PAD, HEAD_DIM), _group_spec(HEAD_DIM, N_CMP_PAD),
            k_spec(K_AUG), v_spec, k_spec(HEAD_DIM), v_spec,
            _group_spec(N_CHUNK, GATE_ROWS, TQ),
            q_spec,
            pl.BlockSpec((N_SEL, N_CMP_PAD), lambda b, g: (0, 0)),
        ],
        out_specs=q_spec,
        out_shape=_tile_shape(GQA * HEAD_DIM, BF16),
        scratch_shapes=SCRATCH_SETS * ([
            pltpu.VMEM((N_SEL, TQ), F32),
            pltpu.VMEM((N_CHUNK, BIAS_ROWS, LANES_Q), F32),
        ] + _flash_scratch() + _flash_scratch()),
        compiler_params=_cparams(("parallel", "parallel")),
        name="nsa_attention",
    )(q, kc, vc, ks, vs, kw, vw, gt, sz, selw)


def _moba_attn_kernel(q_ref, km_ref, k_ref, v_ref, sz_ref, o_ref,
                      *scratch):
    _for_each_query_tile(_moba_attn_tile, (q_ref, sz_ref, o_ref), (km_ref, k_ref, v_ref), scratch)


def _moba_attn_tile(qi, q_ref, sz_ref, o_ref, km_ref, k_ref, v_ref, score_ref, bias_ref, *bufs):
    q4 = _group_queries(q_ref)

    if qi > 0:
        j_idx = lax.broadcasted_iota(jnp.int32, (N_CHUNK, LANES_Q), 0)
        past = j_idx < qi
        score_ref[...] = jnp.where(past, _dot(km_ref[0, 0], q4), -jnp.inf)
        sel = past & _rank_select(score_ref, qi, MOBA_TOPK)
        bias = jnp.where(sel, 0.0, NEG)
        bias_ref[0:qi] = jnp.zeros((qi,) + bias_ref.shape[1:], F32)
        for jp in range(qi):
            bias_ref[jp, 0:1, :] = bias[jp:jp + 1, :]

    yield

    branch = _FlashBranch(q_ref, k_ref, v_ref, bufs, qi, lambda j: j, own_fix=_causal_fix(True),
                          own_rows=lambda cs: jnp.zeros((BIAS_ROWS, COL), BF16),
                          past_rows=lambda j: (lambda cs: bias_ref[j, :, cs].astype(BF16)))
    branch.own()
    yield
    branch.run_past(qi)
    _store_out(o_ref, sz_ref, branch.finish(qi))


def _moba_attn_call(q, km, k, v, sz):
    q_spec, k_spec, v_spec = _attn_specs()
    return pl.pallas_call(
        _moba_attn_kernel,
        grid=(BATCH, N_KV),
        in_specs=[
            q_spec,
            _group_spec(N_CHUNK, HEAD_DIM),
            k_spec(K_AUG), v_spec, q_spec,
        ],
        out_specs=q_spec,
        out_shape=_tile_shape(GQA * HEAD_DIM, BF16),
        scratch_shapes=SCRATCH_SETS * ([
            pltpu.VMEM((N_CHUNK, LANES_Q), F32),
            pltpu.VMEM((N_CHUNK, BIAS_ROWS, LANES_Q), F32),
        ] + _flash_scratch()),
        compiler_params=_cparams(("parallel", "parallel")),
        name="moba_attention",
    )(q, km, k, v, sz)


def _residual_update(oz_ref, w_ref, x_ref, gt_ref, store):
    group_rows = GQA * HEAD_DIM
    for it in range(TS // TQ):
        lanes = slice(it * TQ, (it + 1) * TQ)
        for cblk in range(D_MODEL // 256):
            rows = slice(cblk * 256, (cblk + 1) * 256)
            y = _dot(w_ref[rows, 0:group_rows], oz_ref[0, 0, it])
            for gi in range(1, N_KV):
                y = y + _dot(w_ref[rows, gi * group_rows:(gi + 1) * group_rows], oz_ref[0, gi, it])
            store(rows, lanes, x_ref[0, rows, lanes] + gt_ref[0, rows, :] * y)


def _out_kernel(oz_ref, w_ref, x_ref, gt_ref, o_ref):
    def store(rows, lanes, new_x):
        o_ref[0, lanes, rows] = new_x.T

    _residual_update(oz_ref, w_ref, x_ref, gt_ref, store)


def _pending_specs():
    return [_tile_spec(GQA * HEAD_DIM),
            pl.BlockSpec((D_MODEL, D_MODEL), lambda b, s: (0, 0), pipeline_mode=pl.Buffered(1)),
            _bcol_spec(D_MODEL)]


def _out_call(pending, xT):
    oz, wT, gate = pending
    oz_spec, w_spec, gate_spec = _pending_specs()
    return pl.pallas_call(
        _out_kernel,
        grid=(BATCH, SEQ // TS),
        in_specs=[oz_spec, w_spec, _fm_spec(D_MODEL), gate_spec],
        out_specs=pl.BlockSpec((1, TS, D_MODEL), lambda b, s: (b, s, 0)),
        out_shape=jax.ShapeDtypeStruct((BATCH, SEQ, D_MODEL), F32),
        compiler_params=_cparams(("parallel", "parallel")),
        name="out_proj",
    )(oz, wT, xT, gate)


def _rope_tables(pos):
    inv_freq = ROPE_THETA ** (-jnp.arange(0, 2 * ROT_HALF, 2, dtype=F32) / (2 * ROT_HALF))
    ang = pos.astype(F32)[..., None] * inv_freq
    return jnp.cos(ang), jnp.sin(ang)


def _gate_perm():
    perm = np.full((N_KV * GATE_ROWS,), 3 * N_HEADS, dtype=np.int32)
    for g in range(N_KV):
        for br in range(3):
            for r in range(GQA):
                perm[g * GATE_ROWS + br * GQA + r] = (g * GQA + r) * 3 + br
    return perm


def _sel_weights_T():
    cs = np.arange(N_CMP)[:, None] * CMP_STRIDE
    ss = np.arange(N_SEL)[None, :] * SEL_BLOCK
    shared = np.clip(np.minimum(cs + CMP_LEN, ss + SEL_BLOCK) - np.maximum(cs, ss), 0, None)
    w = np.zeros((N_CMP_PAD, N_SEL), np.float32)
    w[:N_CMP] = shared / CMP_LEN
    return jnp.asarray(w.T, dtype=BF16)


def _col(v):
    return v.reshape(-1, 1)


def _input_args(x, pending):
    if pending is None:
        return (x,), "tokens"
    return tuple(pending) + (x,), "pending"


def _nsa_layer(x, pending, ng, sc, sh, cosT, sinT, cosc, sinc, w_in, q_g, k_g,
               cmp_pe, cmp_w1, cmp_b1, cmp_w2, gate_b):
    widths = [1024] + [256] * 6 + [3 * N_HEADS, 1024]
    q, kc, vc, ks, vs, kw, vw, gl, z = jnp.split(w_in, np.cumsum(widths)[:-1].tolist(), axis=1)
    perm = _gate_perm()
    gl_p = jnp.concatenate([gl, jnp.zeros((D_MODEL, 1), F32)], axis=1)[:, perm]
    gb_p = jnp.concatenate([gate_b, jnp.zeros((1,), F32)])[perm]
    wT = jnp.concatenate([q, ks, kw, kc, vc, vs, vw, gl_p, z], axis=1).T.astype(BF16)

    x_args, source = _input_args(x, pending)
    qT, ks_c, kw_c, kc_c, vc_c, vs_c, vw_c, gates, sz, xT = _nsa_in_call(
        x_args, ng, sc, sh, wT, _col(q_g), _col(k_g[1]), _col(k_g[2]), _col(gb_p), cosT, sinT, source)

    half_blocks = lambda t: t.reshape(BATCH * N_KV, SEQ, HEAD_DIM)
    k_cmp = _compress_call(half_blocks(kc_c), cmp_pe[0], cmp_w1[0], cmp_b1[0], cmp_w2[0],
                           key_extras=(_col(k_g[0]), cosc, sinc))
    v_cmpT = _compress_call(half_blocks(vc_c), cmp_pe[1], cmp_w1[1], cmp_b1[1], cmp_w2[1])
    k_cmp = k_cmp.reshape(BATCH, N_KV, N_CMP_PAD, HEAD_DIM)
    v_cmpT = v_cmpT.reshape(BATCH, N_KV, HEAD_DIM, N_CMP_PAD)

    oz = _nsa_attn_call(qT, k_cmp, v_cmpT, ks_c, vs_c, kw_c, vw_c, gates, sz, _sel_weights_T())
    return oz, xT


def _moba_layer(x, pending, ng, sc, sh, cosT, sinT, w_in, q_g, k_g):
    x_args, source = _input_args(x, pending)
    qT, k_c, km, v_c, sz, xT = _moba_in_call(
        x_args, ng, sc, sh, w_in.T.astype(BF16), _col(q_g), _col(k_g), cosT, sinT, source)
    nb = TS // TQ
    km = km[..., :nb].reshape(BATCH, SEQ // TS, N_KV, HEAD_DIM, nb)
    km = km.transpose(0, 2, 1, 4, 3).reshape(BATCH, N_KV, N_CHUNK, HEAD_DIM).astype(BF16)
    return _moba_attn_call(qT, km, k_c, v_c, sz), xT


@jax.jit
def _forward(x, c, positions, norm_g, ada_w, ada_b, nsa_w_in, nsa_w_out, nsa_q_norm, nsa_k_norm,
             nsa_cmp_pe, nsa_cmp_w1, nsa_cmp_b1, nsa_cmp_w2, nsa_gate_b,
             moba_w_in, moba_w_out, moba_q_norm, moba_k_norm):
    cos, sin = _rope_tables(positions)
    cosT = cos.transpose(0, 2, 1)
    sinT = sin.transpose(0, 2, 1)
    cmp_end = np.minimum(np.arange(N_CMP_PAD) * CMP_STRIDE + CMP_LEN - 1, SEQ - 1)
    cos_c, sin_c = _rope_tables(positions[:, cmp_end])

    def cmp_table(t):
        t = jnp.broadcast_to(t.transpose(2, 0, 1)[:, :, None, :],
                             (ROT_HALF, BATCH, N_KV, N_CMP_PAD))
        return t.reshape(ROT_HALF, CMP_COLS)

    cosc = cmp_table(cos_c)
    sinc = cmp_table(sin_c)

    mod = _ada_call(c, ada_w, ada_b)
    shift, scale, gate = jnp.split(mod[..., None], 3, axis=2)

    pending = None
    for i in range(DEPTH):
        j = i // 2
        ng = _col(norm_g[i])
        if i % 2 == 0:
            oz, x = _nsa_layer(x, pending, ng, scale[i], shift[i], cosT, sinT, cosc, sinc,
                               nsa_w_in[j], nsa_q_norm[j], nsa_k_norm[j], nsa_cmp_pe[j],
                               nsa_cmp_w1[j], nsa_cmp_b1[j], nsa_cmp_w2[j], nsa_gate_b[j])
            w_out = nsa_w_out[j]
        else:
            oz, x = _moba_layer(x, pending, ng, scale[i], shift[i], cosT, sinT,
                                moba_w_in[j], moba_q_norm[j], moba_k_norm[j])
            w_out = moba_w_out[j]
        pending = (oz, w_out.T.astype(BF16), gate[i])
    return _out_call(pending, x)


def kernel(x, c, positions, norm_g, ada_w, ada_b, nsa_w_in, nsa_w_out, nsa_q_norm, nsa_k_norm, nsa_cmp_pe, nsa_cmp_w1, nsa_cmp_b1, nsa_cmp_w2, nsa_gate_b, moba_w_in, moba_w_out, moba_q_norm, moba_k_norm):
    return _forward(x, c, positions, norm_g, ada_w, ada_b, nsa_w_in, nsa_w_out, nsa_q_norm,
                    nsa_k_norm, nsa_cmp_pe, nsa_cmp_w1, nsa_cmp_b1, nsa_cmp_w2, nsa_gate_b,
                    moba_w_in, moba_w_out, moba_q_norm, moba_k_norm)
```
